```python
import math
import jax, jax.numpy as jnp
from jax import lax
import numpy as np

D_MODEL = 1024
BATCH = 8
SEQ = 4096
DEPTH = 1

CHUNK = 64
Q_BLOCK = 128
N_MEM = 256
HEAD_DIM = 64
A_HEADS = 4
A_QK_DIM = HEAD_DIM
A_V_DIM = 2 * HEAD_DIM
B_HEADS = 8
B_DIM = HEAD_DIM
B_LEFT_CHUNKS = 8
B_BAND = B_LEFT_CHUNKS + 1
REL_CLIP = 128
C_HEADS = 4
C_DIM = 128
D_FF = 2816
ROPE_THETA = 10000.0
EPS = 1e-6
NEG = -1e30
N_BRANCH = 3

A_Q = A_HEADS * 2 * A_QK_DIM
A_V = A_HEADS * A_V_DIM
B_W = B_HEADS * B_DIM
C_W = C_HEADS * C_DIM
BRANCH_W = 512
IN_COLS = 2 * A_Q + A_V + 3 * B_W + C_W

kernel_name = "hybrid_streaming_diff_chunk_mem_layer"


def rmsnorm(x, g):
    xf = x.astype(jnp.float32)
    y = xf * lax.rsqrt(jnp.mean(xf * xf, axis=-1, keepdims=True) + EPS)
    return (y * g.astype(jnp.float32)).astype(x.dtype)


def rope(x):
    s, d = x.shape[1], x.shape[-1]
    half = d // 2
    inv = ROPE_THETA ** (-jnp.arange(half, dtype=jnp.float32) / half)
    ang = jnp.arange(s, dtype=jnp.float32)[:, None] * inv[None, :]
    shp = (1, s) + (1,) * (x.ndim - 3) + (half,)
    cos, sin = jnp.cos(ang).reshape(shp), jnp.sin(ang).reshape(shp)
    xf = x.astype(jnp.float32)
    x1, x2 = xf[..., :half], xf[..., half:]
    return jnp.concatenate([x1 * cos - x2 * sin, x2 * cos + x1 * sin], axis=-1).astype(x.dtype)


def swiglu(h, wg, wu, wd):
    return (jax.nn.silu(h @ wg) * (h @ wu)) @ wd


def diff_attention(q, k, v, lam, lambda_init, sub_gain):
    b, s, h = q.shape[0], q.shape[1], q.shape[2]
    scale = A_QK_DIM ** -0.5
    chunk_id = jnp.arange(s) // CHUNK
    outs = []
    for i in range(s // Q_BLOCK):
        q0, q1 = i * Q_BLOCK, (i + 1) * Q_BLOCK
        kb, vb = k[:, :q1], v[:, :q1]
        sc = jnp.einsum('bqhcd,bkhcd->bhcqk', q[:, q0:q1], kb).astype(jnp.float32) * scale
        mask = chunk_id[q0:q1, None] >= chunk_id[None, :q1]
        sc = jnp.where(mask[None, None, None], sc, NEG)
        p = jax.nn.softmax(sc, axis=-1)
        a = p[:, :, 0] - lam * p[:, :, 1]
        outs.append(jnp.einsum('bhqk,bkhd->bqhd', a.astype(v.dtype), vb))
    o = jnp.concatenate(outs, axis=1)
    o = rmsnorm(o, sub_gain) * (1.0 - lambda_init)
    return o.reshape(b, s, h * A_V_DIM)


def chunk_band_attention(q, k, v, rel_table):
    b, s, h, d = q.shape
    nc = s // CHUNK
    scale = d ** -0.5
    qc = q.reshape(b, nc, CHUNK, h, d)
    pad = ((0, 0), (B_LEFT_CHUNKS, 0), (0, 0), (0, 0), (0, 0))
    kp = jnp.pad(k.reshape(b, nc, CHUNK, h, d), pad)
    vp = jnp.pad(v.reshape(b, nc, CHUNK, h, d), pad)
    band_idx = jnp.arange(nc)[:, None] + jnp.arange(B_BAND)[None, :]
    kb = kp[:, band_idx].reshape(b, nc, B_BAND * CHUNK, h, d)
    vb = vp[:, band_idx].reshape(b, nc, B_BAND * CHUNK, h, d)
    sc = jnp.einsum('bcqhd,bckhd->bchqk', qc, kb).astype(jnp.float32) * scale
    qpos = jnp.arange(CHUNK) + B_LEFT_CHUNKS * CHUNK
    kpos = jnp.arange(B_BAND * CHUNK)
    rel = jnp.clip(qpos[:, None] - kpos[None, :], -REL_CLIP, REL_CLIP) + REL_CLIP
    sc = sc + rel_table.astype(jnp.float32)[:, rel][None, None]
    valid = jnp.repeat(band_idx >= B_LEFT_CHUNKS, CHUNK, axis=1)
    sc = jnp.where(valid[None, :, None, None, :], sc, NEG)
    p = jax.nn.softmax(sc, axis=-1)
    o = jnp.einsum('bchqk,bckhd->bcqhd', p.astype(v.dtype), vb)
    return o.reshape(b, s, h * d)


def memory_attention(q, mk, mv):
    b, s, h, d = q.shape
    sc = jnp.einsum('bshd,bmhd->bhsm', q, mk).astype(jnp.float32) * (d ** -0.5)
    p = jax.nn.softmax(sc, axis=-1)
    return jnp.einsum('bhsm,bmhd->bshd', p.astype(mv.dtype), mv).reshape(b, s, h * d)


def setup_inputs(seed: int = 0) -> dict:
    key = jax.random.key(seed)
    ks = iter(jax.random.split(key, 40))
    L, D, F = DEPTH, D_MODEL, D_FF

    def w(shape, fan_in):
        return jax.random.normal(next(ks), shape, jnp.float32) * (fan_in ** -0.5)

    def gain(shape):
        return 1.0 + 0.01 * jax.random.normal(next(ks), shape, jnp.float32)

    def small(shape, s):
        return s * jax.random.normal(next(ks), shape, jnp.float32)

    return {
        "x": jax.random.normal(next(ks), (BATCH, SEQ, D), jnp.float32),
        "mem": jax.random.normal(next(ks), (BATCH, N_MEM, D), jnp.float32),
        "ffn1_norm": gain((L, D)),
        "ffn1_wg": w((L, D, F), D),
        "ffn1_wu": w((L, D, F), D),
        "ffn1_wd": w((L, F, D), F),
        "mix_norm": gain((L, D)),
        "w_in": w((L, D, IN_COLS), D),
        "a_q_norm": gain((L, A_QK_DIM)),
        "a_k_norm": gain((L, A_QK_DIM)),
        "a_lambda": small((L, 4, A_QK_DIM), 0.1),
        "a_sub_norm": gain((L, A_V_DIM)),
        "b_q_norm": gain((L, B_DIM)),
        "b_k_norm": gain((L, B_DIM)),
        "b_rel_bias": small((L, B_HEADS, 2 * REL_CLIP + 1), 0.1),
        "mem_norm": gain((L, D)),
        "w_mem_kv": w((L, D, 2 * C_W), D),
        "c_q_norm": gain((L, C_DIM)),
        "c_k_norm": gain((L, C_DIM)),
        "w_gate": w((L, D, N_BRANCH * D), D),
        "b_gate": small((L, N_BRANCH * D), 0.01),
        "w_branch": w((L, N_BRANCH, BRANCH_W, D), BRANCH_W),
        "w_out": w((L, D, D), D),
        "ffn2_norm": gain((L, D)),
        "ffn2_wg": w((L, D, F), D),
        "ffn2_wu": w((L, D, F), D),
        "ffn2_wd": w((L, F, D), F),
        "final_norm": gain((L, D)),
    }


def reference(x, mem, ffn1_norm, ffn1_wg, ffn1_wu, ffn1_wd, mix_norm, w_in,
              a_q_norm, a_k_norm, a_lambda, a_sub_norm, b_q_norm, b_k_norm, b_rel_bias,
              mem_norm, w_mem_kv, c_q_norm, c_k_norm, w_gate, b_gate, w_branch, w_out,
              ffn2_norm, ffn2_wg, ffn2_wu, ffn2_wd, final_norm):
    b, s, d = x.shape
    m = mem.shape[1]
    split_at = list(np.cumsum([A_Q, A_Q, A_V, B_W, B_W, B_W]))
    for l in range(DEPTH):
        x = x + 0.5 * swiglu(rmsnorm(x, ffn1_norm[l]), ffn1_wg[l], ffn1_wu[l], ffn1_wd[l])

        h = rmsnorm(x, mix_norm[l])
        proj = h @ w_in[l]
        aq, ak, av, bq, bk, bv, cq = jnp.split(proj, [int(i) for i in split_at], axis=-1)

        lambda_init = 0.8 - 0.6 * math.exp(-0.3 * l)
        lp = a_lambda[l].astype(jnp.float32)
        lam = jnp.exp(jnp.sum(lp[0] * lp[1])) - jnp.exp(jnp.sum(lp[2] * lp[3])) + lambda_init
        aq = rope(rmsnorm(aq.reshape(b, s, A_HEADS, 2, A_QK_DIM), a_q_norm[l]))
        ak = rope(rmsnorm(ak.reshape(b, s, A_HEADS, 2, A_QK_DIM), a_k_norm[l]))
        av = av.reshape(b, s, A_HEADS, A_V_DIM)
        o_a = diff_attention(aq, ak, av, lam, lambda_init, a_sub_norm[l])

        bq = rmsnorm(bq.reshape(b, s, B_HEADS, B_DIM), b_q_norm[l])
        bk = rmsnorm(bk.reshape(b, s, B_HEADS, B_DIM), b_k_norm[l])
        bv = bv.reshape(b, s, B_HEADS, B_DIM)
        o_b = chunk_band_attention(bq, bk, bv, b_rel_bias[l])

        mkv = rmsnorm(mem, mem_norm[l]) @ w_mem_kv[l]
        mk, mv = jnp.split(mkv, 2, axis=-1)
        mk = rmsnorm(mk.reshape(b, m, C_HEADS, C_DIM), c_k_norm[l])
        mv = mv.reshape(b, m, C_HEADS, C_DIM)
        cq = rmsnorm(cq.reshape(b, s, C_HEADS, C_DIM), c_q_norm[l])
        o_c = memory_attention(cq, mk, mv)

        gates = jax.nn.sigmoid(h @ w_gate[l] + b_gate[l]).reshape(b, s, N_BRANCH, d)
        branches = jnp.stack([o_a, o_b, o_c], axis=2)
        y = jnp.einsum('bsnc,ncd,bsnd->bsd', branches, w_branch[l], gates)
        x = x + y @ w_out[l]

        x = x + 0.5 * swiglu(rmsnorm(x, ffn2_norm[l]), ffn2_wg[l], ffn2_wu[l], ffn2_wd[l])
        x = rmsnorm(x, final_norm[l])
    return x
```

```python
import functools
import math

import jax
import jax.numpy as jnp
from jax import lax
from jax.experimental import pallas as pl
from jax.experimental.pallas import tpu as pltpu

F32 = jnp.float32
BF16 = jnp.bfloat16

CHUNK = 64
HEAD_DIM = 64
A_HEADS = 4
B_HEADS = 8
B_LEFT_CHUNKS = 8
REL_CLIP = 128
C_HEADS = 4
C_DIM = 128
N_BRANCH = 3
BRANCH_W = 512
ROPE_THETA = 10000.0
EPS = 1e-6
NEG = -1e30

LANES = 128
MXU_TILE = 256
TOKEN_TILE = 512
A_BLOCK = 256
B_BLOCK = B_LEFT_CHUNKS * CHUNK
C_BLOCK = 1024
VMEM_LIMIT = 56 * 1024 * 1024


def _dot(a, b):
    return jnp.dot(a, b, preferred_element_type=F32)


def _dot_nt(a, b):
    return lax.dot_general(a, b, (((1,), (1,)), ((), ())), preferred_element_type=F32)


def _rms(x, g):
    return x * lax.rsqrt(jnp.mean(x * x, axis=-1, keepdims=True) + EPS) * g


def _group_rms(z, gmat_ref, gain):
    z2 = (z * z).astype(BF16)
    ms = jnp.concatenate(
        [_dot(z2[:, :MXU_TILE], gmat_ref[...]), _dot(z2[:, MXU_TILE:], gmat_ref[...])], axis=1)
    return z * lax.rsqrt(ms + EPS) * gain


def _const_spec(shape):
    nd = len(shape)
    return pl.BlockSpec(shape, lambda *_: (0,) * nd, pipeline_mode=pl.Buffered(1))


def _params(n_axes):
    return pltpu.CompilerParams(
        dimension_semantics=("arbitrary",) * n_axes, vmem_limit_bytes=VMEM_LIMIT)


def _ffn_chunks(d_ff):
    step = 3 * MXU_TILE
    return [(c, min(c + step, d_ff)) for c in range(0, d_ff, step)]


def _ffn_kernel(x_ref, g_ref, wg_ref, wu_ref, wd_ref, *rest, final_norm):
    if final_norm:
        fg_ref, o_ref, act_ref = rest
    else:
        o_ref, act_ref = rest
    x = x_ref[...]
    h = _rms(x, g_ref[...]).astype(BF16)
    for c0, c1 in _ffn_chunks(wg_ref.shape[1]):
        g = _dot(h, wg_ref[:, c0:c1])
        u = _dot(h, wu_ref[:, c0:c1])
        act_ref[:, c0:c1] = (g * (1.0 / (1.0 + jnp.exp(-g))) * u).astype(BF16)
    out = x + 0.5 * _dot(act_ref[...], wd_ref[...])
    if final_norm:
        out = _rms(out, fg_ref[...])
    o_ref[...] = out


def _ffn(x, g, wg, wu, wd, final_g=None):
    t, d = x.shape
    f = wg.shape[1]
    tm = TOKEN_TILE
    in_specs = [pl.BlockSpec((tm, d), lambda i: (i, 0)), _const_spec((1, d)),
                _const_spec((d, f)), _const_spec((d, f)), _const_spec((f, d))]
    args = [x, g, wg, wu, wd]
    if final_g is not None:
        in_specs.append(_const_spec((1, d)))
        args.append(final_g)
    return pl.pallas_call(
        functools.partial(_ffn_kernel, final_norm=final_g is not None),
        grid=(t // tm,),
        in_specs=in_specs,
        out_specs=pl.BlockSpec((tm, d), lambda i: (i, 0)),
        out_shape=jax.ShapeDtypeStruct((t, d), F32),
        scratch_shapes=[pltpu.VMEM((tm, f), BF16)],
        compiler_params=_params(1),
        name="ffn_final" if final_g is not None else "ffn",
    )(*args)


def _rope(q, cos, sin):
    lane = lax.broadcasted_iota(jnp.int32, (q.shape[0], LANES), 1)
    low_half = (lane & 32) == 0
    outs = []
    for c in range(q.shape[1] // LANES):
        xc = q[:, c * LANES:(c + 1) * LANES]
        partner = jnp.where(low_half, pltpu.roll(xc, LANES - 32, 1), pltpu.roll(xc, 32, 1))
        outs.append(xc * cos + partner * sin)
    return jnp.concatenate(outs, axis=1)


def _in_kernel(x_ref, g_ref, w_ref, cos_ref, sin_ref, gaq_ref, gak_ref, gbq_ref, gbk_ref,
               gcq_ref, g64_ref, g128_ref,
               aq_ref, ak_ref, av_ref, bq_ref, bk_ref, bv_ref, cq_ref):
    h = _rms(x_ref[...], g_ref[...]).astype(BF16)
    w = BRANCH_W

    def seg(i):
        return _dot(h, w_ref[:, i * w:(i + 1) * w])

    cos = cos_ref[...]
    sin = sin_ref[...]
    aq_ref[...] = (_rope(_group_rms(seg(0), g64_ref, gaq_ref[...]), cos, sin)
                   * (HEAD_DIM ** -0.5)).astype(BF16)
    ak_ref[...] = _rope(_group_rms(seg(1), g64_ref, gak_ref[...]), cos, sin).astype(BF16)
    av_ref[...] = seg(2).astype(BF16)
    bq_ref[...] = (_group_rms(seg(3), g64_ref, gbq_ref[...]) * (HEAD_DIM ** -0.5)).astype(BF16)
    bk_ref[...] = _group_rms(seg(4), g64_ref, gbk_ref[...]).astype(BF16)
    bv_ref[...] = seg(5).astype(BF16)
    cq_ref[...] = (_group_rms(seg(6), g128_ref, gcq_ref[...]) * (C_DIM ** -0.5)).astype(BF16)


def _in_proj(x, g, w_in, cos, sin, gaq, gak, gbq, gbk, gcq, g64, g128, seq):
    t, d = x.shape
    tm = TOKEN_TILE
    n_seq_tiles = seq // tm
    w = BRANCH_W
    tile = lambda i: (i, 0)
    in_specs = [pl.BlockSpec((tm, d), tile), _const_spec((1, d)), _const_spec(w_in.shape),
                pl.BlockSpec((tm, LANES), lambda i: (i % n_seq_tiles, 0)),
                pl.BlockSpec((tm, LANES), lambda i: (i % n_seq_tiles, 0))]
    in_specs += [_const_spec((1, w))] * 5
    in_specs += [_const_spec((MXU_TILE, MXU_TILE))] * 2
    return pl.pallas_call(
        _in_kernel,
        grid=(t // tm,),
        in_specs=in_specs,
        out_specs=[pl.BlockSpec((tm, w), tile)] * 7,
        out_shape=[jax.ShapeDtypeStruct((t, w), BF16)] * 7,
        compiler_params=_params(1),
        name="in_proj",
    )(x, g, w_in, cos, sin, gaq, gak, gbq, gbk, gcq, g64, g128)


def _mem_kernel(m_ref, g_ref, w_ref, gk_ref, g128_ref, mk_ref, mv_ref):
    h = _rms(m_ref[...], g_ref[...]).astype(BF16)
    w = BRANCH_W
    mk_ref[...] = _group_rms(_dot(h, w_ref[:, :w]), g128_ref, gk_ref[...]).astype(BF16)
    mv_ref[...] = _dot(h, w_ref[:, w:]).astype(BF16)


def _mem_proj(mem, g, w_kv, gk, g128, n_mem):
    t, d = mem.shape
    w = BRANCH_W
    tile = lambda i: (i, 0)
    return pl.pallas_call(
        _mem_kernel,
        grid=(t // n_mem,),
        in_specs=[pl.BlockSpec((n_mem, d), tile), _const_spec((1, d)), _const_spec(w_kv.shape),
                  _const_spec((1, w)), _const_spec((MXU_TILE, MXU_TILE))],
        out_specs=[pl.BlockSpec((n_mem, w), tile)] * 2,
        out_shape=[jax.ShapeDtypeStruct((t, w), BF16)] * 2,
        compiler_params=_params(1),
        name="mem_proj",
    )(mem, g, w_kv, gk, g128)


def _bias_kernel(r_ref, o_ref):
    nq, nk = o_ref.shape[1], o_ref.shape[2]
    t = pltpu.roll(jnp.broadcast_to(r_ref[0], (nq, nk)), 0, 1, stride=1, stride_axis=0)
    qc = lax.broadcasted_iota(jnp.int32, (nq, nk), 0) // CHUNK + B_LEFT_CHUNKS
    kc = lax.broadcasted_iota(jnp.int32, (nq, nk), 1) // CHUNK
    valid = (qc >= kc) & (qc - kc <= B_LEFT_CHUNKS)
    o_ref[0] = jnp.where(valid, t, NEG)


def _rel_bias(rows):
    h, _, nk = rows.shape
    return pl.pallas_call(
        _bias_kernel,
        grid=(h,),
        in_specs=[pl.BlockSpec((1, 1, nk), lambda i: (i, 0, 0))],
        out_specs=pl.BlockSpec((1, B_BLOCK, nk), lambda i: (i, 0, 0)),
        out_shape=jax.ShapeDtypeStruct((h, B_BLOCK, nk), F32),
        compiler_params=_params(1),
        name="rel_bias",
    )(rows)


def _attn_a_kernel(q_ref, k_ref, v_ref, lam_ref, gain_ref, o_ref, *, lambda_init):
    tq = q_ref.shape[1]
    qi = pl.program_id(2)
    q = q_ref[0]
    lane = lax.broadcasted_iota(jnp.int32, q.shape, 1)
    zero = jnp.zeros_like(q)
    qs = (jnp.where(lane < HEAD_DIM, q, zero), jnp.where(lane >= HEAD_DIM, q, zero))

    def update(state, kb, vb, mask):
        new = []
        for c in range(2):
            m, l, acc = state[c]
            s = _dot_nt(qs[c], kb)
            if mask is not None:
                s = jnp.where(mask, s, NEG)
            m_new = jnp.maximum(m, jnp.max(s, axis=-1, keepdims=True))
            alpha = jnp.exp(m - m_new)
            p = jnp.exp(s - m_new)
            l = alpha * l + jnp.sum(p, axis=-1, keepdims=True)
            acc = alpha * acc + _dot(p.astype(BF16), vb)
            new.append((m_new, l, acc))
        return tuple(new)

    def body(j, state):
        start = pl.multiple_of(j * tq, tq)
        return update(state, k_ref[0, pl.ds(start, tq), :], v_ref[0, pl.ds(start, tq), :], None)

    init_one = (jnp.full((tq, 1), NEG, F32), jnp.zeros((tq, 1), F32), jnp.zeros((tq, LANES), F32))
    state = lax.fori_loop(0, qi, body, (init_one, init_one))

    start = pl.multiple_of(qi * tq, tq)
    row_chunk = lax.broadcasted_iota(jnp.int32, (tq, tq), 0) // CHUNK
    col_chunk = lax.broadcasted_iota(jnp.int32, (tq, tq), 1) // CHUNK
    state = update(state, k_ref[0, pl.ds(start, tq), :], v_ref[0, pl.ds(start, tq), :],
                   row_chunk >= col_chunk)

    lp = lam_ref[...]
    lam = (jnp.exp(jnp.sum(lp[0:1] * lp[1:2], axis=1, keepdims=True))
           - jnp.exp(jnp.sum(lp[2:3] * lp[3:4], axis=1, keepdims=True)) + lambda_init)
    (_, l1, acc1), (_, l2, acc2) = state
    o = acc1 / l1 - lam * (acc2 / l2)
    o_ref[0] = (_rms(o, gain_ref[...]) * (1.0 - lambda_init)).astype(BF16)


def _attn_a(q, k, v, lam_p, gain, lambda_init):
    b, s, w = q.shape
    tq = A_BLOCK
    kv_spec = pl.BlockSpec((1, s, LANES), lambda bi, hi, qi: (bi, 0, hi))
    q_spec = pl.BlockSpec((1, tq, LANES), lambda bi, hi, qi: (bi, qi, hi))
    return pl.pallas_call(
        functools.partial(_attn_a_kernel, lambda_init=lambda_init),
        grid=(b, w // LANES, s // tq),
        in_specs=[q_spec, kv_spec, kv_spec, _const_spec(lam_p.shape), _const_spec((1, LANES))],
        out_specs=q_spec,
        out_shape=jax.ShapeDtypeStruct((b, s, w), BF16),
        compiler_params=_params(3),
        name="attn_a",
    )(q, k, v, lam_p, gain)


def _attn_b_kernel(q_ref, kp_ref, ko_ref, vp_ref, vo_ref, bias_ref, o_ref):
    qi = pl.program_id(2)
    q = q_ref[0]
    tq = q.shape[0]
    k = jnp.concatenate([kp_ref[0], ko_ref[0]], axis=0)
    v = jnp.concatenate([vp_ref[0], vo_ref[0]], axis=0)
    lane = lax.broadcasted_iota(jnp.int32, q.shape, 1)
    col = lax.broadcasted_iota(jnp.int32, (tq, 2 * tq), 1)
    has_prev = (col >= tq) | (qi > 0)
    zero = jnp.zeros_like(q)
    outs = []
    for hh in range(2):
        in_head = (lane < HEAD_DIM) if hh == 0 else (lane >= HEAD_DIM)
        s = _dot_nt(jnp.where(in_head, q, zero), k) + bias_ref[hh]
        s = jnp.where(has_prev, s, NEG)
        m = jnp.max(s, axis=-1, keepdims=True)
        p = jnp.exp(s - m)
        l = jnp.sum(p, axis=-1, keepdims=True)
        outs.append(_dot(p.astype(BF16), v) / l)
    o_ref[0] = jnp.where(lane < HEAD_DIM, outs[0], outs[1]).astype(BF16)


def _attn_b(q, k, v, bias):
    b, s, w = q.shape
    tq = B_BLOCK
    own = pl.BlockSpec((1, tq, LANES), lambda pi, bi, qi: (bi, qi, pi))
    prev = pl.BlockSpec((1, tq, LANES), lambda pi, bi, qi: (bi, jnp.maximum(qi - 1, 0), pi))
    return pl.pallas_call(
        _attn_b_kernel,
        grid=(w // LANES, b, s // tq),
        in_specs=[own, prev, own, prev, own,
                  pl.BlockSpec((2, tq, 2 * tq), lambda pi, bi, qi: (pi, 0, 0))],
        out_specs=own,
        out_shape=jax.ShapeDtypeStruct((b, s, w), BF16),
        compiler_params=_params(3),
        name="attn_b",
    )(q, k, k, v, v, bias)


def _attn_c_kernel(q_ref, k_ref, v_ref, o_ref):
    s = _dot_nt(q_ref[0], k_ref[0])
    m = jnp.max(s, axis=-1, keepdims=True)
    p = jnp.exp(s - m)
    l = jnp.sum(p, axis=-1, keepdims=True)
    o_ref[0] = (_dot(p.astype(BF16), v_ref[0]) / l).astype(BF16)


def _attn_c(q, mk, mv):
    b, s, w = q.shape
    n_mem = mk.shape[1]
    tq = C_BLOCK
    q_spec = pl.BlockSpec((1, tq, LANES), lambda bi, hi, qi: (bi, qi, hi))
    m_spec = pl.BlockSpec((1, n_mem, LANES), lambda bi, hi, qi: (bi, 0, hi))
    return pl.pallas_call(
        _attn_c_kernel,
        grid=(b, w // LANES, s // tq),
        in_specs=[q_spec, m_spec, m_spec],
        out_specs=q_spec,
        out_shape=jax.ShapeDtypeStruct((b, s, w), BF16),
        compiler_params=_params(3),
        name="attn_c",
    )(q, mk, mv)


def _mix_kernel(x_ref, g_ref, wgate_ref, bgate_ref, oa_ref, ob_ref, oc_ref, wbr_ref, wout_ref,
                o_ref):
    x = x_ref[...]
    d = x.shape[1]
    h = _rms(x, g_ref[...]).astype(BF16)
    y = None
    for n, br_ref in enumerate((oa_ref, ob_ref, oc_ref)):
        z = _dot(h, wgate_ref[:, n * d:(n + 1) * d]) + bgate_ref[:, n * d:(n + 1) * d]
        term = (1.0 / (1.0 + jnp.exp(-z))) * _dot(br_ref[...], wbr_ref[n])
        y = term if y is None else y + term
    o_ref[...] = x + _dot(y.astype(BF16), wout_ref[...])


def _mix_out(x, g, w_gate, b_gate, o_a, o_b, o_c, w_branch, w_out):
    t, d = x.shape
    tm = TOKEN_TILE
    tile = lambda i: (i, 0)
    br_spec = pl.BlockSpec((tm, BRANCH_W), tile)
    return pl.pallas_call(
        _mix_kernel,
        grid=(t // tm,),
        in_specs=[pl.BlockSpec((tm, d), tile), _const_spec((1, d)), _const_spec(w_gate.shape),
                  _const_spec(b_gate.shape), br_spec, br_spec, br_spec,
                  _const_spec(w_branch.shape), _const_spec(w_out.shape)],
        out_specs=pl.BlockSpec((tm, d), tile),
        out_shape=jax.ShapeDtypeStruct((t, d), F32),
        compiler_params=_params(1),
        name="mix_out",
    )(x, g, w_gate, b_gate, o_a, o_b, o_c, w_branch, w_out)


def _rope_tables(seq):
    half = HEAD_DIM // 2
    inv = ROPE_THETA ** (-jnp.arange(half, dtype=F32) / half)
    ang = jnp.arange(seq, dtype=F32)[:, None] * inv[None, :]
    cos, sin = jnp.cos(ang), jnp.sin(ang)
    reps = LANES // HEAD_DIM
    return (jnp.tile(jnp.concatenate([cos, cos], axis=1), (1, reps)),
            jnp.tile(jnp.concatenate([-sin, sin], axis=1), (1, reps)))


def _group_mean_matrix(group):
    idx = jnp.arange(MXU_TILE) // group
    return jnp.where(idx[:, None] == idx[None, :], 1.0 / group, 0.0).astype(BF16)


def _bias_rows(table):
    h = table.shape[0]
    far = jnp.broadcast_to(table[:, -1:], (h, B_BLOCK - REL_CLIP))
    near = table[:, ::-1]
    tail = jnp.broadcast_to(table[:, -1:], (h, 2 * B_BLOCK - far.shape[1] - near.shape[1]))
    return jnp.concatenate([far, near, tail], axis=1)[:, None, :]


def kernel(x, mem, ffn1_norm, ffn1_wg, ffn1_wu, ffn1_wd, mix_norm, w_in, a_q_norm, a_k_norm,
           a_lambda, a_sub_norm, b_q_norm, b_k_norm, b_rel_bias, mem_norm, w_mem_kv, c_q_norm,
           c_k_norm, w_gate, b_gate, w_branch, w_out, ffn2_norm, ffn2_wg, ffn2_wu, ffn2_wd,
           final_norm):
    b, s, d = x.shape
    n_mem = mem.shape[1]
    depth = ffn1_norm.shape[0]
    assert s % B_BLOCK == 0 and s % C_BLOCK == 0 and (b * s) % TOKEN_TILE == 0
    assert d % MXU_TILE == 0 and w_in.shape[2] == 7 * BRANCH_W

    cos, sin = _rope_tables(s)
    g64 = _group_mean_matrix(HEAD_DIM)
    g128 = _group_mean_matrix(C_DIM)
    row = lambda v: v.reshape(1, -1)
    heads = lambda v: jnp.tile(v, BRANCH_W // v.shape[0]).reshape(1, BRANCH_W)

    xt = x.reshape(b * s, d)
    memt = mem.reshape(b * n_mem, d)
    for l in range(depth):
        lambda_init = 0.8 - 0.6 * math.exp(-0.3 * l)
        xt = _ffn(xt, row(ffn1_norm[l]), ffn1_wg[l].astype(BF16), ffn1_wu[l].astype(BF16),
                  ffn1_wd[l].astype(BF16))

        aq, ak, av, bq, bk, bv, cq = _in_proj(
            xt, row(mix_norm[l]), w_in[l].astype(BF16), cos, sin,
            heads(a_q_norm[l]), heads(a_k_norm[l]), heads(b_q_norm[l]), heads(b_k_norm[l]),
            heads(c_q_norm[l]), g64, g128, s)
        mk, mv = _mem_proj(memt, row(mem_norm[l]), w_mem_kv[l].astype(BF16),
                           heads(c_k_norm[l]), g128, n_mem)
        bias = _rel_bias(_bias_rows(b_rel_bias[l]))

        seq3 = lambda a: a.reshape(b, s, BRANCH_W)
        o_a = _attn_a(seq3(aq), seq3(ak), seq3(av), a_lambda[l], row(a_sub_norm[l]), lambda_init)
        o_b = _attn_b(seq3(bq), seq3(bk), seq3(bv), bias)
        o_c = _attn_c(seq3(cq), mk.reshape(b, n_mem, BRANCH_W), mv.reshape(b, n_mem, BRANCH_W))

        flat = lambda a: a.reshape(b * s, BRANCH_W)
        xt = _mix_out(xt, row(mix_norm[l]), w_gate[l].astype(BF16), row(b_gate[l]),
                      flat(o_a), flat(o_b), flat(o_c), w_branch[l].astype(BF16),
                      w_out[l].astype(BF16))

        xt = _ffn(xt, row(ffn2_norm[l]), ffn2_wg[l].astype(BF16), ffn2_wu[l].astype(BF16),
                  ffn2_wd[l].astype(BF16), row(final_norm[l]))
    return xt.reshape(b, s, d)
```

```python
import functools
import math

import jax
import jax.numpy as jnp
from jax import lax
from jax.experimental import pallas as pl
from jax.experimental.pallas import tpu as pltpu

F32 = jnp.float32
BF16 = jnp.bfloat16

CHUNK = 64
HEAD_DIM = 64
A_HEADS = 4
B_HEADS = 8
B_LEFT_CHUNKS = 8
REL_CLIP = 128
C_HEADS = 4
C_DIM = 128
N_BRANCH = 3
BRANCH_W = 512
ROPE_THETA = 10000.0
EPS = 1e-6
NEG = -1e30
LOG2E = math.log2(math.e)

LANES = 128
MXU_TILE = 256
BF16_ROWS = 16
TOKEN_TILE = 512
KEY_PART = MXU_TILE
LOOKAHEAD = 2
ONES_ROWS = BF16_ROWS
VMEM_LIMIT = 56 * 1024 * 1024

A_V_ROWS = 2 * HEAD_DIM + ONES_ROWS
B_V_ROWS = HEAD_DIM + ONES_ROWS

assert TOKEN_TILE == B_LEFT_CHUNKS * CHUNK


def _dot(a, b):
    return jnp.dot(a, b, preferred_element_type=F32)


def _dot_nt(a, b):
    return lax.dot_general(a, b, (((1,), (1,)), ((), ())), preferred_element_type=F32)


def _rms(x, g):
    return x * lax.rsqrt(jnp.mean(x * x, axis=-1, keepdims=True) + EPS) * g


def _group_rms(z, gmat_ref, gain):
    z2 = (z * z).astype(BF16)
    ms = jnp.concatenate(
        [_dot(z2[:, :MXU_TILE], gmat_ref[...]), _dot(z2[:, MXU_TILE:], gmat_ref[...])], axis=1)
    return z * lax.rsqrt(ms + EPS) * gain


def _heads_t(z, gain, group, scale, cos=None, sin=None):
    outs = []
    for g0 in range(0, z.shape[0], group):
        zz = z[g0:g0 + group]
        r = lax.rsqrt(jnp.mean(zz * zz, axis=0, keepdims=True) + EPS) * scale
        y = zz * r * gain[g0:g0 + group]
        if cos is not None:
            y1, y2 = y[:group // 2], y[group // 2:]
            y = jnp.concatenate([y1 * cos - y2 * sin, y2 * cos + y1 * sin], axis=0)
        outs.append(y)
    return jnp.concatenate(outs, axis=0)


def _with_ones(vt, head_rows):
    ones = jnp.ones((ONES_ROWS, vt.shape[1]), vt.dtype)
    parts = []
    for r0 in range(0, vt.shape[0], head_rows):
        parts += [vt[r0:r0 + head_rows], ones]
    return jnp.concatenate(parts, axis=0)


def _const_spec(shape):
    nd = len(shape)
    return pl.BlockSpec(shape, lambda *_: (0,) * nd, pipeline_mode=pl.Buffered(1))


def _params(n_axes):
    return pltpu.CompilerParams(
        dimension_semantics=("arbitrary",) * n_axes, vmem_limit_bytes=VMEM_LIMIT)


def _ffn_chunks(d_ff):
    step = 3 * MXU_TILE
    return [(c, min(c + step, d_ff)) for c in range(0, d_ff, step)]


def _ffn_kernel(x_ref, g_ref, wg_ref, wu_ref, wd_ref, *rest, final_norm):
    if final_norm:
        fg_ref, o_ref, act_ref = rest
    else:
        o_ref, act_ref = rest
    x = x_ref[...]
    h = _rms(x, g_ref[...]).astype(BF16)
    for c0, c1 in _ffn_chunks(wg_ref.shape[1]):
        g = _dot(h, wg_ref[:, c0:c1])
        u = _dot(h, wu_ref[:, c0:c1])
        act_ref[:, c0:c1] = (g * (1.0 / (1.0 + jnp.exp(-g))) * u).astype(BF16)
    out = x + 0.5 * _dot(act_ref[...], wd_ref[...])
    if final_norm:
        out = _rms(out, fg_ref[...])
    o_ref[...] = out


def _ffn(x, g, wg, wu, wd, final_g=None):
    t, d = x.shape
    f = wg.shape[1]
    tm = TOKEN_TILE
    in_specs = [pl.BlockSpec((tm, d), lambda i: (i, 0)), _const_spec((1, d)),
                _const_spec((d, f)), _const_spec((d, f)), _const_spec((f, d))]
    args = [x, g, wg, wu, wd]
    if final_g is not None:
        in_specs.append(_const_spec((1, d)))
        args.append(final_g)
    return pl.pallas_call(
        functools.partial(_ffn_kernel, final_norm=final_g is not None),
        grid=(t // tm,),
        in_specs=in_specs,
        out_specs=pl.BlockSpec((tm, d), lambda i: (i, 0)),
        out_shape=jax.ShapeDtypeStruct((t, d), F32),
        scratch_shapes=[pltpu.VMEM((tm, f), BF16)],
        compiler_params=_params(1),
        name="ffn_final" if final_g is not None else "ffn",
    )(*args)


def _rope(q, cos, sin):
    lane = lax.broadcasted_iota(jnp.int32, (q.shape[0], LANES), 1)
    low_half = (lane & 32) == 0
    outs = []
    for c in range(q.shape[1] // LANES):
        xc = q[:, c * LANES:(c + 1) * LANES]
        partner = jnp.where(low_half, pltpu.roll(xc, LANES - 32, 1), pltpu.roll(xc, 32, 1))
        outs.append(xc * cos + partner * sin)
    return jnp.concatenate(outs, axis=1)


def _in_kernel(x_ref, g_ref, wt_ref, wk_ref, cos_ref, sin_ref, cost_ref, sint_ref,
               gaq_ref, gbq_ref, gcq_ref, gak_ref, gbk_ref, g64_ref,
               aqt_ref, avt_ref, bqt_ref, bvt_ref, cqt_ref, ak_ref, bk_ref):
    h = _rms(x_ref[...], g_ref[...]).astype(BF16)
    w = BRANCH_W

    def seg_t(i):
        return _dot_nt(wt_ref[i * w:(i + 1) * w, :], h)

    qk_scale = HEAD_DIM ** -0.5 * LOG2E
    aqt_ref[0] = _heads_t(seg_t(0), gaq_ref[...], HEAD_DIM, qk_scale,
                          cost_ref[...], sint_ref[...]).astype(BF16)
    avt_ref[0] = _with_ones(seg_t(1).astype(BF16), 2 * HEAD_DIM)
    bqt_ref[0] = _heads_t(seg_t(2), gbq_ref[...], HEAD_DIM, qk_scale).astype(BF16)
    bvt_ref[0] = _with_ones(seg_t(3).astype(BF16), HEAD_DIM)
    cqt_ref[0] = _heads_t(seg_t(4), gcq_ref[...], C_DIM, C_DIM ** -0.5 * LOG2E).astype(BF16)

    zk = _dot(h, wk_ref[:, :w])
    ak_ref[...] = _rope(_group_rms(zk, g64_ref, gak_ref[...]), cos_ref[...], sin_ref[...]
                        ).astype(BF16)
    bk_ref[...] = _group_rms(_dot(h, wk_ref[:, w:]), g64_ref, gbk_ref[...]).astype(BF16)


def _in_proj(x, g, w_t, w_k, tables, gaq, gbq, gcq, gak, gbk, g64, seq):
    t, d = x.shape
    tm = TOKEN_TILE
    n_tiles = t // tm
    n_seq_tiles = seq // tm
    w = BRANCH_W
    cos, sin, cos_t, sin_t = tables
    tile = lambda i: (i, 0)
    half = HEAD_DIM // 2
    in_specs = [pl.BlockSpec((tm, d), tile), _const_spec((1, d)), _const_spec(w_t.shape),
                _const_spec(w_k.shape),
                pl.BlockSpec((tm, LANES), lambda i: (i % n_seq_tiles, 0)),
                pl.BlockSpec((tm, LANES), lambda i: (i % n_seq_tiles, 0)),
                pl.BlockSpec((half, tm), lambda i: (0, i % n_seq_tiles)),
                pl.BlockSpec((half, tm), lambda i: (0, i % n_seq_tiles))]
    in_specs += [_const_spec((w, 1))] * 3 + [_const_spec((1, w))] * 2
    in_specs += [_const_spec((MXU_TILE, MXU_TILE))]
    t_rows = [w, A_HEADS * A_V_ROWS, w, B_HEADS * B_V_ROWS, w]
    return pl.pallas_call(
        _in_kernel,
        grid=(n_tiles,),
        in_specs=in_specs,
        out_specs=([pl.BlockSpec((1, r, tm), lambda i: (i, 0, 0)) for r in t_rows]
                   + [pl.BlockSpec((tm, w), tile)] * 2),
        out_shape=([jax.ShapeDtypeStruct((n_tiles, r, tm), BF16) for r in t_rows]
                   + [jax.ShapeDtypeStruct((t, w), BF16)] * 2),
        compiler_params=_params(1),
        name="in_proj",
    )(x, g, w_t, w_k, cos, sin, cos_t, sin_t, gaq, gbq, gcq, gak, gbk, g64)


def _mem_kernel(m_ref, g_ref, wk_ref, wvt_ref, gk_ref, g128_ref, mk_ref, mvt_ref):
    h = _rms(m_ref[...], g_ref[...]).astype(BF16)
    mk_ref[...] = _group_rms(_dot(h, wk_ref[...]), g128_ref, gk_ref[...]).astype(BF16)
    mvt_ref[0] = _with_ones(_dot_nt(wvt_ref[...], h).astype(BF16), C_DIM)


def _mem_proj(mem, g, w_k, w_vt, gk, g128, n_mem):
    t, d = mem.shape
    w = BRANCH_W
    v_rows = C_HEADS * A_V_ROWS
    tile = lambda i: (i, 0)
    return pl.pallas_call(
        _mem_kernel,
        grid=(t // n_mem,),
        in_specs=[pl.BlockSpec((n_mem, d), tile), _const_spec((1, d)), _const_spec(w_k.shape),
                  _const_spec(w_vt.shape), _const_spec((1, w)),
                  _const_spec((MXU_TILE, MXU_TILE))],
        out_specs=[pl.BlockSpec((n_mem, w), tile),
                   pl.BlockSpec((1, v_rows, n_mem), lambda i: (i, 0, 0))],
        out_shape=[jax.ShapeDtypeStruct((t, w), BF16),
                   jax.ShapeDtypeStruct((t // n_mem, v_rows, n_mem), BF16)],
        compiler_params=_params(1),
        name="mem_proj",
    )(mem, g, w_k, w_vt, gk, g128)


def _bias_kernel(r_ref, o_ref):
    nq = TOKEN_TILE
    nk = 2 * nq
    qc = lax.broadcasted_iota(jnp.int32, (nq, nk), 0) // CHUNK + B_LEFT_CHUNKS
    kc = lax.broadcasted_iota(jnp.int32, (nq, nk), 1) // CHUNK
    valid = (qc >= kc) & (qc - kc <= B_LEFT_CHUNKS)
    for hh in range(2):
        t = pltpu.roll(jnp.broadcast_to(r_ref[hh], (nq, nk)), 0, 1, stride=1, stride_axis=0)
        o_ref[0, :, hh * nq:(hh + 1) * nq] = jnp.where(valid, t * LOG2E, NEG).T


def _rel_bias(rows):
    h, _, nk = rows.shape
    return pl.pallas_call(
        _bias_kernel,
        grid=(h // 2,),
        in_specs=[pl.BlockSpec((2, 1, nk), lambda i: (i, 0, 0))],
        out_specs=pl.BlockSpec((1, nk, nk), lambda i: (i, 0, 0)),
        out_shape=jax.ShapeDtypeStruct((h // 2, nk, nk), F32),
        compiler_params=_params(1),
        name="rel_bias",
    )(rows)


def _split_queries(qt):
    row = lax.broadcasted_iota(jnp.int32, qt.shape, 0)
    zero = jnp.zeros_like(qt)
    return jnp.concatenate(
        [jnp.where(row < HEAD_DIM, qt, zero), jnp.where(row >= HEAD_DIM, qt, zero)], axis=1)


def _online_softmax(s, m_prev, acc_prev, vt):
    s_max = jnp.max(s, axis=0, keepdims=True)
    m_new = s_max if m_prev is None else jnp.maximum(m_prev, s_max)
    pv = _dot(vt, jnp.exp2(s - m_new).astype(BF16))
    if acc_prev is None:
        return m_new, pv
    return m_new, jnp.exp2(m_prev - m_new) * acc_prev + pv


def _normalized(acc, rows):
    return acc[:rows] / acc[rows:rows + 1]


def _attn_a_kernel(qt_ref, k_ref, vt_ref, lam_ref, gain_ref, o_ref, q12_ref, *scratch,
                   lambda_init):
    tq = qt_ref.shape[3]
    qi = pl.program_id(2)
    q12_ref[...] = _split_queries(qt_ref[0, 0])
    score_refs = scratch[:LOOKAHEAD]
    state_refs = [scratch[LOOKAHEAD + 2 * c:LOOKAHEAD + 2 * c + 2] for c in range(2)]
    for m_ref, acc_ref in state_refs:
        m_ref[...] = jnp.full(m_ref.shape, NEG, F32)
        acc_ref[...] = jnp.zeros(acc_ref.shape, F32)

    hk = KEY_PART
    items = [(kp, c) for kp in range(tq // hk) for c in range(2)]
    assert LOOKAHEAD <= len(items)

    def qk(j, item):
        kp, c = item
        kb = k_ref[0, pl.ds(pl.multiple_of(j * tq + kp * hk, hk), hk), :]
        return _dot(kb, q12_ref[:, c * tq:(c + 1) * tq])

    def block(j, own):
        scores = [r[...] for r in score_refs]
        for i, (kp, c) in enumerate(items):
            s = scores.pop(0)
            if own:
                key_chunk = (lax.broadcasted_iota(jnp.int32, (hk, tq), 0) + kp * hk) // CHUNK
                query_chunk = lax.broadcasted_iota(jnp.int32, (hk, tq), 1) // CHUNK
                s = jnp.where(key_chunk <= query_chunk, s, NEG)
            nxt = i + LOOKAHEAD
            if nxt < len(items):
                scores.append(qk(j, items[nxt]))
            elif not own:
                score_refs[nxt - len(items)][...] = qk(j + 1, items[nxt - len(items)])
            m_ref, acc_ref = state_refs[c]
            m_ref[...], acc_ref[...] = _online_softmax(
                s, m_ref[...], acc_ref[...], vt_ref[0, j, :, kp * hk:(kp + 1) * hk])

    for i, r in enumerate(score_refs):
        r[...] = qk(0, items[i])

    def body(j, carry):
        block(j, False)
        return carry

    lax.fori_loop(0, qi, body, 0)
    block(qi, True)

    lp = lam_ref[...]
    lam = (jnp.exp(jnp.sum(lp[0:1] * lp[1:2], axis=1, keepdims=True))
           - jnp.exp(jnp.sum(lp[2:3] * lp[3:4], axis=1, keepdims=True)) + lambda_init)
    o1, o2 = (_normalized(acc_ref[...], 2 * HEAD_DIM) for _, acc_ref in state_refs)
    ot = o1 - lam * o2
    ot = ot * lax.rsqrt(jnp.mean(ot * ot, axis=0, keepdims=True) + EPS)
    o_ref[0] = (ot.T * gain_ref[...] * (1.0 - lambda_init)).astype(BF16)


def _attn_a(qt, k, vt, lam_p, gain, lambda_init):
    b, n_blk, w, tq = qt.shape
    s = n_blk * tq
    return pl.pallas_call(
        functools.partial(_attn_a_kernel, lambda_init=lambda_init),
        grid=(b, w // LANES, n_blk),
        in_specs=[pl.BlockSpec((1, 1, LANES, tq), lambda bi, hi, qi: (bi, qi, hi, 0)),
                  pl.BlockSpec((1, s, LANES), lambda bi, hi, qi: (bi, 0, hi)),
                  pl.BlockSpec((1, n_blk, A_V_ROWS, tq), lambda bi, hi, qi: (bi, 0, hi, 0)),
                  _const_spec(lam_p.shape), _const_spec((1, LANES))],
        out_specs=pl.BlockSpec((1, tq, LANES), lambda bi, hi, qi: (bi, qi, hi)),
        out_shape=jax.ShapeDtypeStruct((b, s, w), BF16),
        scratch_shapes=(
            [pltpu.VMEM((LANES, 2 * tq), BF16)]
            + [pltpu.VMEM((KEY_PART, tq), F32)] * LOOKAHEAD
            + [pltpu.VMEM((1, tq), F32), pltpu.VMEM((A_V_ROWS, tq), F32)] * 2),
        compiler_params=_params(3),
        name="attn_a",
    )(qt, k, vt, lam_p, gain)


def _attn_b_kernel(qt_ref, kp_ref, ko_ref, vtp_ref, vto_ref, bias_ref, o_ref, q12_ref):
    tq = qt_ref.shape[3]
    qi = pl.program_id(2)
    q12_ref[...] = _split_queries(qt_ref[0, 0])
    hk = KEY_PART
    parts = ([(ko_ref, vto_ref, r0, tq + r0) for r0 in range(0, tq, hk)]
             + [(kp_ref, vtp_ref, r0, r0) for r0 in range(0, tq, hk)])
    items = [(part, c) for part in parts for c in range(2)]

    def qk(item):
        (k_ref, _, r0, w0), c = item
        cols = slice(c * tq, (c + 1) * tq)
        s = _dot(k_ref[0, r0:r0 + hk, :], q12_ref[:, cols]) + bias_ref[0, w0:w0 + hk, cols]
        if k_ref is kp_ref:
            s = jnp.where(qi > 0, s, NEG)
        return s

    scores = [qk(it) for it in items[:LOOKAHEAD]]
    state = [(None, None), (None, None)]
    for i, ((_, vt_ref, r0, _), c) in enumerate(items):
        s = scores.pop(0)
        if i + LOOKAHEAD < len(items):
            scores.append(qk(items[i + LOOKAHEAD]))
        vt = vt_ref[0, 0, c * B_V_ROWS:(c + 1) * B_V_ROWS, r0:r0 + hk]
        state[c] = _online_softmax(s, state[c][0], state[c][1], vt)
    ot = jnp.concatenate([_normalized(acc, HEAD_DIM) for _, acc in state], axis=0)
    o_ref[0] = ot.T.astype(BF16)


def _attn_b(qt, k, vt, bias):
    b, n_blk, w, tq = qt.shape
    s = n_blk * tq
    prev = lambda qi: jnp.maximum(qi - 1, 0)
    return pl.pallas_call(
        _attn_b_kernel,
        grid=(w // LANES, b, n_blk),
        in_specs=[pl.BlockSpec((1, 1, LANES, tq), lambda pi, bi, qi: (bi, qi, pi, 0)),
                  pl.BlockSpec((1, tq, LANES), lambda pi, bi, qi: (bi, prev(qi), pi)),
                  pl.BlockSpec((1, tq, LANES), lambda pi, bi, qi: (bi, qi, pi)),
                  pl.BlockSpec((1, 1, 2 * B_V_ROWS, tq), lambda pi, bi, qi: (bi, prev(qi), pi, 0)),
                  pl.BlockSpec((1, 1, 2 * B_V_ROWS, tq), lambda pi, bi, qi: (bi, qi, pi, 0)),
                  pl.BlockSpec((1, 2 * tq, 2 * tq), lambda pi, bi, qi: (pi, 0, 0))],
        out_specs=pl.BlockSpec((1, tq, LANES), lambda pi, bi, qi: (bi, qi, pi)),
        out_shape=jax.ShapeDtypeStruct((b, s, w), BF16),
        scratch_shapes=[pltpu.VMEM((LANES, 2 * tq), BF16)],
        compiler_params=_params(3),
        name="attn_b",
    )(qt, k, k, vt, vt, bias)


def _attn_c_kernel(qt_ref, k_ref, vt_ref, o_ref):
    def qk(hd):
        rows = slice(hd * C_DIM, (hd + 1) * C_DIM)
        return _dot(k_ref[0, :, rows], qt_ref[0, 0, rows, :])

    scores = [qk(hd) for hd in range(LOOKAHEAD)]
    outs = []
    for hd in range(C_HEADS):
        s = scores.pop(0)
        if hd + LOOKAHEAD < C_HEADS:
            scores.append(qk(hd + LOOKAHEAD))
        _, acc = _online_softmax(s, None, None, vt_ref[0, hd * A_V_ROWS:(hd + 1) * A_V_ROWS, :])
        outs.append(_normalized(acc, C_DIM))
    o_ref[0] = jnp.concatenate(outs, axis=0).T.astype(BF16)


def _attn_c(qt, mk, mvt):
    b, n_blk, w, tq = qt.shape
    n_mem = mk.shape[1]
    return pl.pallas_call(
        _attn_c_kernel,
        grid=(b, n_blk),
        in_specs=[pl.BlockSpec((1, 1, w, tq), lambda bi, qi: (bi, qi, 0, 0)),
                  pl.BlockSpec((1, n_mem, w), lambda bi, qi: (bi, 0, 0)),
                  pl.BlockSpec((1, mvt.shape[1], n_mem), lambda bi, qi: (bi, 0, 0))],
        out_specs=pl.BlockSpec((1, tq, w), lambda bi, qi: (bi, qi, 0)),
        out_shape=jax.ShapeDtypeStruct((b, n_blk * tq, w), BF16),
        compiler_params=_params(2),
        name="attn_c",
    )(qt, mk, mvt)


def _mix_kernel(x_ref, g_ref, wgate_ref, bgate_ref, oa_ref, ob_ref, oc_ref, wbr_ref, wout_ref,
                o_ref):
    x = x_ref[...]
    d = x.shape[1]
    h = _rms(x, g_ref[...]).astype(BF16)
    y = None
    for n, br_ref in enumerate((oa_ref, ob_ref, oc_ref)):
        z = _dot(h, wgate_ref[:, n * d:(n + 1) * d]) + bgate_ref[:, n * d:(n + 1) * d]
        term = (1.0 / (1.0 + jnp.exp(-z))) * _dot(br_ref[...], wbr_ref[n])
        y = term if y is None else y + term
    o_ref[...] = x + _dot(y.astype(BF16), wout_ref[...])


def _mix_out(x, g, w_gate, b_gate, o_a, o_b, o_c, w_branch, w_out):
    t, d = x.shape
    tm = TOKEN_TILE
    tile = lambda i: (i, 0)
    br_spec = pl.BlockSpec((tm, BRANCH_W), tile)
    return pl.pallas_call(
        _mix_kernel,
        grid=(t // tm,),
        in_specs=[pl.BlockSpec((tm, d), tile), _const_spec((1, d)), _const_spec(w_gate.shape),
                  _const_spec(b_gate.shape), br_spec, br_spec, br_spec,
                  _const_spec(w_branch.shape), _const_spec(w_out.shape)],
        out_specs=pl.BlockSpec((tm, d), tile),
        out_shape=jax.ShapeDtypeStruct((t, d), F32),
        compiler_params=_params(1),
        name="mix_out",
    )(x, g, w_gate, b_gate, o_a, o_b, o_c, w_branch, w_out)


def _rope_tables(seq):
    half = HEAD_DIM // 2
    inv = ROPE_THETA ** (-jnp.arange(half, dtype=F32) / half)
    ang = jnp.arange(seq, dtype=F32)[:, None] * inv[None, :]
    cos, sin = jnp.cos(ang), jnp.sin(ang)
    reps = LANES // HEAD_DIM
    return (jnp.tile(jnp.concatenate([cos, cos], axis=1), (1, reps)),
            jnp.tile(jnp.concatenate([-sin, sin], axis=1), (1, reps)),
            cos.T, sin.T)


def _group_mean_matrix(group):
    idx = jnp.arange(MXU_TILE) // group
    return jnp.where(idx[:, None] == idx[None, :], 1.0 / group, 0.0).astype(BF16)


def _bias_rows(table):
    h = table.shape[0]
    far = jnp.broadcast_to(table[:, -1:], (h, TOKEN_TILE - REL_CLIP))
    near = table[:, ::-1]
    tail = jnp.broadcast_to(table[:, -1:], (h, 2 * TOKEN_TILE - far.shape[1] - near.shape[1]))
    return jnp.concatenate([far, near, tail], axis=1)[:, None, :]


def kernel(x, mem, ffn1_norm, ffn1_wg, ffn1_wu, ffn1_wd, mix_norm, w_in, a_q_norm, a_k_norm,
           a_lambda, a_sub_norm, b_q_norm, b_k_norm, b_rel_bias, mem_norm, w_mem_kv, c_q_norm,
           c_k_norm, w_gate, b_gate, w_branch, w_out, ffn2_norm, ffn2_wg, ffn2_wu, ffn2_wd,
           final_norm):
    b, s, d = x.shape
    n_mem = mem.shape[1]
    depth = ffn1_norm.shape[0]
    tm = TOKEN_TILE
    w = BRANCH_W
    assert s % tm == 0 and d % MXU_TILE == 0 and w_in.shape[2] == 7 * w

    tables = _rope_tables(s)
    g64 = _group_mean_matrix(HEAD_DIM)
    g128 = _group_mean_matrix(C_DIM)
    row = lambda v: v.reshape(1, -1)
    heads = lambda v: jnp.tile(v, w // v.shape[0])
    cols = lambda i, j: slice(i * w, j * w)

    xt = x.reshape(b * s, d)
    memt = mem.reshape(b * n_mem, d)
    for l in range(depth):
        lambda_init = 0.8 - 0.6 * math.exp(-0.3 * l)
        xt = _ffn(xt, row(ffn1_norm[l]), ffn1_wg[l].astype(BF16), ffn1_wu[l].astype(BF16),
                  ffn1_wd[l].astype(BF16))

        wl = w_in[l].astype(BF16)
        w_t = jnp.concatenate([wl[:, cols(0, 1)], wl[:, cols(2, 4)], wl[:, cols(5, 7)]], axis=1).T
        w_k = jnp.concatenate([wl[:, cols(1, 2)], wl[:, cols(4, 5)]], axis=1)
        aqt, avt, bqt, bvt, cqt, ak, bk = _in_proj(
            xt, row(mix_norm[l]), w_t, w_k, tables,
            heads(a_q_norm[l]).reshape(w, 1), heads(b_q_norm[l]).reshape(w, 1),
            heads(c_q_norm[l]).reshape(w, 1), row(heads(a_k_norm[l])), row(heads(b_k_norm[l])),
            g64, s)
        wm = w_mem_kv[l].astype(BF16)
        mk, mvt = _mem_proj(memt, row(mem_norm[l]), wm[:, :w], wm[:, w:].T,
                            row(heads(c_k_norm[l])), g128, n_mem)
        bias = _rel_bias(_bias_rows(b_rel_bias[l]))

        blocks = lambda a: a.reshape(b, s // tm, a.shape[1], tm)
        seq3 = lambda a: a.reshape(b, s, w)
        o_a = _attn_a(blocks(aqt), seq3(ak), blocks(avt), a_lambda[l], row(a_sub_norm[l]),
                      lambda_init)
        o_b = _attn_b(blocks(bqt), seq3(bk), blocks(bvt), bias)
        o_c = _attn_c(blocks(cqt), mk.reshape(b, n_mem, w), mvt)

        flat = lambda a: a.reshape(b * s, w)
        xt = _mix_out(xt, row(mix_norm[l]), w_gate[l].astype(BF16), row(b_gate[l]),
                      flat(o_a), flat(o_b), flat(o_c), w_branch[l].astype(BF16),
                      w_out[l].astype(BF16))

        xt = _ffn(xt, row(ffn2_norm[l]), ffn2_wg[l].astype(BF16), ffn2_wu[l].astype(BF16),
                  ffn2_wd[l].astype(BF16), row(final_norm[l]))
    return xt.reshape(b, s, d)
```

```python
import functools
import math

import jax
import jax.numpy as jnp
from jax import lax
from jax.experimental import pallas as pl
from jax.experimental.pallas import tpu as pltpu

F32 = jnp.float32
BF16 = jnp.bfloat16

CHUNK = 64
HEAD_DIM = 64
A_HEADS = 4
B_HEADS = 8
B_LEFT_CHUNKS = 8
REL_CLIP = 128
C_HEADS = 4
C_DIM = 128
N_BRANCH = 3
BRANCH_W = 512
ROPE_THETA = 10000.0
EPS = 1e-6
NEG = -1e30
LOG2E = math.log2(math.e)

LANES = 128
MXU_TILE = 256
BF16_ROWS = 16
TOKEN_TILE = 512
KEY_PART = MXU_TILE
LOOKAHEAD = 2
ONES_ROWS = BF16_ROWS
VMEM_LIMIT = 56 * 1024 * 1024

A_V_ROWS = 2 * HEAD_DIM + ONES_ROWS
B_V_ROWS = HEAD_DIM + ONES_ROWS

assert TOKEN_TILE == B_LEFT_CHUNKS * CHUNK


def _dot(a, b):
    return jnp.dot(a, b, preferred_element_type=F32)


def _dot_nt(a, b):
    return lax.dot_general(a, b, (((1,), (1,)), ((), ())), preferred_element_type=F32)


def _rms(x, g):
    return x * lax.rsqrt(jnp.mean(x * x, axis=-1, keepdims=True) + EPS) * g


def _group_rms(z, gmat_ref, gain):
    z2 = (z * z).astype(BF16)
    ms = jnp.concatenate(
        [_dot(z2[:, :MXU_TILE], gmat_ref[...]), _dot(z2[:, MXU_TILE:], gmat_ref[...])], axis=1)
    return z * lax.rsqrt(ms + EPS) * gain


def _heads_t(z, gain, group, scale, cos=None, sin=None):
    outs = []
    for g0 in range(0, z.shape[0], group):
        zz = z[g0:g0 + group]
        r = lax.rsqrt(jnp.mean(zz * zz, axis=0, keepdims=True) + EPS) * scale
        y = zz * r * gain[g0:g0 + group]
        if cos is not None:
            y1, y2 = y[:group // 2], y[group // 2:]
            y = jnp.concatenate([y1 * cos - y2 * sin, y2 * cos + y1 * sin], axis=0)
        outs.append(y)
    return jnp.concatenate(outs, axis=0)


def _with_ones(vt, head_rows):
    ones = jnp.ones((ONES_ROWS, vt.shape[1]), vt.dtype)
    parts = []
    for r0 in range(0, vt.shape[0], head_rows):
        parts += [vt[r0:r0 + head_rows], ones]
    return jnp.concatenate(parts, axis=0)


def _const_spec(shape):
    nd = len(shape)
    return pl.BlockSpec(shape, lambda *_: (0,) * nd, pipeline_mode=pl.Buffered(1))


def _params(n_axes):
    return pltpu.CompilerParams(
        dimension_semantics=("arbitrary",) * n_axes, vmem_limit_bytes=VMEM_LIMIT)


def _ffn_chunks(d_ff):
    step = 3 * MXU_TILE
    return [(c, min(c + step, d_ff)) for c in range(0, d_ff, step)]


def _ffn_kernel(x_ref, g_ref, wg_ref, wu_ref, wd_ref, *rest, final_norm):
    if final_norm:
        fg_ref, o_ref, act_ref = rest
    else:
        o_ref, act_ref = rest
    x = x_ref[...]
    h = _rms(x, g_ref[...]).astype(BF16)
    for c0, c1 in _ffn_chunks(wg_ref.shape[1]):
        g = _dot(h, wg_ref[:, c0:c1])
        u = _dot(h, wu_ref[:, c0:c1])
        act_ref[:, c0:c1] = (g * (1.0 / (1.0 + jnp.exp(-g))) * u).astype(BF16)
    out = x + 0.5 * _dot(act_ref[...], wd_ref[...])
    if final_norm:
        out = _rms(out, fg_ref[...])
    o_ref[...] = out


def _ffn(x, g, wg, wu, wd, final_g=None):
    t, d = x.shape
    f = wg.shape[1]
    tm = TOKEN_TILE
    in_specs = [pl.BlockSpec((tm, d), lambda i: (i, 0)), _const_spec((1, d)),
                _const_spec((d, f)), _const_spec((d, f)), _const_spec((f, d))]
    args = [x, g, wg, wu, wd]
    if final_g is not None:
        in_specs.append(_const_spec((1, d)))
        args.append(final_g)
    return pl.pallas_call(
        functools.partial(_ffn_kernel, final_norm=final_g is not None),
        grid=(t // tm,),
        in_specs=in_specs,
        out_specs=pl.BlockSpec((tm, d), lambda i: (i, 0)),
        out_shape=jax.ShapeDtypeStruct((t, d), F32),
        scratch_shapes=[pltpu.VMEM((tm, f), BF16)],
        compiler_params=_params(1),
        name="ffn_final" if final_g is not None else "ffn",
    )(*args)


def _rope(q, cos, sin):
    lane = lax.broadcasted_iota(jnp.int32, (q.shape[0], LANES), 1)
    low_half = (lane & 32) == 0
    outs = []
    for c in range(q.shape[1] // LANES):
        xc = q[:, c * LANES:(c + 1) * LANES]
        partner = jnp.where(low_half, pltpu.roll(xc, LANES - 32, 1), pltpu.roll(xc, 32, 1))
        outs.append(xc * cos + partner * sin)
    return jnp.concatenate(outs, axis=1)


def _in_kernel(x_ref, g_ref, wt_ref, wk_ref, cos_ref, sin_ref, cost_ref, sint_ref,
               gaq_ref, gbq_ref, gcq_ref, gak_ref, gbk_ref, g64_ref,
               aqt_ref, avt_ref, bqt_ref, bvt_ref, cqt_ref, ak_ref, bk_ref):
    h = _rms(x_ref[...], g_ref[...]).astype(BF16)
    w = BRANCH_W

    def seg_t(i):
        return _dot_nt(wt_ref[i * w:(i + 1) * w, :], h)

    qk_scale = HEAD_DIM ** -0.5 * LOG2E
    aqt_ref[0] = _heads_t(seg_t(0), gaq_ref[...], HEAD_DIM, qk_scale,
                          cost_ref[...], sint_ref[...]).astype(BF16)
    avt_ref[0] = _with_ones(seg_t(1).astype(BF16), 2 * HEAD_DIM)
    bqt_ref[0] = _heads_t(seg_t(2), gbq_ref[...], HEAD_DIM, qk_scale).astype(BF16)
    bvt_ref[0] = _with_ones(seg_t(3).astype(BF16), HEAD_DIM)
    cqt_ref[0] = _heads_t(seg_t(4), gcq_ref[...], C_DIM, C_DIM ** -0.5 * LOG2E).astype(BF16)

    zk = _dot(h, wk_ref[:, :w])
    ak_ref[...] = _rope(_group_rms(zk, g64_ref, gak_ref[...]), cos_ref[...], sin_ref[...]
                        ).astype(BF16)
    bk_ref[...] = _group_rms(_dot(h, wk_ref[:, w:]), g64_ref, gbk_ref[...]).astype(BF16)


def _in_proj(x, g, w_t, w_k, tables, gaq, gbq, gcq, gak, gbk, g64, seq):
    t, d = x.shape
    tm = TOKEN_TILE
    n_tiles = t // tm
    n_seq_tiles = seq // tm
    w = BRANCH_W
    cos, sin, cos_t, sin_t = tables
    tile = lambda i: (i, 0)
    half = HEAD_DIM // 2
    in_specs = [pl.BlockSpec((tm, d), tile), _const_spec((1, d)), _const_spec(w_t.shape),
                _const_spec(w_k.shape),
                pl.BlockSpec((tm, LANES), lambda i: (i % n_seq_tiles, 0)),
                pl.BlockSpec((tm, LANES), lambda i: (i % n_seq_tiles, 0)),
                pl.BlockSpec((half, tm), lambda i: (0, i % n_seq_tiles)),
                pl.BlockSpec((half, tm), lambda i: (0, i % n_seq_tiles))]
    in_specs += [_const_spec((w, 1))] * 3 + [_const_spec((1, w))] * 2
    in_specs += [_const_spec((MXU_TILE, MXU_TILE))]
    t_rows = [w, A_HEADS * A_V_ROWS, w, B_HEADS * B_V_ROWS, w]
    return pl.pallas_call(
        _in_kernel,
        grid=(n_tiles,),
        in_specs=in_specs,
        out_specs=([pl.BlockSpec((1, r, tm), lambda i: (i, 0, 0)) for r in t_rows]
                   + [pl.BlockSpec((tm, w), tile)] * 2),
        out_shape=([jax.ShapeDtypeStruct((n_tiles, r, tm), BF16) for r in t_rows]
                   + [jax.ShapeDtypeStruct((t, w), BF16)] * 2),
        compiler_params=_params(1),
        name="in_proj",
    )(x, g, w_t, w_k, cos, sin, cos_t, sin_t, gaq, gbq, gcq, gak, gbk, g64)


def _mem_kernel(m_ref, g_ref, wk_ref, wvt_ref, gk_ref, g128_ref, mk_ref, mvt_ref):
    h = _rms(m_ref[...], g_ref[...]).astype(BF16)
    mk_ref[...] = _group_rms(_dot(h, wk_ref[...]), g128_ref, gk_ref[...]).astype(BF16)
    mvt_ref[0] = _with_ones(_dot_nt(wvt_ref[...], h).astype(BF16), C_DIM)


def _mem_proj(mem, g, w_k, w_vt, gk, g128, n_mem):
    t, d = mem.shape
    w = BRANCH_W
    v_rows = C_HEADS * A_V_ROWS
    tile = lambda i: (i, 0)
    return pl.pallas_call(
        _mem_kernel,
        grid=(t // n_mem,),
        in_specs=[pl.BlockSpec((n_mem, d), tile), _const_spec((1, d)), _const_spec(w_k.shape),
                  _const_spec(w_vt.shape), _const_spec((1, w)),
                  _const_spec((MXU_TILE, MXU_TILE))],
        out_specs=[pl.BlockSpec((n_mem, w), tile),
                   pl.BlockSpec((1, v_rows, n_mem), lambda i: (i, 0, 0))],
        out_shape=[jax.ShapeDtypeStruct((t, w), BF16),
                   jax.ShapeDtypeStruct((t // n_mem, v_rows, n_mem), BF16)],
        compiler_params=_params(1),
        name="mem_proj",
    )(mem, g, w_k, w_vt, gk, g128)


def _bias_kernel(r_ref, o_ref):
    nq = TOKEN_TILE
    nk = 2 * nq
    half = nq // 2
    qc = lax.broadcasted_iota(jnp.int32, (nq, nk), 0) // CHUNK + B_LEFT_CHUNKS
    kc = lax.broadcasted_iota(jnp.int32, (nq, nk), 1) // CHUNK
    valid = (qc >= kc) & (qc - kc <= B_LEFT_CHUNKS)
    for hh in range(2):
        t = pltpu.roll(jnp.broadcast_to(r_ref[hh], (nq, nk)), 0, 1, stride=1, stride_axis=0)
        t = jnp.where(valid, t * LOG2E, NEG).T
        for ch in range(2):
            c0 = ch * nq + hh * half
            o_ref[0, :, c0:c0 + half] = t[:, ch * half:(ch + 1) * half]


def _rel_bias(rows):
    h, _, nk = rows.shape
    return pl.pallas_call(
        _bias_kernel,
        grid=(h // 2,),
        in_specs=[pl.BlockSpec((2, 1, nk), lambda i: (i, 0, 0))],
        out_specs=pl.BlockSpec((1, nk, nk), lambda i: (i, 0, 0)),
        out_shape=jax.ShapeDtypeStruct((h // 2, nk, nk), F32),
        compiler_params=_params(1),
        name="rel_bias",
    )(rows)


def _split_queries(qt):
    row = lax.broadcasted_iota(jnp.int32, qt.shape, 0)
    zero = jnp.zeros_like(qt)
    return jnp.concatenate(
        [jnp.where(row < HEAD_DIM, qt, zero), jnp.where(row >= HEAD_DIM, qt, zero)], axis=1)


def _online_softmax(s, m_prev, acc_prev, vt):
    s_max = jnp.max(s, axis=0, keepdims=True)
    m_new = s_max if m_prev is None else jnp.maximum(m_prev, s_max)
    pv = _dot(vt, jnp.exp2(s - m_new).astype(BF16))
    if acc_prev is None:
        return m_new, pv
    return m_new, jnp.exp2(m_prev - m_new) * acc_prev + pv


def _normalized(acc, rows):
    return acc[:rows] / acc[rows:rows + 1]


def _attn_a_kernel(qt_ref, k_ref, vt_ref, lam_ref, gain_ref, o_ref, q12_ref, *scratch,
                   lambda_init):
    n_blk, _, tq = qt_ref.shape[1:]
    score_refs = scratch[:LOOKAHEAD]
    state_refs = [scratch[LOOKAHEAD + 2 * c:LOOKAHEAD + 2 * c + 2] for c in range(2)]
    for blk in range(n_blk):
        q12_ref[blk] = _split_queries(qt_ref[0, blk])
    lp = lam_ref[...]
    lam = (jnp.exp(jnp.sum(lp[0:1] * lp[1:2], axis=1, keepdims=True))
           - jnp.exp(jnp.sum(lp[2:3] * lp[3:4], axis=1, keepdims=True)) + lambda_init)

    hk = KEY_PART
    items = [(kp, c) for kp in range(tq // hk) for c in range(2)]
    assert LOOKAHEAD <= len(items)

    def qk(qi, j, item):
        kp, c = item
        kb = k_ref[0, pl.ds(pl.multiple_of(j * tq + kp * hk, hk), hk), :]
        return _dot(kb, q12_ref[qi, :, c * tq:(c + 1) * tq])

    def begin(qi):
        for m_ref, acc_ref in state_refs:
            m_ref[...] = jnp.full(m_ref.shape, NEG, F32)
            acc_ref[...] = jnp.zeros(acc_ref.shape, F32)
        for i, r in enumerate(score_refs):
            r[...] = qk(qi, 0, items[i])

    def block(qi, j, own):
        scores = [r[...] for r in score_refs]
        for i, (kp, c) in enumerate(items):
            s = scores.pop(0)
            if own:
                key_chunk = (lax.broadcasted_iota(jnp.int32, (hk, tq), 0) + kp * hk) // CHUNK
                query_chunk = lax.broadcasted_iota(jnp.int32, (hk, tq), 1) // CHUNK
                s = jnp.where(key_chunk <= query_chunk, s, NEG)
            nxt = i + LOOKAHEAD
            if nxt < len(items):
                scores.append(qk(qi, j, items[nxt]))
            elif not own:
                score_refs[nxt - len(items)][...] = qk(qi, j + 1, items[nxt - len(items)])
            m_ref, acc_ref = state_refs[c]
            m_ref[...], acc_ref[...] = _online_softmax(
                s, m_ref[...], acc_ref[...], vt_ref[0, j, :, kp * hk:(kp + 1) * hk])

    def query_block(qi, carry):
        def body(j, c):
            block(qi, j, False)
            return c

        lax.fori_loop(0, qi, body, 0)
        block(qi, qi, True)
        o1, o2 = (_normalized(acc_ref[...], 2 * HEAD_DIM) for _, acc_ref in state_refs)
        ot = o1 - lam * o2
        ot = ot * lax.rsqrt(jnp.mean(ot * ot, axis=0, keepdims=True) + EPS)
        o_ref[0, pl.ds(pl.multiple_of(qi * tq, tq), tq), :] = (
            ot.T * gain_ref[...] * (1.0 - lambda_init)).astype(BF16)
        begin(jnp.minimum(qi + 1, n_blk - 1))
        return carry

    begin(0)
    lax.fori_loop(0, n_blk, query_block, 0)


def _attn_a(qt, k, vt, lam_p, gain, lambda_init):
    b, n_blk, w, tq = qt.shape
    s = n_blk * tq
    return pl.pallas_call(
        functools.partial(_attn_a_kernel, lambda_init=lambda_init),
        grid=(b, w // LANES),
        in_specs=[pl.BlockSpec((1, n_blk, LANES, tq), lambda bi, hi: (bi, 0, hi, 0)),
                  pl.BlockSpec((1, s, LANES), lambda bi, hi: (bi, 0, hi)),
                  pl.BlockSpec((1, n_blk, A_V_ROWS, tq), lambda bi, hi: (bi, 0, hi, 0)),
                  _const_spec(lam_p.shape), _const_spec((1, LANES))],
        out_specs=pl.BlockSpec((1, s, LANES), lambda bi, hi: (bi, 0, hi)),
        out_shape=jax.ShapeDtypeStruct((b, s, w), BF16),
        scratch_shapes=(
            [pltpu.VMEM((n_blk, LANES, 2 * tq), BF16)]
            + [pltpu.VMEM((KEY_PART, tq), F32)] * LOOKAHEAD
            + [pltpu.VMEM((1, tq), F32), pltpu.VMEM((A_V_ROWS, tq), F32)] * 2),
        compiler_params=_params(2),
        name="attn_a",
    )(qt, k, vt, lam_p, gain)


B_ITEMS = ((2, 0), (2, 1), (1, 0), (3, 1), (0, 0), (1, 1))


def _attn_b_kernel(qt_ref, k_ref, vt_ref, bias_ref, o_ref, q12_ref, *score_refs):
    n_blk, _, tq = qt_ref.shape[1:]
    hk = KEY_PART
    half = tq // 2
    assert half == hk and LOOKAHEAD == 2
    for blk in range(n_blk):
        q12 = _split_queries(qt_ref[0, blk])
        q12_ref[blk] = jnp.concatenate(
            [q12[:, :half], q12[:, tq:tq + half], q12[:, half:tq], q12[:, tq + half:]], axis=1)

    def source(qi, wp):
        blk = qi if wp >= 2 else jnp.maximum(qi - 1, 0)
        return blk, (wp % 2) * hk

    def qk(qi, item):
        wp, ch = item
        blk, r0 = source(qi, wp)
        cols = slice(ch * tq, (ch + 1) * tq)
        kb = k_ref[0, pl.ds(pl.multiple_of(blk * tq + r0, hk), hk), :]
        s = _dot(kb, q12_ref[qi, :, cols]) + bias_ref[0, wp * hk:(wp + 1) * hk, cols]
        if wp < 2:
            s = jnp.where(qi > 0, s, NEG)
        return s

    def query_block(qi, carry):
        scores = [r[...] for r in score_refs]
        state = [(None, None), (None, None)]
        for i, (wp, ch) in enumerate(B_ITEMS):
            s = scores.pop(0)
            nxt = i + LOOKAHEAD
            if nxt < len(B_ITEMS):
                scores.append(qk(qi, B_ITEMS[nxt]))
            else:
                score_refs[nxt - len(B_ITEMS)][...] = qk(
                    jnp.minimum(qi + 1, n_blk - 1), B_ITEMS[nxt - len(B_ITEMS)])
            blk, r0 = source(qi, wp)
            state[ch] = _online_softmax(s, state[ch][0], state[ch][1],
                                        vt_ref[0, blk, :, r0:r0 + hk])
        halves = []
        for _, acc in state:
            heads = [_normalized(acc[hh * B_V_ROWS:(hh + 1) * B_V_ROWS,
                                     hh * half:(hh + 1) * half], HEAD_DIM) for hh in range(2)]
            halves.append(jnp.concatenate(heads, axis=0))
        ot = jnp.concatenate(halves, axis=1)
        o_ref[0, pl.ds(pl.multiple_of(qi * tq, tq), tq), :] = ot.T.astype(BF16)
        return carry

    for i, r in enumerate(score_refs):
        r[...] = qk(0, B_ITEMS[i])
    lax.fori_loop(0, n_blk, query_block, 0)


def _attn_b(qt, k, vt, bias):
    b, n_blk, w, tq = qt.shape
    s = n_blk * tq
    return pl.pallas_call(
        _attn_b_kernel,
        grid=(w // LANES, b),
        in_specs=[pl.BlockSpec((1, n_blk, LANES, tq), lambda pi, bi: (bi, 0, pi, 0)),
                  pl.BlockSpec((1, s, LANES), lambda pi, bi: (bi, 0, pi)),
                  pl.BlockSpec((1, n_blk, 2 * B_V_ROWS, tq), lambda pi, bi: (bi, 0, pi, 0)),
                  pl.BlockSpec((1, 2 * tq, 2 * tq), lambda pi, bi: (pi, 0, 0))],
        out_specs=pl.BlockSpec((1, s, LANES), lambda pi, bi: (bi, 0, pi)),
        out_shape=jax.ShapeDtypeStruct((b, s, w), BF16),
        scratch_shapes=([pltpu.VMEM((n_blk, LANES, 2 * tq), BF16)]
                        + [pltpu.VMEM((KEY_PART, tq), F32)] * LOOKAHEAD),
        compiler_params=_params(2),
        name="attn_b",
    )(qt, k, vt, bias)


def _attn_c_kernel(qt_ref, k_ref, vt_ref, o_ref):
    n_blk, _, tq = qt_ref.shape[1:]

    def query_block(qi, carry):
        def qk(hd):
            rows = slice(hd * C_DIM, (hd + 1) * C_DIM)
            return _dot(k_ref[0, :, rows], qt_ref[0, qi, rows, :])

        scores = [qk(hd) for hd in range(LOOKAHEAD)]
        outs = []
        for hd in range(C_HEADS):
            s = scores.pop(0)
            if hd + LOOKAHEAD < C_HEADS:
                scores.append(qk(hd + LOOKAHEAD))
            _, acc = _online_softmax(s, None, None,
                                     vt_ref[0, hd * A_V_ROWS:(hd + 1) * A_V_ROWS, :])
            outs.append(_normalized(acc, C_DIM))
        o_ref[0, pl.ds(pl.multiple_of(qi * tq, tq), tq), :] = (
            jnp.concatenate(outs, axis=0).T.astype(BF16))
        return carry

    lax.fori_loop(0, n_blk, query_block, 0)


def _attn_c(qt, mk, mvt):
    b, n_blk, w, tq = qt.shape
    n_mem = mk.shape[1]
    return pl.pallas_call(
        _attn_c_kernel,
        grid=(b,),
        in_specs=[pl.BlockSpec((1, n_blk, w, tq), lambda bi: (bi, 0, 0, 0)),
                  pl.BlockSpec((1, n_mem, w), lambda bi: (bi, 0, 0)),
                  pl.BlockSpec((1, mvt.shape[1], n_mem), lambda bi: (bi, 0, 0))],
        out_specs=pl.BlockSpec((1, n_blk * tq, w), lambda bi: (bi, 0, 0)),
        out_shape=jax.ShapeDtypeStruct((b, n_blk * tq, w), BF16),
        compiler_params=_params(1),
        name="attn_c",
    )(qt, mk, mvt)


def _mix_kernel(x_ref, g_ref, wgate_ref, bgate_ref, oa_ref, ob_ref, oc_ref, wbr_ref, wout_ref,
                o_ref):
    x = x_ref[...]
    d = x.shape[1]
    h = _rms(x, g_ref[...]).astype(BF16)
    y = None
    for n, br_ref in enumerate((oa_ref, ob_ref, oc_ref)):
        z = _dot(h, wgate_ref[:, n * d:(n + 1) * d]) + bgate_ref[:, n * d:(n + 1) * d]
        term = (1.0 / (1.0 + jnp.exp(-z))) * _dot(br_ref[...], wbr_ref[n])
        y = term if y is None else y + term
    o_ref[...] = x + _dot(y.astype(BF16), wout_ref[...])


def _mix_out(x, g, w_gate, b_gate, o_a, o_b, o_c, w_branch, w_out):
    t, d = x.shape
    tm = TOKEN_TILE
    tile = lambda i: (i, 0)
    br_spec = pl.BlockSpec((tm, BRANCH_W), tile)
    return pl.pallas_call(
        _mix_kernel,
        grid=(t // tm,),
        in_specs=[pl.BlockSpec((tm, d), tile), _const_spec((1, d)), _const_spec(w_gate.shape),
                  _const_spec(b_gate.shape), br_spec, br_spec, br_spec,
                  _const_spec(w_branch.shape), _const_spec(w_out.shape)],
        out_specs=pl.BlockSpec((tm, d), tile),
        out_shape=jax.ShapeDtypeStruct((t, d), F32),
        compiler_params=_params(1),
        name="mix_out",
    )(x, g, w_gate, b_gate, o_a, o_b, o_c, w_branch, w_out)


def _rope_tables(seq):
    half = HEAD_DIM // 2
    inv = ROPE_THETA ** (-jnp.arange(half, dtype=F32) / half)
    ang = jnp.arange(seq, dtype=F32)[:, None] * inv[None, :]
    cos, sin = jnp.cos(ang), jnp.sin(ang)
    reps = LANES // HEAD_DIM
    return (jnp.tile(jnp.concatenate([cos, cos], axis=1), (1, reps)),
            jnp.tile(jnp.concatenate([-sin, sin], axis=1), (1, reps)),
            cos.T, sin.T)


def _group_mean_matrix(group):
    idx = jnp.arange(MXU_TILE) // group
    return jnp.where(idx[:, None] == idx[None, :], 1.0 / group, 0.0).astype(BF16)


def _bias_rows(table):
    h = table.shape[0]
    far = jnp.broadcast_to(table[:, -1:], (h, TOKEN_TILE - REL_CLIP))
    near = table[:, ::-1]
    tail = jnp.broadcast_to(table[:, -1:], (h, 2 * TOKEN_TILE - far.shape[1] - near.shape[1]))
    return jnp.concatenate([far, near, tail], axis=1)[:, None, :]


def kernel(x, mem, ffn1_norm, ffn1_wg, ffn1_wu, ffn1_wd, mix_norm, w_in, a_q_norm, a_k_norm,
           a_lambda, a_sub_norm, b_q_norm, b_k_norm, b_rel_bias, mem_norm, w_mem_kv, c_q_norm,
           c_k_norm, w_gate, b_gate, w_branch, w_out, ffn2_norm, ffn2_wg, ffn2_wu, ffn2_wd,
           final_norm):
    b, s, d = x.shape
    n_mem = mem.shape[1]
    depth = ffn1_norm.shape[0]
    tm = TOKEN_TILE
    w = BRANCH_W
    assert s % tm == 0 and d % MXU_TILE == 0 and w_in.shape[2] == 7 * w

    tables = _rope_tables(s)
    g64 = _group_mean_matrix(HEAD_DIM)
    g128 = _group_mean_matrix(C_DIM)
    row = lambda v: v.reshape(1, -1)
    heads = lambda v: jnp.tile(v, w // v.shape[0])
    cols = lambda i, j: slice(i * w, j * w)

    xt = x.reshape(b * s, d)
    memt = mem.reshape(b * n_mem, d)
    for l in range(depth):
        lambda_init = 0.8 - 0.6 * math.exp(-0.3 * l)
        xt = _ffn(xt, row(ffn1_norm[l]), ffn1_wg[l].astype(BF16), ffn1_wu[l].astype(BF16),
                  ffn1_wd[l].astype(BF16))

        wl = w_in[l].astype(BF16)
        w_t = jnp.concatenate([wl[:, cols(0, 1)], wl[:, cols(2, 4)], wl[:, cols(5, 7)]], axis=1).T
        w_k = jnp.concatenate([wl[:, cols(1, 2)], wl[:, cols(4, 5)]], axis=1)
        aqt, avt, bqt, bvt, cqt, ak, bk = _in_proj(
            xt, row(mix_norm[l]), w_t, w_k, tables,
            heads(a_q_norm[l]).reshape(w, 1), heads(b_q_norm[l]).reshape(w, 1),
            heads(c_q_norm[l]).reshape(w, 1), row(heads(a_k_norm[l])), row(heads(b_k_norm[l])),
            g64, s)
        wm = w_mem_kv[l].astype(BF16)
        mk, mvt = _mem_proj(memt, row(mem_norm[l]), wm[:, :w], wm[:, w:].T,
                            row(heads(c_k_norm[l])), g128, n_mem)
        bias = _rel_bias(_bias_rows(b_rel_bias[l]))

        blocks = lambda a: a.reshape(b, s // tm, a.shape[1], tm)
        seq3 = lambda a: a.reshape(b, s, w)
        o_a = _attn_a(blocks(aqt), seq3(ak), blocks(avt), a_lambda[l], row(a_sub_norm[l]),
                      lambda_init)
        o_b = _attn_b(blocks(bqt), seq3(bk), blocks(bvt), bias)
        o_c = _attn_c(blocks(cqt), mk.reshape(b, n_mem, w), mvt)

        flat = lambda a: a.reshape(b * s, w)
        xt = _mix_out(xt, row(mix_norm[l]), w_gate[l].astype(BF16), row(b_gate[l]),
                      flat(o_a), flat(o_b), flat(o_c), w_branch[l].astype(BF16),
                      w_out[l].astype(BF16))

        xt = _ffn(xt, row(ffn2_norm[l]), ffn2_wg[l].astype(BF16), ffn2_wu[l].astype(BF16),
                  ffn2_wd[l].astype(BF16), row(final_norm[l]))
    return xt.reshape(b, s, d)
```

```python
import functools
import math

import jax
import jax.numpy as jnp
from jax import lax
from jax.experimental import pallas as pl
from jax.experimental.pallas import tpu as pltpu

F32 = jnp.float32
BF16 = jnp.bfloat16

CHUNK = 64
HEAD_DIM = 64
A_HEADS = 4
B_HEADS = 8
B_LEFT_CHUNKS = 8
REL_CLIP = 128
C_HEADS = 4
C_DIM = 128
N_BRANCH = 3
BRANCH_W = 512
ROPE_THETA = 10000.0
EPS = 1e-6
NEG = -1e30
LOG2E = math.log2(math.e)

LANES = 128
MXU_TILE = 256
BF16_ROWS = 16
TOKEN_TILE = 512
KEY_PART = MXU_TILE
LOOKAHEAD = 2
A_KEY_PART = MXU_TILE
A_LOOKAHEAD = 2
A_QUERY_SUBS = 2
ONES_ROWS = BF16_ROWS
VMEM_LIMIT = 56 * 1024 * 1024

A_V_ROWS = 2 * HEAD_DIM + ONES_ROWS
B_V_ROWS = HEAD_DIM + ONES_ROWS

assert TOKEN_TILE == B_LEFT_CHUNKS * CHUNK


def _dot(a, b):
    return jnp.dot(a, b, preferred_element_type=F32)


def _dot_nt(a, b):
    return lax.dot_general(a, b, (((1,), (1,)), ((), ())), preferred_element_type=F32)


def _rms(x, g):
    return x * lax.rsqrt(jnp.mean(x * x, axis=-1, keepdims=True) + EPS) * g


def _group_rms(z, gmat_ref, gain):
    z2 = (z * z).astype(BF16)
    ms = jnp.concatenate(
        [_dot(z2[:, :MXU_TILE], gmat_ref[...]), _dot(z2[:, MXU_TILE:], gmat_ref[...])], axis=1)
    return z * lax.rsqrt(ms + EPS) * gain


def _heads_t(z, gain, group, scale, cos=None, sin=None):
    outs = []
    for g0 in range(0, z.shape[0], group):
        zz = z[g0:g0 + group]
        r = lax.rsqrt(jnp.mean(zz * zz, axis=0, keepdims=True) + EPS) * scale
        y = zz * r * gain[g0:g0 + group]
        if cos is not None:
            y1, y2 = y[:group // 2], y[group // 2:]
            y = jnp.concatenate([y1 * cos - y2 * sin, y2 * cos + y1 * sin], axis=0)
        outs.append(y)
    return jnp.concatenate(outs, axis=0)


def _with_ones(vt, head_rows):
    ones = jnp.ones((ONES_ROWS, vt.shape[1]), vt.dtype)
    parts = []
    for r0 in range(0, vt.shape[0], head_rows):
        parts += [vt[r0:r0 + head_rows], ones]
    return jnp.concatenate(parts, axis=0)


def _const_spec(shape):
    nd = len(shape)
    return pl.BlockSpec(shape, lambda *_: (0,) * nd, pipeline_mode=pl.Buffered(1))


def _params(n_axes):
    return pltpu.CompilerParams(
        dimension_semantics=("arbitrary",) * n_axes, vmem_limit_bytes=VMEM_LIMIT)


def _ffn_chunks(d_ff):
    step = 3 * MXU_TILE
    return [(c, min(c + step, d_ff)) for c in range(0, d_ff, step)]


def _ffn_kernel(x_ref, g_ref, wg_ref, wu_ref, wd_ref, *rest, final_norm):
    if final_norm:
        fg_ref, o_ref, act_ref = rest
    else:
        o_ref, act_ref = rest
    x = x_ref[...]
    h = _rms(x, g_ref[...]).astype(BF16)
    for c0, c1 in _ffn_chunks(wg_ref.shape[1]):
        g = _dot(h, wg_ref[:, c0:c1])
        u = _dot(h, wu_ref[:, c0:c1])
        act_ref[:, c0:c1] = (g * (1.0 / (1.0 + jnp.exp(-g))) * u).astype(BF16)
    out = x + 0.5 * _dot(act_ref[...], wd_ref[...])
    if final_norm:
        out = _rms(out, fg_ref[...])
    o_ref[...] = out


def _ffn(x, g, wg, wu, wd, final_g=None):
    t, d = x.shape
    f = wg.shape[1]
    tm = TOKEN_TILE
    in_specs = [pl.BlockSpec((tm, d), lambda i: (i, 0)), _const_spec((1, d)),
                _const_spec((d, f)), _const_spec((d, f)), _const_spec((f, d))]
    args = [x, g, wg, wu, wd]
    if final_g is not None:
        in_specs.append(_const_spec((1, d)))
        args.append(final_g)
    return pl.pallas_call(
        functools.partial(_ffn_kernel, final_norm=final_g is not None),
        grid=(t // tm,),
        in_specs=in_specs,
        out_specs=pl.BlockSpec((tm, d), lambda i: (i, 0)),
        out_shape=jax.ShapeDtypeStruct((t, d), F32),
        scratch_shapes=[pltpu.VMEM((tm, f), BF16)],
        compiler_params=_params(1),
        name="ffn_final" if final_g is not None else "ffn",
    )(*args)


def _rope(q, cos, sin):
    lane = lax.broadcasted_iota(jnp.int32, (q.shape[0], LANES), 1)
    low_half = (lane & 32) == 0
    outs = []
    for c in range(q.shape[1] // LANES):
        xc = q[:, c * LANES:(c + 1) * LANES]
        partner = jnp.where(low_half, pltpu.roll(xc, LANES - 32, 1), pltpu.roll(xc, 32, 1))
        outs.append(xc * cos + partner * sin)
    return jnp.concatenate(outs, axis=1)


def _in_kernel(x_ref, g_ref, wt_ref, wk_ref, cos_ref, sin_ref, cost_ref, sint_ref,
               gaq_ref, gbq_ref, gcq_ref, gak_ref, gbk_ref, g64_ref,
               aqt_ref, avt_ref, bqt_ref, bvt_ref, cqt_ref, ak_ref, bk_ref):
    h = _rms(x_ref[...], g_ref[...]).astype(BF16)
    w = BRANCH_W

    def seg_t(i):
        return _dot_nt(wt_ref[i * w:(i + 1) * w, :], h)

    qk_scale = HEAD_DIM ** -0.5 * LOG2E
    aqt_ref[0] = _heads_t(seg_t(0), gaq_ref[...], HEAD_DIM, qk_scale,
                          cost_ref[...], sint_ref[...]).astype(BF16)
    avt_ref[0] = _with_ones(seg_t(1).astype(BF16), 2 * HEAD_DIM)
    bqt_ref[0] = _heads_t(seg_t(2), gbq_ref[...], HEAD_DIM, qk_scale).astype(BF16)
    bvt_ref[0] = _with_ones(seg_t(3).astype(BF16), HEAD_DIM)
    cqt_ref[0] = _heads_t(seg_t(4), gcq_ref[...], C_DIM, C_DIM ** -0.5 * LOG2E).astype(BF16)

    zk = _dot(h, wk_ref[:, :w])
    ak_ref[...] = _rope(_group_rms(zk, g64_ref, gak_ref[...]), cos_ref[...], sin_ref[...]
                        ).astype(BF16)
    bk_ref[...] = _group_rms(_dot(h, wk_ref[:, w:]), g64_ref, gbk_ref[...]).astype(BF16)


def _in_proj(x, g, w_t, w_k, tables, gaq, gbq, gcq, gak, gbk, g64, seq):
    t, d = x.shape
    tm = TOKEN_TILE
    n_tiles = t // tm
    n_seq_tiles = seq // tm
    w = BRANCH_W
    cos, sin, cos_t, sin_t = tables
    tile = lambda i: (i, 0)
    half = HEAD_DIM // 2
    in_specs = [pl.BlockSpec((tm, d), tile), _const_spec((1, d)), _const_spec(w_t.shape),
                _const_spec(w_k.shape),
                pl.BlockSpec((tm, LANES), lambda i: (i % n_seq_tiles, 0)),
                pl.BlockSpec((tm, LANES), lambda i: (i % n_seq_tiles, 0)),
                pl.BlockSpec((half, tm), lambda i: (0, i % n_seq_tiles)),
                pl.BlockSpec((half, tm), lambda i: (0, i % n_seq_tiles))]
    in_specs += [_const_spec((w, 1))] * 3 + [_const_spec((1, w))] * 2
    in_specs += [_const_spec((MXU_TILE, MXU_TILE))]
    t_rows = [w, A_HEADS * A_V_ROWS, w, B_HEADS * B_V_ROWS, w]
    return pl.pallas_call(
        _in_kernel,
        grid=(n_tiles,),
        in_specs=in_specs,
        out_specs=([pl.BlockSpec((1, r, tm), lambda i: (i, 0, 0)) for r in t_rows]
                   + [pl.BlockSpec((tm, w), tile)] * 2),
        out_shape=([jax.ShapeDtypeStruct((n_tiles, r, tm), BF16) for r in t_rows]
                   + [jax.ShapeDtypeStruct((t, w), BF16)] * 2),
        compiler_params=_params(1),
        name="in_proj",
    )(x, g, w_t, w_k, cos, sin, cos_t, sin_t, gaq, gbq, gcq, gak, gbk, g64)


def _mem_kernel(m_ref, g_ref, wk_ref, wvt_ref, gk_ref, g128_ref, mk_ref, mvt_ref):
    h = _rms(m_ref[...], g_ref[...]).astype(BF16)
    mk_ref[...] = _group_rms(_dot(h, wk_ref[...]), g128_ref, gk_ref[...]).astype(BF16)
    mvt_ref[0] = _with_ones(_dot_nt(wvt_ref[...], h).astype(BF16), C_DIM)


def _mem_proj(mem, g, w_k, w_vt, gk, g128, n_mem):
    t, d = mem.shape
    w = BRANCH_W
    v_rows = C_HEADS * A_V_ROWS
    tile = lambda i: (i, 0)
    return pl.pallas_call(
        _mem_kernel,
        grid=(t // n_mem,),
        in_specs=[pl.BlockSpec((n_mem, d), tile), _const_spec((1, d)), _const_spec(w_k.shape),
                  _const_spec(w_vt.shape), _const_spec((1, w)),
                  _const_spec((MXU_TILE, MXU_TILE))],
        out_specs=[pl.BlockSpec((n_mem, w), tile),
                   pl.BlockSpec((1, v_rows, n_mem), lambda i: (i, 0, 0))],
        out_shape=[jax.ShapeDtypeStruct((t, w), BF16),
                   jax.ShapeDtypeStruct((t // n_mem, v_rows, n_mem), BF16)],
        compiler_params=_params(1),
        name="mem_proj",
    )(mem, g, w_k, w_vt, gk, g128)


def _bias_kernel(r_ref, o_ref):
    nq = TOKEN_TILE
    nk = 2 * nq
    half = nq // 2
    qc = lax.broadcasted_iota(jnp.int32, (nq, nk), 0) // CHUNK + B_LEFT_CHUNKS
    kc = lax.broadcasted_iota(jnp.int32, (nq, nk), 1) // CHUNK
    valid = (qc >= kc) & (qc - kc <= B_LEFT_CHUNKS)
    for hh in range(2):
        t = pltpu.roll(jnp.broadcast_to(r_ref[hh], (nq, nk)), 0, 1, stride=1, stride_axis=0)
        t = jnp.where(valid, t * LOG2E, NEG).T
        for ch in range(2):
            c0 = ch * nq + hh * half
            o_ref[0, :, c0:c0 + half] = t[:, ch * half:(ch + 1) * half]


def _rel_bias(rows):
    h, _, nk = rows.shape
    return pl.pallas_call(
        _bias_kernel,
        grid=(h // 2,),
        in_specs=[pl.BlockSpec((2, 1, nk), lambda i: (i, 0, 0))],
        out_specs=pl.BlockSpec((1, nk, nk), lambda i: (i, 0, 0)),
        out_shape=jax.ShapeDtypeStruct((h // 2, nk, nk), F32),
        compiler_params=_params(1),
        name="rel_bias",
    )(rows)


def _split_queries(qt):
    row = lax.broadcasted_iota(jnp.int32, qt.shape, 0)
    zero = jnp.zeros_like(qt)
    return jnp.concatenate(
        [jnp.where(row < HEAD_DIM, qt, zero), jnp.where(row >= HEAD_DIM, qt, zero)], axis=1)


def _online_softmax(s, m_prev, acc_prev, vt, s_max=None):
    if s_max is None:
        s_max = jnp.max(s, axis=0, keepdims=True)
    m_new = s_max if m_prev is None else jnp.maximum(m_prev, s_max)
    pv = _dot(vt, jnp.exp2(s - m_new).astype(BF16))
    if acc_prev is None:
        return m_new, pv
    return m_new, jnp.exp2(m_prev - m_new) * acc_prev + pv


def _normalized(acc, rows):
    return acc[:rows] / acc[rows:rows + 1]


def _attn_a_kernel(qt_ref, k_ref, vt_ref, lam_ref, gain_ref, o_ref, q12_ref, *scratch,
                   lambda_init):
    n_sub, _, ts = qt_ref.shape[1:]
    subs = A_QUERY_SUBS
    n_blk = n_sub // subs
    n_chains = 2 * subs
    look = A_LOOKAHEAD
    score_refs = scratch[:look]
    smax_refs = scratch[look:2 * look]
    state_refs = [scratch[2 * look + 2 * c:2 * look + 2 * c + 2] for c in range(n_chains)]
    for blk in range(n_blk):
        q12_ref[blk] = jnp.concatenate(
            [_split_queries(qt_ref[0, blk * subs + h]) for h in range(subs)], axis=1)
    lp = lam_ref[...]
    lam = (jnp.exp(jnp.sum(lp[0:1] * lp[1:2], axis=1, keepdims=True))
           - jnp.exp(jnp.sum(lp[2:3] * lp[3:4], axis=1, keepdims=True)) + lambda_init)

    hk = A_KEY_PART
    parts = ts // hk

    def qk(item):
        qi, jb, kp, ch, _ = item
        kb = k_ref[0, pl.ds(pl.multiple_of(jb * ts + kp * hk, hk), hk), :]
        s = _dot(kb, q12_ref[qi, :, ch * ts:(ch + 1) * ts])
        return s, jnp.max(s, axis=0, keepdims=True)

    def run(seq, following):
        scores = [(score_refs[i][...], smax_refs[i][...]) for i in range(look)]
        for i, (_, jb, kp, ch, masked) in enumerate(seq):
            s, s_max = scores.pop(0)
            if masked:
                key_chunk = (lax.broadcasted_iota(jnp.int32, (hk, ts), 0) + kp * hk) // CHUNK
                query_chunk = lax.broadcasted_iota(jnp.int32, (hk, ts), 1) // CHUNK
                s = jnp.where(key_chunk <= query_chunk, s, NEG)
                s_max = None
            nxt = i + look
            if nxt < len(seq):
                scores.append(qk(seq[nxt]))
            else:
                n = nxt - len(seq)
                score_refs[n][...], smax_refs[n][...] = qk(following[n])
            m_ref, acc_ref = state_refs[ch]
            m_ref[...], acc_ref[...] = _online_softmax(
                s, m_ref[...], acc_ref[...], vt_ref[0, jb, :, kp * hk:(kp + 1) * hk], s_max)

    def key_block(qi, jb):
        return [(qi, jb, kp, ch, False) for kp in range(parts) for ch in range(n_chains)]

    def own_blocks(qi):
        return [(qi, subs * qi + kb, kp, 2 * h + c, h == kb)
                for kb in range(subs) for kp in range(parts)
                for h in range(kb, subs) for c in range(2)]

    def reset():
        for m_ref, acc_ref in state_refs:
            m_ref[...] = jnp.full(m_ref.shape, NEG, F32)
            acc_ref[...] = jnp.zeros(acc_ref.shape, F32)

    def query_block(qi, carry):
        def body(jb, c):
            run(key_block(qi, jb), key_block(qi, jb + 1))
            return c

        lax.fori_loop(0, subs * qi, body, 0)
        nxt_qi = jnp.minimum(qi + 1, n_blk - 1)
        run(own_blocks(qi), key_block(nxt_qi, 0))
        for h in range(subs):
            o1, o2 = (_normalized(state_refs[2 * h + c][1][...], 2 * HEAD_DIM) for c in range(2))
            ot = o1 - lam * o2
            ot = ot * lax.rsqrt(jnp.mean(ot * ot, axis=0, keepdims=True) + EPS)
            o_ref[0, pl.ds(pl.multiple_of((qi * subs + h) * ts, ts), ts), :] = (
                ot.T * gain_ref[...] * (1.0 - lambda_init)).astype(BF16)
        reset()
        return carry

    reset()
    for i, item in enumerate(key_block(0, 0)[:look]):
        score_refs[i][...], smax_refs[i][...] = qk(item)
    lax.fori_loop(0, n_blk, query_block, 0)


def _attn_a(qt, k, vt, lam_p, gain, lambda_init):
    b, n_sub, w, ts = qt.shape
    s = n_sub * ts
    n_chains = 2 * A_QUERY_SUBS
    return pl.pallas_call(
        functools.partial(_attn_a_kernel, lambda_init=lambda_init),
        grid=(b, w // LANES),
        in_specs=[pl.BlockSpec((1, n_sub, LANES, ts), lambda bi, hi: (bi, 0, hi, 0)),
                  pl.BlockSpec((1, s, LANES), lambda bi, hi: (bi, 0, hi)),
                  pl.BlockSpec((1, n_sub, A_V_ROWS, ts), lambda bi, hi: (bi, 0, hi, 0)),
                  _const_spec(lam_p.shape), _const_spec((1, LANES))],
        out_specs=pl.BlockSpec((1, s, LANES), lambda bi, hi: (bi, 0, hi)),
        out_shape=jax.ShapeDtypeStruct((b, s, w), BF16),
        scratch_shapes=(
            [pltpu.VMEM((n_sub // A_QUERY_SUBS, LANES, n_chains * ts), BF16)]
            + [pltpu.VMEM((A_KEY_PART, ts), F32)] * A_LOOKAHEAD
            + [pltpu.VMEM((1, ts), F32)] * A_LOOKAHEAD
            + [pltpu.VMEM((1, ts), F32), pltpu.VMEM((A_V_ROWS, ts), F32)] * n_chains),
        compiler_params=_params(2),
        name="attn_a",
    )(qt, k, vt, lam_p, gain)


B_ITEMS = ((2, 0), (2, 1), (1, 0), (3, 1), (0, 0), (1, 1))


def _attn_b_kernel(qt_ref, k_ref, vt_ref, bias_ref, o_ref, q12_ref, *score_refs):
    n_blk, _, tq = qt_ref.shape[1:]
    hk = KEY_PART
    half = tq // 2
    assert half == hk and LOOKAHEAD == 2
    for blk in range(n_blk):
        q12 = _split_queries(qt_ref[0, blk])
        q12_ref[blk] = jnp.concatenate(
            [q12[:, :half], q12[:, tq:tq + half], q12[:, half:tq], q12[:, tq + half:]], axis=1)

    def source(qi, wp):
        blk = qi if wp >= 2 else jnp.maximum(qi - 1, 0)
        return blk, (wp % 2) * hk

    def qk(qi, item):
        wp, ch = item
        blk, r0 = source(qi, wp)
        cols = slice(ch * tq, (ch + 1) * tq)
        kb = k_ref[0, pl.ds(pl.multiple_of(blk * tq + r0, hk), hk), :]
        s = _dot(kb, q12_ref[qi, :, cols]) + bias_ref[0, wp * hk:(wp + 1) * hk, cols]
        if wp < 2:
            s = jnp.where(qi > 0, s, NEG)
        return s

    def query_block(qi, carry):
        scores = [r[...] for r in score_refs]
        state = [(None, None), (None, None)]
        for i, (wp, ch) in enumerate(B_ITEMS):
            s = scores.pop(0)
            nxt = i + LOOKAHEAD
            if nxt < len(B_ITEMS):
                scores.append(qk(qi, B_ITEMS[nxt]))
            else:
                score_refs[nxt - len(B_ITEMS)][...] = qk(
                    jnp.minimum(qi + 1, n_blk - 1), B_ITEMS[nxt - len(B_ITEMS)])
            blk, r0 = source(qi, wp)
            state[ch] = _online_softmax(s, state[ch][0], state[ch][1],
                                        vt_ref[0, blk, :, r0:r0 + hk])
        halves = []
        for _, acc in state:
            heads = [_normalized(acc[hh * B_V_ROWS:(hh + 1) * B_V_ROWS,
                                     hh * half:(hh + 1) * half], HEAD_DIM) for hh in range(2)]
            halves.append(jnp.concatenate(heads, axis=0))
        ot = jnp.concatenate(halves, axis=1)
        o_ref[0, pl.ds(pl.multiple_of(qi * tq, tq), tq), :] = ot.T.astype(BF16)
        return carry

    for i, r in enumerate(score_refs):
        r[...] = qk(0, B_ITEMS[i])
    lax.fori_loop(0, n_blk, query_block, 0)


def _attn_b(qt, k, vt, bias):
    b, n_blk, w, tq = qt.shape
    s = n_blk * tq
    return pl.pallas_call(
        _attn_b_kernel,
        grid=(w // LANES, b),
        in_specs=[pl.BlockSpec((1, n_blk, LANES, tq), lambda pi, bi: (bi, 0, pi, 0)),
                  pl.BlockSpec((1, s, LANES), lambda pi, bi: (bi, 0, pi)),
                  pl.BlockSpec((1, n_blk, 2 * B_V_ROWS, tq), lambda pi, bi: (bi, 0, pi, 0)),
                  pl.BlockSpec((1, 2 * tq, 2 * tq), lambda pi, bi: (pi, 0, 0))],
        out_specs=pl.BlockSpec((1, s, LANES), lambda pi, bi: (bi, 0, pi)),
        out_shape=jax.ShapeDtypeStruct((b, s, w), BF16),
        scratch_shapes=([pltpu.VMEM((n_blk, LANES, 2 * tq), BF16)]
                        + [pltpu.VMEM((KEY_PART, tq), F32)] * LOOKAHEAD),
        compiler_params=_params(2),
        name="attn_b",
    )(qt, k, vt, bias)


def _attn_c_kernel(qt_ref, k_ref, vt_ref, o_ref):
    n_blk, _, tq = qt_ref.shape[1:]

    def query_block(qi, carry):
        def qk(hd):
            rows = slice(hd * C_DIM, (hd + 1) * C_DIM)
            return _dot(k_ref[0, :, rows], qt_ref[0, qi, rows, :])

        scores = [qk(hd) for hd in range(LOOKAHEAD)]
        outs = []
        for hd in range(C_HEADS):
            s = scores.pop(0)
            if hd + LOOKAHEAD < C_HEADS:
                scores.append(qk(hd + LOOKAHEAD))
            _, acc = _online_softmax(s, None, None,
                                     vt_ref[0, hd * A_V_ROWS:(hd + 1) * A_V_ROWS, :])
            outs.append(_normalized(acc, C_DIM))
        o_ref[0, pl.ds(pl.multiple_of(qi * tq, tq), tq), :] = (
            jnp.concatenate(outs, axis=0).T.astype(BF16))
        return carry

    lax.fori_loop(0, n_blk, query_block, 0)


def _attn_c(qt, mk, mvt):
    b, n_blk, w, tq = qt.shape
    n_mem = mk.shape[1]
    return pl.pallas_call(
        _attn_c_kernel,
        grid=(b,),
        in_specs=[pl.BlockSpec((1, n_blk, w, tq), lambda bi: (bi, 0, 0, 0)),
                  pl.BlockSpec((1, n_mem, w), lambda bi: (bi, 0, 0)),
                  pl.BlockSpec((1, mvt.shape[1], n_mem), lambda bi: (bi, 0, 0))],
        out_specs=pl.BlockSpec((1, n_blk * tq, w), lambda bi: (bi, 0, 0)),
        out_shape=jax.ShapeDtypeStruct((b, n_blk * tq, w), BF16),
        compiler_params=_params(1),
        name="attn_c",
    )(qt, mk, mvt)


def _mix_kernel(x_ref, g_ref, wgate_ref, bgate_ref, oa_ref, ob_ref, oc_ref, wbr_ref, wout_ref,
                o_ref):
    x = x_ref[...]
    d = x.shape[1]
    h = _rms(x, g_ref[...]).astype(BF16)
    y = None
    for n, br_ref in enumerate((oa_ref, ob_ref, oc_ref)):
        z = _dot(h, wgate_ref[:, n * d:(n + 1) * d]) + bgate_ref[:, n * d:(n + 1) * d]
        term = (1.0 / (1.0 + jnp.exp(-z))) * _dot(br_ref[...], wbr_ref[n])
        y = term if y is None else y + term
    o_ref[...] = x + _dot(y.astype(BF16), wout_ref[...])


def _mix_out(x, g, w_gate, b_gate, o_a, o_b, o_c, w_branch, w_out):
    t, d = x.shape
    tm = TOKEN_TILE
    tile = lambda i: (i, 0)
    br_spec = pl.BlockSpec((tm, BRANCH_W), tile)
    return pl.pallas_call(
        _mix_kernel,
        grid=(t // tm,),
        in_specs=[pl.BlockSpec((tm, d), tile), _const_spec((1, d)), _const_spec(w_gate.shape),
                  _const_spec(b_gate.shape), br_spec, br_spec, br_spec,
                  _const_spec(w_branch.shape), _const_spec(w_out.shape)],
        out_specs=pl.BlockSpec((tm, d), tile),
        out_shape=jax.ShapeDtypeStruct((t, d), F32),
        compiler_params=_params(1),
        name="mix_out",
    )(x, g, w_gate, b_gate, o_a, o_b, o_c, w_branch, w_out)


def _rope_tables(seq):
    half = HEAD_DIM // 2
    inv = ROPE_THETA ** (-jnp.arange(half, dtype=F32) / half)
    ang = jnp.arange(seq, dtype=F32)[:, None] * inv[None, :]
    cos, sin = jnp.cos(ang), jnp.sin(ang)
    reps = LANES // HEAD_DIM
    return (jnp.tile(jnp.concatenate([cos, cos], axis=1), (1, reps)),
            jnp.tile(jnp.concatenate([-sin, sin], axis=1), (1, reps)),
            cos.T, sin.T)


def _group_mean_matrix(group):
    idx = jnp.arange(MXU_TILE) // group
    return jnp.where(idx[:, None] == idx[None, :], 1.0 / group, 0.0).astype(BF16)


def _bias_rows(table):
    h = table.shape[0]
    far = jnp.broadcast_to(table[:, -1:], (h, TOKEN_TILE - REL_CLIP))
    near = table[:, ::-1]
    tail = jnp.broadcast_to(table[:, -1:], (h, 2 * TOKEN_TILE - far.shape[1] - near.shape[1]))
    return jnp.concatenate([far, near, tail], axis=1)[:, None, :]


def kernel(x, mem, ffn1_norm, ffn1_wg, ffn1_wu, ffn1_wd, mix_norm, w_in, a_q_norm, a_k_norm,
           a_lambda, a_sub_norm, b_q_norm, b_k_norm, b_rel_bias, mem_norm, w_mem_kv, c_q_norm,
           c_k_norm, w_gate, b_gate, w_branch, w_out, ffn2_norm, ffn2_wg, ffn2_wu, ffn2_wd,
           final_norm):
    b, s, d = x.shape
    n_mem = mem.shape[1]
    depth = ffn1_norm.shape[0]
    tm = TOKEN_TILE
    w = BRANCH_W
    assert s % tm == 0 and d % MXU_TILE == 0 and w_in.shape[2] == 7 * w

    tables = _rope_tables(s)
    g64 = _group_mean_matrix(HEAD_DIM)
    g128 = _group_mean_matrix(C_DIM)
    row = lambda v: v.reshape(1, -1)
    heads = lambda v: jnp.tile(v, w // v.shape[0])
    cols = lambda i, j: slice(i * w, j * w)

    xt = x.reshape(b * s, d)
    memt = mem.reshape(b * n_mem, d)
    for l in range(depth):
        lambda_init = 0.8 - 0.6 * math.exp(-0.3 * l)
        xt = _ffn(xt, row(ffn1_norm[l]), ffn1_wg[l].astype(BF16), ffn1_wu[l].astype(BF16),
                  ffn1_wd[l].astype(BF16))

        wl = w_in[l].astype(BF16)
        w_t = jnp.concatenate([wl[:, cols(0, 1)], wl[:, cols(2, 4)], wl[:, cols(5, 7)]], axis=1).T
        w_k = jnp.concatenate([wl[:, cols(1, 2)], wl[:, cols(4, 5)]], axis=1)
        aqt, avt, bqt, bvt, cqt, ak, bk = _in_proj(
            xt, row(mix_norm[l]), w_t, w_k, tables,
            heads(a_q_norm[l]).reshape(w, 1), heads(b_q_norm[l]).reshape(w, 1),
            heads(c_q_norm[l]).reshape(w, 1), row(heads(a_k_norm[l])), row(heads(b_k_norm[l])),
            g64, s)
        wm = w_mem_kv[l].astype(BF16)
        mk, mvt = _mem_proj(memt, row(mem_norm[l]), wm[:, :w], wm[:, w:].T,
                            row(heads(c_k_norm[l])), g128, n_mem)
        bias = _rel_bias(_bias_rows(b_rel_bias[l]))

        blocks = lambda a: a.reshape(b, s // tm, a.shape[1], tm)
        seq3 = lambda a: a.reshape(b, s, w)
        o_a = _attn_a(blocks(aqt), seq3(ak), blocks(avt), a_lambda[l], row(a_sub_norm[l]),
                      lambda_init)
        o_b = _attn_b(blocks(bqt), seq3(bk), blocks(bvt), bias)
        o_c = _attn_c(blocks(cqt), mk.reshape(b, n_mem, w), mvt)

        flat = lambda a: a.reshape(b * s, w)
        xt = _mix_out(xt, row(mix_norm[l]), w_gate[l].astype(BF16), row(b_gate[l]),
                      flat(o_a), flat(o_b), flat(o_c), w_branch[l].astype(BF16),
                      w_out[l].astype(BF16))

        xt = _ffn(xt, row(ffn2_norm[l]), ffn2_wg[l].astype(BF16), ffn2_wu[l].astype(BF16),
                  ffn2_wd[l].astype(BF16), row(final_norm[l]))
    return xt.reshape(b, s, d)
```

```python
import functools
import math

import jax
import jax.numpy as jnp
from jax import lax
from jax.experimental import pallas as pl
from jax.experimental.pallas import tpu as pltpu

F32 = jnp.float32
BF16 = jnp.bfloat16

CHUNK = 64
HEAD_DIM = 64
A_HEADS = 4
B_HEADS = 8
B_LEFT_CHUNKS = 8
REL_CLIP = 128
C_HEADS = 4
C_DIM = 128
N_BRANCH = 3
BRANCH_W = 512
ROPE_THETA = 10000.0
EPS = 1e-6
NEG = -1e30
LOG2E = math.log2(math.e)

LANES = 128
MXU_TILE = 256
BF16_ROWS = 16
TOKEN_TILE = 512
FFN_TILE = 1024
IN_TILES = 2
KEY_PART = MXU_TILE
LOOKAHEAD = 2
A_KEY_PART = MXU_TILE
A_LOOKAHEAD = 2
A_QUERY_SUBS = 2
B_QUERY_SUBS = 2
B_BAND_PARTS = 3
ONES_ROWS = BF16_ROWS
VMEM_LIMIT = 56 * 1024 * 1024

A_V_ROWS = 2 * HEAD_DIM + ONES_ROWS
B_V_ROWS = HEAD_DIM + ONES_ROWS

assert TOKEN_TILE == B_LEFT_CHUNKS * CHUNK


def _dot(a, b):
    return jnp.dot(a, b, preferred_element_type=F32)


def _dot_nt(a, b):
    return lax.dot_general(a, b, (((1,), (1,)), ((), ())), preferred_element_type=F32)


def _rms(x, g):
    return x * lax.rsqrt(jnp.mean(x * x, axis=-1, keepdims=True) + EPS) * g


def _group_rms(z, gmat_ref, gain):
    z2 = (z * z).astype(BF16)
    ms = jnp.concatenate(
        [_dot(z2[:, :MXU_TILE], gmat_ref[...]), _dot(z2[:, MXU_TILE:], gmat_ref[...])], axis=1)
    return z * lax.rsqrt(ms + EPS) * gain


def _heads_t(z, gain, group, scale, cos=None, sin=None):
    outs = []
    for g0 in range(0, z.shape[0], group):
        zz = z[g0:g0 + group]
        r = lax.rsqrt(jnp.mean(zz * zz, axis=0, keepdims=True) + EPS) * scale
        y = zz * r * gain[g0:g0 + group]
        if cos is not None:
            y1, y2 = y[:group // 2], y[group // 2:]
            y = jnp.concatenate([y1 * cos - y2 * sin, y2 * cos + y1 * sin], axis=0)
        outs.append(y)
    return jnp.concatenate(outs, axis=0)


def _with_ones(vt, head_rows):
    ones = jnp.ones((ONES_ROWS, vt.shape[1]), vt.dtype)
    parts = []
    for r0 in range(0, vt.shape[0], head_rows):
        parts += [vt[r0:r0 + head_rows], ones]
    return jnp.concatenate(parts, axis=0)


def _const_spec(shape):
    nd = len(shape)
    return pl.BlockSpec(shape, lambda *_: (0,) * nd, pipeline_mode=pl.Buffered(1))


def _params(n_axes):
    return pltpu.CompilerParams(
        dimension_semantics=("arbitrary",) * n_axes, vmem_limit_bytes=VMEM_LIMIT)


def _ffn_chunks(d_ff):
    step = 3 * MXU_TILE
    return [(c, min(c + step, d_ff)) for c in range(0, d_ff, step)]


def _ffn_kernel(x_ref, g_ref, wg_ref, wu_ref, wd_ref, *rest, final_norm):
    if final_norm:
        fg_ref, o_ref, act_ref = rest
    else:
        o_ref, act_ref = rest
    x = x_ref[...]
    h = _rms(x, g_ref[...]).astype(BF16)
    for c0, c1 in _ffn_chunks(wg_ref.shape[1]):
        g = _dot(h, wg_ref[:, c0:c1])
        u = _dot(h, wu_ref[:, c0:c1])
        act_ref[:, c0:c1] = (g * (1.0 / (1.0 + jnp.exp(-g))) * u).astype(BF16)
    out = x + 0.5 * _dot(act_ref[...], wd_ref[...])
    if final_norm:
        out = _rms(out, fg_ref[...])
    o_ref[...] = out


def _ffn(x, g, wg, wu, wd, final_g=None):
    t, d = x.shape
    f = wg.shape[1]
    tm = FFN_TILE
    in_specs = [pl.BlockSpec((tm, d), lambda i: (i, 0)), _const_spec((1, d)),
                _const_spec((d, f)), _const_spec((d, f)), _const_spec((f, d))]
    args = [x, g, wg, wu, wd]
    if final_g is not None:
        in_specs.append(_const_spec((1, d)))
        args.append(final_g)
    return pl.pallas_call(
        functools.partial(_ffn_kernel, final_norm=final_g is not None),
        grid=(t // tm,),
        in_specs=in_specs,
        out_specs=pl.BlockSpec((tm, d), lambda i: (i, 0)),
        out_shape=jax.ShapeDtypeStruct((t, d), F32),
        scratch_shapes=[pltpu.VMEM((tm, f), BF16)],
        compiler_params=_params(1),
        name="ffn_final" if final_g is not None else "ffn",
    )(*args)


def _rope(q, cos, sin):
    lane = lax.broadcasted_iota(jnp.int32, (q.shape[0], LANES), 1)
    low_half = (lane & 32) == 0
    outs = []
    for c in range(q.shape[1] // LANES):
        xc = q[:, c * LANES:(c + 1) * LANES]
        partner = jnp.where(low_half, pltpu.roll(xc, LANES - 32, 1), pltpu.roll(xc, 32, 1))
        outs.append(xc * cos + partner * sin)
    return jnp.concatenate(outs, axis=1)


def _in_kernel(x_ref, g_ref, wt_ref, wk_ref, cos_ref, sin_ref, cost_ref, sint_ref,
               gaq_ref, gbq_ref, gcq_ref, gak_ref, gbk_ref, g64_ref,
               aqt_ref, avt_ref, bqt_ref, bvt_ref, cqt_ref, ak_ref, bk_ref):
    w = BRANCH_W
    tm = TOKEN_TILE
    qk_scale = HEAD_DIM ** -0.5 * LOG2E
    for sub in range(IN_TILES):
        rows = slice(sub * tm, (sub + 1) * tm)
        h = _rms(x_ref[rows, :], g_ref[...]).astype(BF16)

        def seg_t(i):
            return _dot_nt(wt_ref[i * w:(i + 1) * w, :], h)

        aqt_ref[sub] = _heads_t(seg_t(0), gaq_ref[...], HEAD_DIM, qk_scale,
                                cost_ref[:, rows], sint_ref[:, rows]).astype(BF16)
        avt_ref[sub] = _with_ones(seg_t(1).astype(BF16), 2 * HEAD_DIM)
        bqt_ref[sub] = _heads_t(seg_t(2), gbq_ref[...], HEAD_DIM, qk_scale).astype(BF16)
        bvt_ref[sub] = _with_ones(seg_t(3).astype(BF16), HEAD_DIM)
        cqt_ref[sub] = _heads_t(seg_t(4), gcq_ref[...], C_DIM, C_DIM ** -0.5 * LOG2E
                                ).astype(BF16)

        zk = _dot(h, wk_ref[:, :w])
        ak_ref[rows, :] = _rope(_group_rms(zk, g64_ref, gak_ref[...]), cos_ref[rows, :],
                                sin_ref[rows, :]).astype(BF16)
        bk_ref[rows, :] = _group_rms(_dot(h, wk_ref[:, w:]), g64_ref, gbk_ref[...]).astype(BF16)


def _in_proj(x, g, w_t, w_k, tables, gaq, gbq, gcq, gak, gbk, g64, seq):
    t, d = x.shape
    tm = TOKEN_TILE
    ts = IN_TILES * tm
    n_tiles = t // tm
    n_seq_steps = seq // ts
    w = BRANCH_W
    cos, sin, cos_t, sin_t = tables
    tile = lambda i: (i, 0)
    half = HEAD_DIM // 2
    in_specs = [pl.BlockSpec((ts, d), tile), _const_spec((1, d)), _const_spec(w_t.shape),
                _const_spec(w_k.shape),
                pl.BlockSpec((ts, LANES), lambda i: (i % n_seq_steps, 0)),
                pl.BlockSpec((ts, LANES), lambda i: (i % n_seq_steps, 0)),
                pl.BlockSpec((half, ts), lambda i: (0, i % n_seq_steps)),
                pl.BlockSpec((half, ts), lambda i: (0, i % n_seq_steps))]
    in_specs += [_const_spec((w, 1))] * 3 + [_const_spec((1, w))] * 2
    in_specs += [_const_spec((MXU_TILE, MXU_TILE))]
    t_rows = [w, A_HEADS * A_V_ROWS, w, B_HEADS * B_V_ROWS, w]
    return pl.pallas_call(
        _in_kernel,
        grid=(t // ts,),
        in_specs=in_specs,
        out_specs=([pl.BlockSpec((IN_TILES, r, tm), lambda i: (i, 0, 0)) for r in t_rows]
                   + [pl.BlockSpec((ts, w), tile)] * 2),
        out_shape=([jax.ShapeDtypeStruct((n_tiles, r, tm), BF16) for r in t_rows]
                   + [jax.ShapeDtypeStruct((t, w), BF16)] * 2),
        compiler_params=_params(1),
        name="in_proj",
    )(x, g, w_t, w_k, cos, sin, cos_t, sin_t, gaq, gbq, gcq, gak, gbk, g64)


def _mem_kernel(m_ref, g_ref, wk_ref, wvt_ref, gk_ref, g128_ref, mk_ref, mvt_ref):
    h = _rms(m_ref[...], g_ref[...]).astype(BF16)
    mk_ref[...] = _group_rms(_dot(h, wk_ref[...]), g128_ref, gk_ref[...]).astype(BF16)
    mvt_ref[0] = _with_ones(_dot_nt(wvt_ref[...], h).astype(BF16), C_DIM)


def _mem_proj(mem, g, w_k, w_vt, gk, g128, n_mem):
    t, d = mem.shape
    w = BRANCH_W
    v_rows = C_HEADS * A_V_ROWS
    tile = lambda i: (i, 0)
    return pl.pallas_call(
        _mem_kernel,
        grid=(t // n_mem,),
        in_specs=[pl.BlockSpec((n_mem, d), tile), _const_spec((1, d)), _const_spec(w_k.shape),
                  _const_spec(w_vt.shape), _const_spec((1, w)),
                  _const_spec((MXU_TILE, MXU_TILE))],
        out_specs=[pl.BlockSpec((n_mem, w), tile),
                   pl.BlockSpec((1, v_rows, n_mem), lambda i: (i, 0, 0))],
        out_shape=[jax.ShapeDtypeStruct((t, w), BF16),
                   jax.ShapeDtypeStruct((t // n_mem, v_rows, n_mem), BF16)],
        compiler_params=_params(1),
        name="mem_proj",
    )(mem, g, w_k, w_vt, gk, g128)


def _bias_kernel(r_ref, o_ref):
    nq = TOKEN_TILE
    nk = 2 * nq
    hk = KEY_PART
    qc = lax.broadcasted_iota(jnp.int32, (nq, nk), 0) // CHUNK + B_LEFT_CHUNKS
    kc = lax.broadcasted_iota(jnp.int32, (nq, nk), 1) // CHUNK
    valid = (qc >= kc) & (qc - kc <= B_LEFT_CHUNKS)
    for hh in range(2):
        t = pltpu.roll(jnp.broadcast_to(r_ref[hh], (nq, nk)), 0, 1, stride=1, stride_axis=0)
        t = jnp.where(valid, t * LOG2E, NEG).T
        o_ref[0, :, hh * hk:(hh + 1) * hk] = t[:B_BAND_PARTS * hk, :hk]


def _rel_bias(rows):
    h, _, nk = rows.shape
    tile = (B_BAND_PARTS * KEY_PART, 2 * KEY_PART)
    return pl.pallas_call(
        _bias_kernel,
        grid=(h // 2,),
        in_specs=[pl.BlockSpec((2, 1, nk), lambda i: (i, 0, 0))],
        out_specs=pl.BlockSpec((1,) + tile, lambda i: (i, 0, 0)),
        out_shape=jax.ShapeDtypeStruct((h // 2,) + tile, F32),
        compiler_params=_params(1),
        name="rel_bias",
    )(rows)


def _split_queries(qt):
    row = lax.broadcasted_iota(jnp.int32, qt.shape, 0)
    zero = jnp.zeros_like(qt)
    return jnp.concatenate(
        [jnp.where(row < HEAD_DIM, qt, zero), jnp.where(row >= HEAD_DIM, qt, zero)], axis=1)


def _online_softmax(s, m_prev, acc_prev, vt, s_max=None):
    if s_max is None:
        s_max = jnp.max(s, axis=0, keepdims=True)
    m_new = s_max if m_prev is None else jnp.maximum(m_prev, s_max)
    pv = _dot(vt, jnp.exp2(s - m_new).astype(BF16))
    if acc_prev is None:
        return m_new, pv
    return m_new, jnp.exp2(m_prev - m_new) * acc_prev + pv


def _normalized(acc, rows):
    return acc[:rows] / acc[rows:rows + 1]


def _attn_a_kernel(qt_ref, k_ref, vt_ref, lam_ref, gain_ref, o_ref, q12_ref, *scratch,
                   lambda_init):
    n_sub, _, ts = qt_ref.shape[1:]
    subs = A_QUERY_SUBS
    n_blk = n_sub // subs
    n_chains = 2 * subs
    look = A_LOOKAHEAD
    score_refs = scratch[:look]
    smax_refs = scratch[look:2 * look]
    state_refs = [scratch[2 * look + 2 * c:2 * look + 2 * c + 2] for c in range(n_chains)]
    for blk in range(n_blk):
        q12_ref[blk] = jnp.concatenate(
            [_split_queries(qt_ref[0, blk * subs + h]) for h in range(subs)], axis=1)
    lp = lam_ref[...]
    lam = (jnp.exp(jnp.sum(lp[0:1] * lp[1:2], axis=1, keepdims=True))
           - jnp.exp(jnp.sum(lp[2:3] * lp[3:4], axis=1, keepdims=True)) + lambda_init)

    hk = A_KEY_PART
    parts = ts // hk

    def qk(item):
        qi, jb, kp, ch, _ = item
        kb = k_ref[0, pl.ds(pl.multiple_of(jb * ts + kp * hk, hk), hk), :]
        s = _dot(kb, q12_ref[qi, :, ch * ts:(ch + 1) * ts])
        return s, jnp.max(s, axis=0, keepdims=True)

    def run(seq, following):
        scores = [(score_refs[i][...], smax_refs[i][...]) for i in range(look)]
        for i, (_, jb, kp, ch, masked) in enumerate(seq):
            s, s_max = scores.pop(0)
            if masked:
                key_chunk = (lax.broadcasted_iota(jnp.int32, (hk, ts), 0) + kp * hk) // CHUNK
                query_chunk = lax.broadcasted_iota(jnp.int32, (hk, ts), 1) // CHUNK
                s = jnp.where(key_chunk <= query_chunk, s, NEG)
                s_max = None
            nxt = i + look
            if nxt < len(seq):
                scores.append(qk(seq[nxt]))
            else:
                n = nxt - len(seq)
                score_refs[n][...], smax_refs[n][...] = qk(following[n])
            m_ref, acc_ref = state_refs[ch]
            m_ref[...], acc_ref[...] = _online_softmax(
                s, m_ref[...], acc_ref[...], vt_ref[0, jb, :, kp * hk:(kp + 1) * hk], s_max)

    def key_block(qi, jb):
        return [(qi, jb, kp, ch, False) for kp in range(parts) for ch in range(n_chains)]

    def own_blocks(qi):
        return [(qi, subs * qi + kb, kp, 2 * h + c, h == kb)
                for kb in range(subs) for kp in range(parts)
                for h in range(kb, subs) for c in range(2)]

    def reset():
        for m_ref, acc_ref in state_refs:
            m_ref[...] = jnp.full(m_ref.shape, NEG, F32)
            acc_ref[...] = jnp.zeros(acc_ref.shape, F32)

    def query_block(qi, carry):
        def body(jb, c):
            run(key_block(qi, jb), key_block(qi, jb + 1))
            return c

        lax.fori_loop(0, subs * qi, body, 0)
        nxt_qi = jnp.minimum(qi + 1, n_blk - 1)
        run(own_blocks(qi), key_block(nxt_qi, 0))
        for h in range(subs):
            o1, o2 = (_normalized(state_refs[2 * h + c][1][...], 2 * HEAD_DIM) for c in range(2))
            ot = o1 - lam * o2
            ot = ot * lax.rsqrt(jnp.mean(ot * ot, axis=0, keepdims=True) + EPS)
            o_ref[0, pl.ds(pl.multiple_of((qi * subs + h) * ts, ts), ts), :] = (
                ot.T * gain_ref[...] * (1.0 - lambda_init)).astype(BF16)
        reset()
        return carry

    reset()
    for i, item in enumerate(key_block(0, 0)[:look]):
        score_refs[i][...], smax_refs[i][...] = qk(item)
    lax.fori_loop(0, n_blk, query_block, 0)


def _attn_a(qt, k, vt, lam_p, gain, lambda_init):
    b, n_sub, w, ts = qt.shape
    s = n_sub * ts
    n_chains = 2 * A_QUERY_SUBS
    return pl.pallas_call(
        functools.partial(_attn_a_kernel, lambda_init=lambda_init),
        grid=(b, w // LANES),
        in_specs=[pl.BlockSpec((1, n_sub, LANES, ts), lambda bi, hi: (bi, 0, hi, 0)),
                  pl.BlockSpec((1, s, LANES), lambda bi, hi: (bi, 0, hi)),
                  pl.BlockSpec((1, n_sub, A_V_ROWS, ts), lambda bi, hi: (bi, 0, hi, 0)),
                  _const_spec(lam_p.shape), _const_spec((1, LANES))],
        out_specs=pl.BlockSpec((1, s, LANES), lambda bi, hi: (bi, 0, hi)),
        out_shape=jax.ShapeDtypeStruct((b, s, w), BF16),
        scratch_shapes=(
            [pltpu.VMEM((n_sub // A_QUERY_SUBS, LANES, n_chains * ts), BF16)]
            + [pltpu.VMEM((A_KEY_PART, ts), F32)] * A_LOOKAHEAD
            + [pltpu.VMEM((1, ts), F32)] * A_LOOKAHEAD
            + [pltpu.VMEM((1, ts), F32), pltpu.VMEM((A_V_ROWS, ts), F32)] * n_chains),
        compiler_params=_params(2),
        name="attn_a",
    )(qt, k, vt, lam_p, gain)


def _attn_b_kernel(qt_ref, k_ref, vt_ref, bias_ref, o_ref, q12_ref, *score_refs):
    n_sub, _, ts = qt_ref.shape[1:]
    hk = KEY_PART
    per_sub = ts // hk
    n_chains = B_QUERY_SUBS * per_sub
    n_blk = n_sub // B_QUERY_SUBS
    lead = B_BAND_PARTS - 1
    assert lead * hk == B_LEFT_CHUNKS * CHUNK and LOOKAHEAD <= n_chains
    for blk in range(n_blk):
        cols = []
        for c in range(n_chains):
            q12 = _split_queries(qt_ref[0, blk * B_QUERY_SUBS + c // per_sub])
            c0 = (c % per_sub) * hk
            cols += [q12[:, c0:c0 + hk], q12[:, ts + c0:ts + c0 + hk]]
        q12_ref[blk] = jnp.concatenate(cols, axis=1)

    items = [(c + d, c) for d in range(lead, -1, -1) for c in range(n_chains)]

    def source(qi, wp):
        part = qi * n_chains + wp - lead
        return jnp.maximum(part, 0) // per_sub, (wp - lead) % per_sub * hk

    def qk(qi, item):
        wp, c = item
        blk, r0 = source(qi, wp)
        cols = slice(c * 2 * hk, (c + 1) * 2 * hk)
        d = wp - c
        kb = k_ref[0, pl.ds(pl.multiple_of(blk * ts + r0, hk), hk), :]
        s = _dot(kb, q12_ref[qi, :, cols]) + bias_ref[0, d * hk:(d + 1) * hk, :]
        if wp < lead:
            s = jnp.where(qi > 0, s, NEG)
        return s

    def query_block(qi, carry):
        scores = [r[...] for r in score_refs]
        state = [(None, None)] * n_chains
        for i, (wp, c) in enumerate(items):
            s = scores.pop(0)
            nxt = i + LOOKAHEAD
            if nxt < len(items):
                scores.append(qk(qi, items[nxt]))
            else:
                score_refs[nxt - len(items)][...] = qk(
                    jnp.minimum(qi + 1, n_blk - 1), items[nxt - len(items)])
            blk, r0 = source(qi, wp)
            state[c] = _online_softmax(s, state[c][0], state[c][1],
                                       vt_ref[0, blk, :, r0:r0 + hk])
        outs = []
        for _, acc in state:
            heads = [_normalized(acc[hh * B_V_ROWS:(hh + 1) * B_V_ROWS,
                                     hh * hk:(hh + 1) * hk], HEAD_DIM) for hh in range(2)]
            outs.append(jnp.concatenate(heads, axis=0))
        for h in range(B_QUERY_SUBS):
            ot = jnp.concatenate(outs[h * per_sub:(h + 1) * per_sub], axis=1)
            o_ref[0, pl.ds(pl.multiple_of((qi * B_QUERY_SUBS + h) * ts, ts), ts), :] = (
                ot.T.astype(BF16))
        return carry

    for i, r in enumerate(score_refs):
        r[...] = qk(0, items[i])
    lax.fori_loop(0, n_blk, query_block, 0)


def _attn_b(qt, k, vt, bias):
    b, n_sub, w, ts = qt.shape
    s = n_sub * ts
    return pl.pallas_call(
        _attn_b_kernel,
        grid=(w // LANES, b),
        in_specs=[pl.BlockSpec((1, n_sub, LANES, ts), lambda pi, bi: (bi, 0, pi, 0)),
                  pl.BlockSpec((1, s, LANES), lambda pi, bi: (bi, 0, pi)),
                  pl.BlockSpec((1, n_sub, 2 * B_V_ROWS, ts), lambda pi, bi: (bi, 0, pi, 0)),
                  pl.BlockSpec((1,) + bias.shape[1:], lambda pi, bi: (pi, 0, 0))],
        out_specs=pl.BlockSpec((1, s, LANES), lambda pi, bi: (bi, 0, pi)),
        out_shape=jax.ShapeDtypeStruct((b, s, w), BF16),
        scratch_shapes=([pltpu.VMEM((n_sub // B_QUERY_SUBS, LANES, 2 * B_QUERY_SUBS * ts), BF16)]
                        + [pltpu.VMEM((KEY_PART, 2 * KEY_PART), F32)] * LOOKAHEAD),
        compiler_params=_params(2),
        name="attn_b",
    )(qt, k, vt, bias)


def _attn_c_kernel(qt_ref, k_ref, vt_ref, o_ref):
    n_blk, _, tq = qt_ref.shape[1:]

    def query_block(qi, carry):
        def qk(hd):
            rows = slice(hd * C_DIM, (hd + 1) * C_DIM)
            return _dot(k_ref[0, :, rows], qt_ref[0, qi, rows, :])

        scores = [qk(hd) for hd in range(LOOKAHEAD)]
        outs = []
        for hd in range(C_HEADS):
            s = scores.pop(0)
            if hd + LOOKAHEAD < C_HEADS:
                scores.append(qk(hd + LOOKAHEAD))
            _, acc = _online_softmax(s, None, None,
                                     vt_ref[0, hd * A_V_ROWS:(hd + 1) * A_V_ROWS, :])
            outs.append(_normalized(acc, C_DIM))
        o_ref[0, pl.ds(pl.multiple_of(qi * tq, tq), tq), :] = (
            jnp.concatenate(outs, axis=0).T.astype(BF16))
        return carry

    lax.fori_loop(0, n_blk, query_block, 0)


def _attn_c(qt, mk, mvt):
    b, n_blk, w, tq = qt.shape
    n_mem = mk.shape[1]
    return pl.pallas_call(
        _attn_c_kernel,
        grid=(b,),
        in_specs=[pl.BlockSpec((1, n_blk, w, tq), lambda bi: (bi, 0, 0, 0)),
                  pl.BlockSpec((1, n_mem, w), lambda bi: (bi, 0, 0)),
                  pl.BlockSpec((1, mvt.shape[1], n_mem), lambda bi: (bi, 0, 0))],
        out_specs=pl.BlockSpec((1, n_blk * tq, w), lambda bi: (bi, 0, 0)),
        out_shape=jax.ShapeDtypeStruct((b, n_blk * tq, w), BF16),
        compiler_params=_params(1),
        name="attn_c",
    )(qt, mk, mvt)


def _mix_kernel(x_ref, g_ref, wgate_ref, bgate_ref, oa_ref, ob_ref, oc_ref, wbr_ref, wout_ref,
                o_ref):
    x = x_ref[...]
    d = x.shape[1]
    h = _rms(x, g_ref[...]).astype(BF16)
    y = None
    for n, br_ref in enumerate((oa_ref, ob_ref, oc_ref)):
        z = _dot(h, wgate_ref[:, n * d:(n + 1) * d]) + bgate_ref[:, n * d:(n + 1) * d]
        term = (1.0 / (1.0 + jnp.exp(-z))) * _dot(br_ref[...], wbr_ref[n])
        y = term if y is None else y + term
    o_ref[...] = x + _dot(y.astype(BF16), wout_ref[...])


def _mix_out(x, g, w_gate, b_gate, o_a, o_b, o_c, w_branch, w_out):
    t, d = x.shape
    tm = FFN_TILE
    tile = lambda i: (i, 0)
    br_spec = pl.BlockSpec((tm, BRANCH_W), tile)
    return pl.pallas_call(
        _mix_kernel,
        grid=(t // tm,),
        in_specs=[pl.BlockSpec((tm, d), tile), _const_spec((1, d)), _const_spec(w_gate.shape),
                  _const_spec(b_gate.shape), br_spec, br_spec, br_spec,
                  _const_spec(w_branch.shape), _const_spec(w_out.shape)],
        out_specs=pl.BlockSpec((tm, d), tile),
        out_shape=jax.ShapeDtypeStruct((t, d), F32),
        compiler_params=_params(1),
        name="mix_out",
    )(x, g, w_gate, b_gate, o_a, o_b, o_c, w_branch, w_out)


def _rope_tables(seq):
    half = HEAD_DIM // 2
    inv = ROPE_THETA ** (-jnp.arange(half, dtype=F32) / half)
    ang = jnp.arange(seq, dtype=F32)[:, None] * inv[None, :]
    cos, sin = jnp.cos(ang), jnp.sin(ang)
    reps = LANES // HEAD_DIM
    return (jnp.tile(jnp.concatenate([cos, cos], axis=1), (1, reps)),
            jnp.tile(jnp.concatenate([-sin, sin], axis=1), (1, reps)),
            cos.T, sin.T)


def _group_mean_matrix(group):
    idx = jnp.arange(MXU_TILE) // group
    return jnp.where(idx[:, None] == idx[None, :], 1.0 / group, 0.0).astype(BF16)


def _bias_rows(table):
    h = table.shape[0]
    far = jnp.broadcast_to(table[:, -1:], (h, TOKEN_TILE - REL_CLIP))
    near = table[:, ::-1]
    tail = jnp.broadcast_to(table[:, -1:], (h, 2 * TOKEN_TILE - far.shape[1] - near.shape[1]))
    return jnp.concatenate([far, near, tail], axis=1)[:, None, :]


def kernel(x, mem, ffn1_norm, ffn1_wg, ffn1_wu, ffn1_wd, mix_norm, w_in, a_q_norm, a_k_norm,
           a_lambda, a_sub_norm, b_q_norm, b_k_norm, b_rel_bias, mem_norm, w_mem_kv, c_q_norm,
           c_k_norm, w_gate, b_gate, w_branch, w_out, ffn2_norm, ffn2_wg, ffn2_wu, ffn2_wd,
           final_norm):
    b, s, d = x.shape
    n_mem = mem.shape[1]
    depth = ffn1_norm.shape[0]
    tm = TOKEN_TILE
    w = BRANCH_W
    assert s % tm == 0 and d % MXU_TILE == 0 and w_in.shape[2] == 7 * w

    tables = _rope_tables(s)
    g64 = _group_mean_matrix(HEAD_DIM)
    g128 = _group_mean_matrix(C_DIM)
    row = lambda v: v.reshape(1, -1)
    heads = lambda v: jnp.tile(v, w // v.shape[0])
    cols = lambda i, j: slice(i * w, j * w)

    xt = x.reshape(b * s, d)
    memt = mem.reshape(b * n_mem, d)
    for l in range(depth):
        lambda_init = 0.8 - 0.6 * math.exp(-0.3 * l)
        xt = _ffn(xt, row(ffn1_norm[l]), ffn1_wg[l].astype(BF16), ffn1_wu[l].astype(BF16),
                  ffn1_wd[l].astype(BF16))

        wl = w_in[l].astype(BF16)
        w_t = jnp.concatenate([wl[:, cols(0, 1)], wl[:, cols(2, 4)], wl[:, cols(5, 7)]], axis=1).T
        w_k = jnp.concatenate([wl[:, cols(1, 2)], wl[:, cols(4, 5)]], axis=1)
        aqt, avt, bqt, bvt, cqt, ak, bk = _in_proj(
            xt, row(mix_norm[l]), w_t, w_k, tables,
            heads(a_q_norm[l]).reshape(w, 1), heads(b_q_norm[l]).reshape(w, 1),
            heads(c_q_norm[l]).reshape(w, 1), row(heads(a_k_norm[l])), row(heads(b_k_norm[l])),
            g64, s)
        wm = w_mem_kv[l].astype(BF16)
        mk, mvt = _mem_proj(memt, row(mem_norm[l]), wm[:, :w], wm[:, w:].T,
                            row(heads(c_k_norm[l])), g128, n_mem)
        bias = _rel_bias(_bias_rows(b_rel_bias[l]))

        blocks = lambda a: a.reshape(b, s // tm, a.shape[1], tm)
        seq3 = lambda a: a.reshape(b, s, w)
        o_a = _attn_a(blocks(aqt), seq3(ak), blocks(avt), a_lambda[l], row(a_sub_norm[l]),
                      lambda_init)
        o_b = _attn_b(blocks(bqt), seq3(bk), blocks(bvt), bias)
        o_c = _attn_c(blocks(cqt), mk.reshape(b, n_mem, w), mvt)

        flat = lambda a: a.reshape(b * s, w)
        xt = _mix_out(xt, row(mix_norm[l]), w_gate[l].astype(BF16), row(b_gate[l]),
                      flat(o_a), flat(o_b), flat(o_c), w_branch[l].astype(BF16),
                      w_out[l].astype(BF16))

        xt = _ffn(xt, row(ffn2_norm[l]), ffn2_wg[l].astype(BF16), ffn2_wu[l].astype(BF16),
                  ffn2_wd[l].astype(BF16), row(final_norm[l]))
    return xt.reshape(b, s, d)
```

```python
import functools
import math

import jax
import jax.numpy as jnp
from jax import lax
from jax.experimental import pallas as pl
from jax.experimental.pallas import tpu as pltpu

F32 = jnp.float32
BF16 = jnp.bfloat16

CHUNK = 64
HEAD_DIM = 64
A_HEADS = 4
B_HEADS = 8
B_LEFT_CHUNKS = 8
REL_CLIP = 128
C_HEADS = 4
C_DIM = 128
N_BRANCH = 3
BRANCH_W = 512
ROPE_THETA = 10000.0
EPS = 1e-6
NEG = -1e30
LOG2E = math.log2(math.e)

LANES = 128
MXU_TILE = 256
BF16_ROWS = 16
TOKEN_TILE = 512
FFN_TILE = 1024
IN_TILES = 2
KEY_PART = MXU_TILE
LOOKAHEAD = 2
A_KEY_PART = MXU_TILE
A_LOOKAHEAD = 2
A_QUERY_SUBS = 2
B_QUERY_SUBS = 2
B_BAND_PARTS = 3
ONES_ROWS = BF16_ROWS
VMEM_LIMIT = 56 * 1024 * 1024

A_V_ROWS = 2 * HEAD_DIM + ONES_ROWS
B_V_ROWS = HEAD_DIM + ONES_ROWS

assert TOKEN_TILE == B_LEFT_CHUNKS * CHUNK


def _dot(a, b):
    return jnp.dot(a, b, preferred_element_type=F32)


def _dot_nt(a, b):
    return lax.dot_general(a, b, (((1,), (1,)), ((), ())), preferred_element_type=F32)


def _rms(x, g):
    return x * lax.rsqrt(jnp.mean(x * x, axis=-1, keepdims=True) + EPS) * g


def _group_rms(z, gmat_ref, gain):
    z2 = (z * z).astype(BF16)
    ms = jnp.concatenate(
        [_dot(z2[:, :MXU_TILE], gmat_ref[...]), _dot(z2[:, MXU_TILE:], gmat_ref[...])], axis=1)
    return z * lax.rsqrt(ms + EPS) * gain


def _heads_t(z, gain, group, scale, cos=None, sin=None):
    outs = []
    for g0 in range(0, z.shape[0], group):
        zz = z[g0:g0 + group]
        r = lax.rsqrt(jnp.mean(zz * zz, axis=0, keepdims=True) + EPS) * scale
        y = zz * r * gain[g0:g0 + group]
        if cos is not None:
            y1, y2 = y[:group // 2], y[group // 2:]
            y = jnp.concatenate([y1 * cos - y2 * sin, y2 * cos + y1 * sin], axis=0)
        outs.append(y)
    return jnp.concatenate(outs, axis=0)


def _with_ones(vt, head_rows):
    ones = jnp.ones((ONES_ROWS, vt.shape[1]), vt.dtype)
    parts = []
    for r0 in range(0, vt.shape[0], head_rows):
        parts += [vt[r0:r0 + head_rows], ones]
    return jnp.concatenate(parts, axis=0)


def _const_spec(shape):
    nd = len(shape)
    return pl.BlockSpec(shape, lambda *_: (0,) * nd, pipeline_mode=pl.Buffered(1))


def _params(n_axes):
    return pltpu.CompilerParams(
        dimension_semantics=("arbitrary",) * n_axes, vmem_limit_bytes=VMEM_LIMIT)


def _ffn_chunks(d_ff):
    step = 3 * MXU_TILE
    return [(c, min(c + step, d_ff)) for c in range(0, d_ff, step)]


def _ffn_kernel(x_ref, g_ref, wg_ref, wu_ref, wd_ref, *rest, final_norm):
    if final_norm:
        fg_ref, o_ref, act_ref = rest
    else:
        o_ref, act_ref = rest
    x = x_ref[...]
    h = _rms(x, g_ref[...]).astype(BF16)
    for c0, c1 in _ffn_chunks(wg_ref.shape[1]):
        g = _dot(h, wg_ref[:, c0:c1])
        u = _dot(h, wu_ref[:, c0:c1])
        act_ref[:, c0:c1] = (g * (1.0 / (1.0 + jnp.exp(-g))) * u).astype(BF16)
    out = x + 0.5 * _dot(act_ref[...], wd_ref[...])
    if final_norm:
        out = _rms(out, fg_ref[...])
    o_ref[...] = out


def _ffn(x, g, wg, wu, wd, final_g=None):
    t, d = x.shape
    f = wg.shape[1]
    tm = FFN_TILE
    in_specs = [pl.BlockSpec((tm, d), lambda i: (i, 0)), _const_spec((1, d)),
                _const_spec((d, f)), _const_spec((d, f)), _const_spec((f, d))]
    args = [x, g, wg, wu, wd]
    if final_g is not None:
        in_specs.append(_const_spec((1, d)))
        args.append(final_g)
    return pl.pallas_call(
        functools.partial(_ffn_kernel, final_norm=final_g is not None),
        grid=(t // tm,),
        in_specs=in_specs,
        out_specs=pl.BlockSpec((tm, d), lambda i: (i, 0)),
        out_shape=jax.ShapeDtypeStruct((t, d), F32),
        scratch_shapes=[pltpu.VMEM((tm, f), BF16)],
        compiler_params=_params(1),
        name="ffn_final" if final_g is not None else "ffn",
    )(*args)


def _in_kernel(x_ref, g_ref, wt_ref, cost_ref, sint_ref,
               gaq_ref, gak_ref, gbq_ref, gbk_ref, gcq_ref,
               aqt_ref, avt_ref, bqt_ref, bvt_ref, cqt_ref, ak_ref, bk_ref):
    w = BRANCH_W
    tm = TOKEN_TILE
    qk_scale = HEAD_DIM ** -0.5 * LOG2E
    for sub in range(IN_TILES):
        rows = slice(sub * tm, (sub + 1) * tm)
        h = _rms(x_ref[rows, :], g_ref[...]).astype(BF16)
        cos, sin = cost_ref[:, rows], sint_ref[:, rows]

        def seg_t(i):
            return _dot_nt(wt_ref[i * w:(i + 1) * w, :], h)

        aqt_ref[sub] = _heads_t(seg_t(0), gaq_ref[...], HEAD_DIM, qk_scale, cos, sin
                                ).astype(BF16)
        ak_ref[rows, :] = _heads_t(seg_t(1), gak_ref[...], HEAD_DIM, 1.0, cos, sin
                                   ).T.astype(BF16)
        avt_ref[sub] = _with_ones(seg_t(2).astype(BF16), 2 * HEAD_DIM)
        bqt_ref[sub] = _heads_t(seg_t(3), gbq_ref[...], HEAD_DIM, qk_scale).astype(BF16)
        bk_ref[rows, :] = _heads_t(seg_t(4), gbk_ref[...], HEAD_DIM, 1.0).T.astype(BF16)
        bvt_ref[sub] = _with_ones(seg_t(5).astype(BF16), HEAD_DIM)
        cqt_ref[sub] = _heads_t(seg_t(6), gcq_ref[...], C_DIM, C_DIM ** -0.5 * LOG2E
                                ).astype(BF16)


def _in_proj(x, g, w_t, tables, gaq, gak, gbq, gbk, gcq, seq):
    t, d = x.shape
    tm = TOKEN_TILE
    ts = IN_TILES * tm
    n_tiles = t // tm
    n_seq_steps = seq // ts
    w = BRANCH_W
    cos_t, sin_t = tables
    tile = lambda i: (i, 0)
    half = HEAD_DIM // 2
    in_specs = [pl.BlockSpec((ts, d), tile), _const_spec((1, d)), _const_spec(w_t.shape),
                pl.BlockSpec((half, ts), lambda i: (0, i % n_seq_steps)),
                pl.BlockSpec((half, ts), lambda i: (0, i % n_seq_steps))]
    in_specs += [_const_spec((w, 1))] * 5
    t_rows = [w, A_HEADS * A_V_ROWS, w, B_HEADS * B_V_ROWS, w]
    return pl.pallas_call(
        _in_kernel,
        grid=(t // ts,),
        in_specs=in_specs,
        out_specs=([pl.BlockSpec((IN_TILES, r, tm), lambda i: (i, 0, 0)) for r in t_rows]
                   + [pl.BlockSpec((ts, w), tile)] * 2),
        out_shape=([jax.ShapeDtypeStruct((n_tiles, r, tm), BF16) for r in t_rows]
                   + [jax.ShapeDtypeStruct((t, w), BF16)] * 2),
        compiler_params=_params(1),
        name="in_proj",
    )(x, g, w_t, cos_t, sin_t, gaq, gak, gbq, gbk, gcq)


def _mem_kernel(m_ref, g_ref, wk_ref, wvt_ref, gk_ref, g128_ref, mk_ref, mvt_ref):
    h = _rms(m_ref[...], g_ref[...]).astype(BF16)
    mk_ref[...] = _group_rms(_dot(h, wk_ref[...]), g128_ref, gk_ref[...]).astype(BF16)
    mvt_ref[0] = _with_ones(_dot_nt(wvt_ref[...], h).astype(BF16), C_DIM)


def _mem_proj(mem, g, w_k, w_vt, gk, g128, n_mem):
    t, d = mem.shape
    w = BRANCH_W
    v_rows = C_HEADS * A_V_ROWS
    tile = lambda i: (i, 0)
    return pl.pallas_call(
        _mem_kernel,
        grid=(t // n_mem,),
        in_specs=[pl.BlockSpec((n_mem, d), tile), _const_spec((1, d)), _const_spec(w_k.shape),
                  _const_spec(w_vt.shape), _const_spec((1, w)),
                  _const_spec((MXU_TILE, MXU_TILE))],
        out_specs=[pl.BlockSpec((n_mem, w), tile),
                   pl.BlockSpec((1, v_rows, n_mem), lambda i: (i, 0, 0))],
        out_shape=[jax.ShapeDtypeStruct((t, w), BF16),
                   jax.ShapeDtypeStruct((t // n_mem, v_rows, n_mem), BF16)],
        compiler_params=_params(1),
        name="mem_proj",
    )(mem, g, w_k, w_vt, gk, g128)


def _bias_kernel(r_ref, o_ref):
    nq = TOKEN_TILE
    nk = 2 * nq
    hk = KEY_PART
    qc = lax.broadcasted_iota(jnp.int32, (nq, nk), 0) // CHUNK + B_LEFT_CHUNKS
    kc = lax.broadcasted_iota(jnp.int32, (nq, nk), 1) // CHUNK
    valid = (qc >= kc) & (qc - kc <= B_LEFT_CHUNKS)
    for hh in range(2):
        t = pltpu.roll(jnp.broadcast_to(r_ref[hh], (nq, nk)), 0, 1, stride=1, stride_axis=0)
        t = jnp.where(valid, t * LOG2E, NEG).T
        o_ref[0, :, hh * hk:(hh + 1) * hk] = t[:B_BAND_PARTS * hk, :hk]


def _rel_bias(rows):
    h, _, nk = rows.shape
    tile = (B_BAND_PARTS * KEY_PART, 2 * KEY_PART)
    return pl.pallas_call(
        _bias_kernel,
        grid=(h // 2,),
        in_specs=[pl.BlockSpec((2, 1, nk), lambda i: (i, 0, 0))],
        out_specs=pl.BlockSpec((1,) + tile, lambda i: (i, 0, 0)),
        out_shape=jax.ShapeDtypeStruct((h // 2,) + tile, F32),
        compiler_params=_params(1),
        name="rel_bias",
    )(rows)


def _split_queries(qt):
    row = lax.broadcasted_iota(jnp.int32, qt.shape, 0)
    zero = jnp.zeros_like(qt)
    return jnp.concatenate(
        [jnp.where(row < HEAD_DIM, qt, zero), jnp.where(row >= HEAD_DIM, qt, zero)], axis=1)


def _column_max(s):
    return jnp.max(s, axis=0, keepdims=True).astype(F32)


def _online_softmax(s, m_prev, acc_prev, vt, s_max=None):
    if s_max is None:
        s_max = _column_max(s)
    m_new = s_max if m_prev is None else jnp.maximum(m_prev, s_max)
    pv = _dot(vt, jnp.exp2(s - m_new.astype(s.dtype)).astype(BF16))
    if acc_prev is None:
        return m_new, pv
    return m_new, jnp.exp2(m_prev - m_new) * acc_prev + pv


def _normalized(acc, rows):
    return acc[:rows] / acc[rows:rows + 1]


def _attn_a_kernel(qt_ref, k_ref, vt_ref, lam_ref, gain_ref, o_ref, q12_ref, *scratch,
                   lambda_init):
    n_sub, _, ts = qt_ref.shape[1:]
    subs = A_QUERY_SUBS
    n_blk = n_sub // subs
    n_chains = 2 * subs
    look = A_LOOKAHEAD
    score_refs = scratch[:look]
    smax_refs = scratch[look:2 * look]
    state_refs = [scratch[2 * look + 2 * c:2 * look + 2 * c + 2] for c in range(n_chains)]
    for blk in range(n_blk):
        q12_ref[blk] = jnp.concatenate(
            [_split_queries(qt_ref[0, blk * subs + h]) for h in range(subs)], axis=1)
    lp = lam_ref[...]
    lam = (jnp.exp(jnp.sum(lp[0:1] * lp[1:2], axis=1, keepdims=True))
           - jnp.exp(jnp.sum(lp[2:3] * lp[3:4], axis=1, keepdims=True)) + lambda_init)

    hk = A_KEY_PART
    parts = ts // hk

    def qk(item):
        qi, jb, kp, ch, _ = item
        kb = k_ref[0, pl.ds(pl.multiple_of(jb * ts + kp * hk, hk), hk), :]
        s = _dot(kb, q12_ref[qi, :, ch * ts:(ch + 1) * ts]).astype(BF16)
        return s, _column_max(s)

    def run(seq, following):
        scores = [(score_refs[i][...], smax_refs[i][...]) for i in range(look)]
        for i, (_, jb, kp, ch, masked) in enumerate(seq):
            s, s_max = scores.pop(0)
            if masked:
                key_chunk = (lax.broadcasted_iota(jnp.int32, (hk, ts), 0) + kp * hk) // CHUNK
                query_chunk = lax.broadcasted_iota(jnp.int32, (hk, ts), 1) // CHUNK
                s = jnp.where(key_chunk <= query_chunk, s, NEG)
                s_max = None
            nxt = i + look
            if nxt < len(seq):
                scores.append(qk(seq[nxt]))
            else:
                n = nxt - len(seq)
                score_refs[n][...], smax_refs[n][...] = qk(following[n])
            m_ref, acc_ref = state_refs[ch]
            m_ref[...], acc_ref[...] = _online_softmax(
                s, m_ref[...], acc_ref[...], vt_ref[0, jb, :, kp * hk:(kp + 1) * hk], s_max)

    def key_block(qi, jb):
        return [(qi, jb, kp, ch, False) for kp in range(parts) for ch in range(n_chains)]

    def own_blocks(qi):
        return [(qi, subs * qi + kb, kp, 2 * h + c, h == kb)
                for kb in range(subs) for kp in range(parts)
                for h in range(kb, subs) for c in range(2)]

    def reset():
        for m_ref, acc_ref in state_refs:
            m_ref[...] = jnp.full(m_ref.shape, NEG, F32)
            acc_ref[...] = jnp.zeros(acc_ref.shape, F32)

    def query_block(qi, carry):
        def body(jb, c):
            run(key_block(qi, jb), key_block(qi, jb + 1))
            return c

        lax.fori_loop(0, subs * qi, body, 0)
        nxt_qi = jnp.minimum(qi + 1, n_blk - 1)
        run(own_blocks(qi), key_block(nxt_qi, 0))
        for h in range(subs):
            o1, o2 = (_normalized(state_refs[2 * h + c][1][...], 2 * HEAD_DIM) for c in range(2))
            ot = o1 - lam * o2
            ot = ot * lax.rsqrt(jnp.mean(ot * ot, axis=0, keepdims=True) + EPS)
            o_ref[0, pl.ds(pl.multiple_of((qi * subs + h) * ts, ts), ts), :] = (
                ot.T * gain_ref[...] * (1.0 - lambda_init)).astype(BF16)
        reset()
        return carry

    reset()
    for i, item in enumerate(key_block(0, 0)[:look]):
        score_refs[i][...], smax_refs[i][...] = qk(item)
    lax.fori_loop(0, n_blk, query_block, 0)


def _attn_a(qt, k, vt, lam_p, gain, lambda_init):
    b, n_sub, w, ts = qt.shape
    s = n_sub * ts
    n_chains = 2 * A_QUERY_SUBS
    return pl.pallas_call(
        functools.partial(_attn_a_kernel, lambda_init=lambda_init),
        grid=(b, w // LANES),
        in_specs=[pl.BlockSpec((1, n_sub, LANES, ts), lambda bi, hi: (bi, 0, hi, 0)),
                  pl.BlockSpec((1, s, LANES), lambda bi, hi: (bi, 0, hi)),
                  pl.BlockSpec((1, n_sub, A_V_ROWS, ts), lambda bi, hi: (bi, 0, hi, 0)),
                  _const_spec(lam_p.shape), _const_spec((1, LANES))],
        out_specs=pl.BlockSpec((1, s, LANES), lambda bi, hi: (bi, 0, hi)),
        out_shape=jax.ShapeDtypeStruct((b, s, w), BF16),
        scratch_shapes=(
            [pltpu.VMEM((n_sub // A_QUERY_SUBS, LANES, n_chains * ts), BF16)]
            + [pltpu.VMEM((A_KEY_PART, ts), BF16)] * A_LOOKAHEAD
            + [pltpu.VMEM((1, ts), F32)] * A_LOOKAHEAD
            + [pltpu.VMEM((1, ts), F32), pltpu.VMEM((A_V_ROWS, ts), F32)] * n_chains),
        compiler_params=_params(2),
        name="attn_a",
    )(qt, k, vt, lam_p, gain)


def _attn_b_kernel(qt_ref, k_ref, vt_ref, bias_ref, o_ref, q12_ref, *score_refs):
    n_sub, _, ts = qt_ref.shape[1:]
    hk = KEY_PART
    per_sub = ts // hk
    n_chains = B_QUERY_SUBS * per_sub
    n_blk = n_sub // B_QUERY_SUBS
    lead = B_BAND_PARTS - 1
    assert lead * hk == B_LEFT_CHUNKS * CHUNK and LOOKAHEAD <= n_chains
    for blk in range(n_blk):
        cols = []
        for c in range(n_chains):
            q12 = _split_queries(qt_ref[0, blk * B_QUERY_SUBS + c // per_sub])
            c0 = (c % per_sub) * hk
            cols += [q12[:, c0:c0 + hk], q12[:, ts + c0:ts + c0 + hk]]
        q12_ref[blk] = jnp.concatenate(cols, axis=1)

    items = [(c + d, c) for d in range(lead, -1, -1) for c in range(n_chains)]

    def source(qi, wp):
        part = qi * n_chains + wp - lead
        return jnp.maximum(part, 0) // per_sub, (wp - lead) % per_sub * hk

    def qk(qi, item):
        wp, c = item
        blk, r0 = source(qi, wp)
        cols = slice(c * 2 * hk, (c + 1) * 2 * hk)
        d = wp - c
        kb = k_ref[0, pl.ds(pl.multiple_of(blk * ts + r0, hk), hk), :]
        s = _dot(kb, q12_ref[qi, :, cols]) + bias_ref[0, d * hk:(d + 1) * hk, :]
        if wp < lead:
            s = jnp.where(qi > 0, s, NEG)
        return s

    def query_block(qi, carry):
        scores = [r[...] for r in score_refs]
        state = [(None, None)] * n_chains
        for i, (wp, c) in enumerate(items):
            s = scores.pop(0)
            nxt = i + LOOKAHEAD
            if nxt < len(items):
                scores.append(qk(qi, items[nxt]))
            else:
                score_refs[nxt - len(items)][...] = qk(
                    jnp.minimum(qi + 1, n_blk - 1), items[nxt - len(items)])
            blk, r0 = source(qi, wp)
            state[c] = _online_softmax(s, state[c][0], state[c][1],
                                       vt_ref[0, blk, :, r0:r0 + hk])
        outs = []
        for _, acc in state:
            heads = [_normalized(acc[hh * B_V_ROWS:(hh + 1) * B_V_ROWS,
                                     hh * hk:(hh + 1) * hk], HEAD_DIM) for hh in range(2)]
            outs.append(jnp.concatenate(heads, axis=0))
        for h in range(B_QUERY_SUBS):
            ot = jnp.concatenate(outs[h * per_sub:(h + 1) * per_sub], axis=1)
            o_ref[0, pl.ds(pl.multiple_of((qi * B_QUERY_SUBS + h) * ts, ts), ts), :] = (
                ot.T.astype(BF16))
        return carry

    for i, r in enumerate(score_refs):
        r[...] = qk(0, items[i])
    lax.fori_loop(0, n_blk, query_block, 0)


def _attn_b(qt, k, vt, bias):
    b, n_sub, w, ts = qt.shape
    s = n_sub * ts
    return pl.pallas_call(
        _attn_b_kernel,
        grid=(w // LANES, b),
        in_specs=[pl.BlockSpec((1, n_sub, LANES, ts), lambda pi, bi: (bi, 0, pi, 0)),
                  pl.BlockSpec((1, s, LANES), lambda pi, bi: (bi, 0, pi)),
                  pl.BlockSpec((1, n_sub, 2 * B_V_ROWS, ts), lambda pi, bi: (bi, 0, pi, 0)),
                  pl.BlockSpec((1,) + bias.shape[1:], lambda pi, bi: (pi, 0, 0))],
        out_specs=pl.BlockSpec((1, s, LANES), lambda pi, bi: (bi, 0, pi)),
        out_shape=jax.ShapeDtypeStruct((b, s, w), BF16),
        scratch_shapes=([pltpu.VMEM((n_sub // B_QUERY_SUBS, LANES, 2 * B_QUERY_SUBS * ts), BF16)]
                        + [pltpu.VMEM((KEY_PART, 2 * KEY_PART), F32)] * LOOKAHEAD),
        compiler_params=_params(2),
        name="attn_b",
    )(qt, k, vt, bias)


def _attn_c_kernel(qt_ref, k_ref, vt_ref, o_ref):
    n_blk, _, tq = qt_ref.shape[1:]

    def query_block(qi, carry):
        def qk(hd):
            rows = slice(hd * C_DIM, (hd + 1) * C_DIM)
            return _dot(k_ref[0, :, rows], qt_ref[0, qi, rows, :])

        scores = [qk(hd) for hd in range(LOOKAHEAD)]
        outs = []
        for hd in range(C_HEADS):
            s = scores.pop(0)
            if hd + LOOKAHEAD < C_HEADS:
                scores.append(qk(hd + LOOKAHEAD))
            _, acc = _online_softmax(s, None, None,
                                     vt_ref[0, hd * A_V_ROWS:(hd + 1) * A_V_ROWS, :])
            outs.append(_normalized(acc, C_DIM))
        o_ref[0, pl.ds(pl.multiple_of(qi * tq, tq), tq), :] = (
            jnp.concatenate(outs, axis=0).T.astype(BF16))
        return carry

    lax.fori_loop(0, n_blk, query_block, 0)


def _attn_c(qt, mk, mvt):
    b, n_blk, w, tq = qt.shape
    n_mem = mk.shape[1]
    return pl.pallas_call(
        _attn_c_kernel,
        grid=(b,),
        in_specs=[pl.BlockSpec((1, n_blk, w, tq), lambda bi: (bi, 0, 0, 0)),
                  pl.BlockSpec((1, n_mem, w), lambda bi: (bi, 0, 0)),
                  pl.BlockSpec((1, mvt.shape[1], n_mem), lambda bi: (bi, 0, 0))],
        out_specs=pl.BlockSpec((1, n_blk * tq, w), lambda bi: (bi, 0, 0)),
        out_shape=jax.ShapeDtypeStruct((b, n_blk * tq, w), BF16),
        compiler_params=_params(1),
        name="attn_c",
    )(qt, mk, mvt)


def _mix_kernel(x_ref, g_ref, wgate_ref, bgate_ref, oa_ref, ob_ref, oc_ref, wbr_ref, wout_ref,
                o_ref):
    x = x_ref[...]
    d = x.shape[1]
    h = _rms(x, g_ref[...]).astype(BF16)
    y = None
    for n, br_ref in enumerate((oa_ref, ob_ref, oc_ref)):
        z = _dot(h, wgate_ref[:, n * d:(n + 1) * d]) + bgate_ref[:, n * d:(n + 1) * d]
        term = (1.0 / (1.0 + jnp.exp(-z))) * _dot(br_ref[...], wbr_ref[n])
        y = term if y is None else y + term
    o_ref[...] = x + _dot(y.astype(BF16), wout_ref[...])


def _mix_out(x, g, w_gate, b_gate, o_a, o_b, o_c, w_branch, w_out):
    t, d = x.shape
    tm = FFN_TILE
    tile = lambda i: (i, 0)
    br_spec = pl.BlockSpec((tm, BRANCH_W), tile)
    return pl.pallas_call(
        _mix_kernel,
        grid=(t // tm,),
        in_specs=[pl.BlockSpec((tm, d), tile), _const_spec((1, d)), _const_spec(w_gate.shape),
                  _const_spec(b_gate.shape), br_spec, br_spec, br_spec,
                  _const_spec(w_branch.shape), _const_spec(w_out.shape)],
        out_specs=pl.BlockSpec((tm, d), tile),
        out_shape=jax.ShapeDtypeStruct((t, d), F32),
        compiler_params=_params(1),
        name="mix_out",
    )(x, g, w_gate, b_gate, o_a, o_b, o_c, w_branch, w_out)


def _rope_tables(seq):
    half = HEAD_DIM // 2
    inv = ROPE_THETA ** (-jnp.arange(half, dtype=F32) / half)
    ang = jnp.arange(seq, dtype=F32)[:, None] * inv[None, :]
    return jnp.cos(ang).T, jnp.sin(ang).T


def _group_mean_matrix(group):
    idx = jnp.arange(MXU_TILE) // group
    return jnp.where(idx[:, None] == idx[None, :], 1.0 / group, 0.0).astype(BF16)


def _bias_rows(table):
    h = table.shape[0]
    far = jnp.broadcast_to(table[:, -1:], (h, TOKEN_TILE - REL_CLIP))
    near = table[:, ::-1]
    tail = jnp.broadcast_to(table[:, -1:], (h, 2 * TOKEN_TILE - far.shape[1] - near.shape[1]))
    return jnp.concatenate([far, near, tail], axis=1)[:, None, :]


def kernel(x, mem, ffn1_norm, ffn1_wg, ffn1_wu, ffn1_wd, mix_norm, w_in, a_q_norm, a_k_norm,
           a_lambda, a_sub_norm, b_q_norm, b_k_norm, b_rel_bias, mem_norm, w_mem_kv, c_q_norm,
           c_k_norm, w_gate, b_gate, w_branch, w_out, ffn2_norm, ffn2_wg, ffn2_wu, ffn2_wd,
           final_norm):
    b, s, d = x.shape
    n_mem = mem.shape[1]
    depth = ffn1_norm.shape[0]
    tm = TOKEN_TILE
    w = BRANCH_W
    assert s % tm == 0 and d % MXU_TILE == 0 and w_in.shape[2] == 7 * w

    tables = _rope_tables(s)
    g128 = _group_mean_matrix(C_DIM)
    row = lambda v: v.reshape(1, -1)
    heads = lambda v: jnp.tile(v, w // v.shape[0])

    xt = x.reshape(b * s, d)
    memt = mem.reshape(b * n_mem, d)
    for l in range(depth):
        lambda_init = 0.8 - 0.6 * math.exp(-0.3 * l)
        xt = _ffn(xt, row(ffn1_norm[l]), ffn1_wg[l].astype(BF16), ffn1_wu[l].astype(BF16),
                  ffn1_wd[l].astype(BF16))

        col = lambda v: heads(v).reshape(w, 1)
        aqt, avt, bqt, bvt, cqt, ak, bk = _in_proj(
            xt, row(mix_norm[l]), w_in[l].astype(BF16).T, tables,
            col(a_q_norm[l]), col(a_k_norm[l]), col(b_q_norm[l]), col(b_k_norm[l]),
            col(c_q_norm[l]), s)
        wm = w_mem_kv[l].astype(BF16)
        mk, mvt = _mem_proj(memt, row(mem_norm[l]), wm[:, :w], wm[:, w:].T,
                            row(heads(c_k_norm[l])), g128, n_mem)
        bias = _rel_bias(_bias_rows(b_rel_bias[l]))

        blocks = lambda a: a.reshape(b, s // tm, a.shape[1], tm)
        seq3 = lambda a: a.reshape(b, s, w)
        o_a = _attn_a(blocks(aqt), seq3(ak), blocks(avt), a_lambda[l], row(a_sub_norm[l]),
                      lambda_init)
        o_b = _attn_b(blocks(bqt), seq3(bk), blocks(bvt), bias)
        o_c = _attn_c(blocks(cqt), mk.reshape(b, n_mem, w), mvt)

        flat = lambda a: a.reshape(b * s, w)
        xt = _mix_out(xt, row(mix_norm[l]), w_gate[l].astype(BF16), row(b_gate[l]),
                      flat(o_a), flat(o_b), flat(o_c), w_branch[l].astype(BF16),
                      w_out[l].astype(BF16))

        xt = _ffn(xt, row(ffn2_norm[l]), ffn2_wg[l].astype(BF16), ffn2_wu[l].astype(BF16),
                  ffn2_wd[l].astype(BF16), row(final_norm[l]))
    return xt.reshape(b, s, d)
```

```python
import functools
import math

import jax
import jax.numpy as jnp
from jax import lax
from jax.experimental import pallas as pl
from jax.experimental.pallas import tpu as pltpu

F32 = jnp.float32
BF16 = jnp.bfloat16

CHUNK = 64
HEAD_DIM = 64
A_HEADS = 4
B_HEADS = 8
B_LEFT_CHUNKS = 8
REL_CLIP = 128
C_HEADS = 4
C_DIM = 128
N_BRANCH = 3
BRANCH_W = 512
ROPE_THETA = 10000.0
EPS = 1e-6
NEG = -1e30
LOG2E = math.log2(math.e)

LANES = 128
MXU_TILE = 256
BF16_ROWS = 16
TOKEN_TILE = 512
FFN_TILE = 1024
IN_TILES = 2
KEY_PART = MXU_TILE
LOOKAHEAD = 2
A_KEY_PART = MXU_TILE
A_LOOKAHEAD = 2
A_QUERY_SUBS = 4
B_QUERY_SUBS = 4
B_BAND_PARTS = 3
ONES_ROWS = BF16_ROWS
VMEM_LIMIT = 56 * 1024 * 1024

A_V_ROWS = 2 * HEAD_DIM + ONES_ROWS
B_V_ROWS = HEAD_DIM + ONES_ROWS

assert TOKEN_TILE == B_LEFT_CHUNKS * CHUNK


def _dot(a, b):
    return jnp.dot(a, b, preferred_element_type=F32)


def _dot_nt(a, b):
    return lax.dot_general(a, b, (((1,), (1,)), ((), ())), preferred_element_type=F32)


def _rms(x, g):
    return x * lax.rsqrt(jnp.mean(x * x, axis=-1, keepdims=True) + EPS) * g


def _group_rms(z, gmat_ref, gain):
    z2 = (z * z).astype(BF16)
    ms = jnp.concatenate(
        [_dot(z2[:, :MXU_TILE], gmat_ref[...]), _dot(z2[:, MXU_TILE:], gmat_ref[...])], axis=1)
    return z * lax.rsqrt(ms + EPS) * gain


def _heads_t(z, gain, group, scale, cos=None, sin=None):
    outs = []
    for g0 in range(0, z.shape[0], group):
        zz = z[g0:g0 + group]
        r = lax.rsqrt(jnp.mean(zz * zz, axis=0, keepdims=True) + EPS) * scale
        y = zz * r * gain[g0:g0 + group]
        if cos is not None:
            y1, y2 = y[:group // 2], y[group // 2:]
            y = jnp.concatenate([y1 * cos - y2 * sin, y2 * cos + y1 * sin], axis=0)
        outs.append(y)
    return jnp.concatenate(outs, axis=0)


def _with_ones(vt, head_rows):
    ones = jnp.ones((ONES_ROWS, vt.shape[1]), vt.dtype)
    parts = []
    for r0 in range(0, vt.shape[0], head_rows):
        parts += [vt[r0:r0 + head_rows], ones]
    return jnp.concatenate(parts, axis=0)


def _const_spec(shape):
    nd = len(shape)
    return pl.BlockSpec(shape, lambda *_: (0,) * nd, pipeline_mode=pl.Buffered(1))


def _params(n_axes):
    return pltpu.CompilerParams(
        dimension_semantics=("arbitrary",) * n_axes, vmem_limit_bytes=VMEM_LIMIT)


def _ffn_chunks(d_ff):
    step = 3 * MXU_TILE
    return [(c, min(c + step, d_ff)) for c in range(0, d_ff, step)]


def _ffn_kernel(x_ref, g_ref, wg_ref, wu_ref, wd_ref, *rest, final_norm):
    if final_norm:
        fg_ref, o_ref, act_ref = rest
    else:
        o_ref, act_ref = rest
    x = x_ref[...]
    h = _rms(x, g_ref[...]).astype(BF16)
    for c0, c1 in _ffn_chunks(wg_ref.shape[1]):
        g = _dot(h, wg_ref[:, c0:c1])
        u = _dot(h, wu_ref[:, c0:c1])
        act_ref[:, c0:c1] = (g * (1.0 / (1.0 + jnp.exp(-g))) * u).astype(BF16)
    out = x + 0.5 * _dot(act_ref[...], wd_ref[...])
    if final_norm:
        out = _rms(out, fg_ref[...])
    o_ref[...] = out


def _ffn(x, g, wg, wu, wd, final_g=None):
    t, d = x.shape
    f = wg.shape[1]
    tm = FFN_TILE
    in_specs = [pl.BlockSpec((tm, d), lambda i: (i, 0)), _const_spec((1, d)),
                _const_spec((d, f)), _const_spec((d, f)), _const_spec((f, d))]
    args = [x, g, wg, wu, wd]
    if final_g is not None:
        in_specs.append(_const_spec((1, d)))
        args.append(final_g)
    return pl.pallas_call(
        functools.partial(_ffn_kernel, final_norm=final_g is not None),
        grid=(t // tm,),
        in_specs=in_specs,
        out_specs=pl.BlockSpec((tm, d), lambda i: (i, 0)),
        out_shape=jax.ShapeDtypeStruct((t, d), F32),
        scratch_shapes=[pltpu.VMEM((tm, f), BF16)],
        compiler_params=_params(1),
        name="ffn_final" if final_g is not None else "ffn",
    )(*args)


def _in_kernel(x_ref, g_ref, wt_ref, cost_ref, sint_ref,
               gaq_ref, gak_ref, gbq_ref, gbk_ref, gcq_ref,
               aqt_ref, avt_ref, bqt_ref, bvt_ref, cqt_ref, ak_ref, bk_ref):
    w = BRANCH_W
    tm = TOKEN_TILE
    qk_scale = HEAD_DIM ** -0.5 * LOG2E
    for sub in range(IN_TILES):
        rows = slice(sub * tm, (sub + 1) * tm)
        h = _rms(x_ref[rows, :], g_ref[...]).astype(BF16)
        cos, sin = cost_ref[:, rows], sint_ref[:, rows]

        def seg_t(i):
            return _dot_nt(wt_ref[i * w:(i + 1) * w, :], h)

        aqt_ref[sub] = _heads_t(seg_t(0), gaq_ref[...], HEAD_DIM, qk_scale, cos, sin
                                ).astype(BF16)
        ak_ref[rows, :] = _heads_t(seg_t(1), gak_ref[...], HEAD_DIM, 1.0, cos, sin
                                   ).T.astype(BF16)
        avt_ref[sub] = _with_ones(seg_t(2).astype(BF16), 2 * HEAD_DIM)
        bqt_ref[sub] = _heads_t(seg_t(3), gbq_ref[...], HEAD_DIM, qk_scale).astype(BF16)
        bk_ref[rows, :] = _heads_t(seg_t(4), gbk_ref[...], HEAD_DIM, 1.0).T.astype(BF16)
        bvt_ref[sub] = _with_ones(seg_t(5).astype(BF16), HEAD_DIM)
        cqt_ref[sub] = _heads_t(seg_t(6), gcq_ref[...], C_DIM, C_DIM ** -0.5 * LOG2E
                                ).astype(BF16)


def _in_proj(x, g, w_t, tables, gaq, gak, gbq, gbk, gcq, seq):
    t, d = x.shape
    tm = TOKEN_TILE
    ts = IN_TILES * tm
    n_tiles = t // tm
    n_seq_steps = seq // ts
    w = BRANCH_W
    cos_t, sin_t = tables
    tile = lambda i: (i, 0)
    half = HEAD_DIM // 2
    in_specs = [pl.BlockSpec((ts, d), tile), _const_spec((1, d)), _const_spec(w_t.shape),
                pl.BlockSpec((half, ts), lambda i: (0, i % n_seq_steps)),
                pl.BlockSpec((half, ts), lambda i: (0, i % n_seq_steps))]
    in_specs += [_const_spec((w, 1))] * 5
    t_rows = [w, A_HEADS * A_V_ROWS, w, B_HEADS * B_V_ROWS, w]
    return pl.pallas_call(
        _in_kernel,
        grid=(t // ts,),
        in_specs=in_specs,
        out_specs=([pl.BlockSpec((IN_TILES, r, tm), lambda i: (i, 0, 0)) for r in t_rows]
                   + [pl.BlockSpec((ts, w), tile)] * 2),
        out_shape=([jax.ShapeDtypeStruct((n_tiles, r, tm), BF16) for r in t_rows]
                   + [jax.ShapeDtypeStruct((t, w), BF16)] * 2),
        compiler_params=_params(1),
        name="in_proj",
    )(x, g, w_t, cos_t, sin_t, gaq, gak, gbq, gbk, gcq)


def _mem_kernel(m_ref, g_ref, wk_ref, wvt_ref, gk_ref, g128_ref, mk_ref, mvt_ref):
    h = _rms(m_ref[...], g_ref[...]).astype(BF16)
    mk_ref[...] = _group_rms(_dot(h, wk_ref[...]), g128_ref, gk_ref[...]).astype(BF16)
    mvt_ref[0] = _with_ones(_dot_nt(wvt_ref[...], h).astype(BF16), C_DIM)


def _mem_proj(mem, g, w_k, w_vt, gk, g128, n_mem):
    t, d = mem.shape
    w = BRANCH_W
    v_rows = C_HEADS * A_V_ROWS
    tile = lambda i: (i, 0)
    return pl.pallas_call(
        _mem_kernel,
        grid=(t // n_mem,),
        in_specs=[pl.BlockSpec((n_mem, d), tile), _const_spec((1, d)), _const_spec(w_k.shape),
                  _const_spec(w_vt.shape), _const_spec((1, w)),
                  _const_spec((MXU_TILE, MXU_TILE))],
        out_specs=[pl.BlockSpec((n_mem, w), tile),
                   pl.BlockSpec((1, v_rows, n_mem), lambda i: (i, 0, 0))],
        out_shape=[jax.ShapeDtypeStruct((t, w), BF16),
                   jax.ShapeDtypeStruct((t // n_mem, v_rows, n_mem), BF16)],
        compiler_params=_params(1),
        name="mem_proj",
    )(mem, g, w_k, w_vt, gk, g128)


def _bias_kernel(r_ref, o_ref):
    nq = TOKEN_TILE
    nk = 2 * nq
    hk = KEY_PART
    qc = lax.broadcasted_iota(jnp.int32, (nq, nk), 0) // CHUNK + B_LEFT_CHUNKS
    kc = lax.broadcasted_iota(jnp.int32, (nq, nk), 1) // CHUNK
    valid = (qc >= kc) & (qc - kc <= B_LEFT_CHUNKS)
    for hh in range(2):
        t = pltpu.roll(jnp.broadcast_to(r_ref[hh], (nq, nk)), 0, 1, stride=1, stride_axis=0)
        t = jnp.where(valid, t * LOG2E, NEG).T
        o_ref[0, :, hh * hk:(hh + 1) * hk] = t[:B_BAND_PARTS * hk, :hk]


def _rel_bias(rows):
    h, _, nk = rows.shape
    tile = (B_BAND_PARTS * KEY_PART, 2 * KEY_PART)
    return pl.pallas_call(
        _bias_kernel,
        grid=(h // 2,),
        in_specs=[pl.BlockSpec((2, 1, nk), lambda i: (i, 0, 0))],
        out_specs=pl.BlockSpec((1,) + tile, lambda i: (i, 0, 0)),
        out_shape=jax.ShapeDtypeStruct((h // 2,) + tile, F32),
        compiler_params=_params(1),
        name="rel_bias",
    )(rows)


def _split_queries(qt):
    row = lax.broadcasted_iota(jnp.int32, qt.shape, 0)
    zero = jnp.zeros_like(qt)
    return jnp.concatenate(
        [jnp.where(row < HEAD_DIM, qt, zero), jnp.where(row >= HEAD_DIM, qt, zero)], axis=1)


def _column_max(s):
    return jnp.max(s, axis=0, keepdims=True)


def _online_softmax(s, m_prev, acc_prev, vt, s_max=None):
    if s_max is None:
        s_max = _column_max(s)
    m_new = s_max if m_prev is None else jnp.maximum(m_prev, s_max)
    pv = _dot(vt, jnp.exp2(s - m_new).astype(BF16))
    if acc_prev is None:
        return m_new, pv
    return m_new, jnp.exp2(m_prev - m_new) * acc_prev + pv


def _normalized(acc, rows):
    return acc[:rows] / acc[rows:rows + 1]


def _attn_a_kernel(qt_ref, k_ref, vt_ref, lam_ref, gain_ref, o_ref, q12_ref, *scratch,
                   lambda_init):
    n_sub, _, ts = qt_ref.shape[1:]
    subs = A_QUERY_SUBS
    n_blk = n_sub // subs
    n_chains = 2 * subs
    look = A_LOOKAHEAD
    score_refs = scratch[:look]
    smax_refs = scratch[look:2 * look]
    state_refs = [scratch[2 * look + 2 * c:2 * look + 2 * c + 2] for c in range(n_chains)]
    for blk in range(n_blk):
        q12_ref[blk] = jnp.concatenate(
            [_split_queries(qt_ref[0, blk * subs + h]) for h in range(subs)], axis=1)
    lp = lam_ref[...]
    lam = (jnp.exp(jnp.sum(lp[0:1] * lp[1:2], axis=1, keepdims=True))
           - jnp.exp(jnp.sum(lp[2:3] * lp[3:4], axis=1, keepdims=True)) + lambda_init)

    hk = A_KEY_PART
    parts = ts // hk

    def qk(item):
        qi, jb, kp, ch, _ = item
        kb = k_ref[0, pl.ds(pl.multiple_of(jb * ts + kp * hk, hk), hk), :]
        s = _dot(kb, q12_ref[qi, :, ch * ts:(ch + 1) * ts])
        return s, _column_max(s)

    def run(seq, following):
        scores = [(score_refs[i][...], smax_refs[i][...]) for i in range(look)]
        for i, (_, jb, kp, ch, masked) in enumerate(seq):
            s, s_max = scores.pop(0)
            if masked:
                key_chunk = (lax.broadcasted_iota(jnp.int32, (hk, ts), 0) + kp * hk) // CHUNK
                query_chunk = lax.broadcasted_iota(jnp.int32, (hk, ts), 1) // CHUNK
                s = jnp.where(key_chunk <= query_chunk, s, NEG)
                s_max = None
            nxt = i + look
            if nxt < len(seq):
                scores.append(qk(seq[nxt]))
            else:
                n = nxt - len(seq)
                score_refs[n][...], smax_refs[n][...] = qk(following[n])
            m_ref, acc_ref = state_refs[ch]
            m_ref[...], acc_ref[...] = _online_softmax(
                s, m_ref[...], acc_ref[...], vt_ref[0, jb, :, kp * hk:(kp + 1) * hk], s_max)

    def key_block(qi, jb):
        return [(qi, jb, kp, ch, False) for kp in range(parts) for ch in range(n_chains)]

    def own_blocks(qi):
        return [(qi, subs * qi + kb, kp, 2 * h + c, h == kb)
                for kb in range(subs) for kp in range(parts)
                for h in range(kb, subs) for c in range(2)]

    def reset():
        for m_ref, acc_ref in state_refs:
            m_ref[...] = jnp.full(m_ref.shape, NEG, F32)
            acc_ref[...] = jnp.zeros(acc_ref.shape, F32)

    def query_block(qi, carry):
        def body(jb, c):
            run(key_block(qi, jb), key_block(qi, jb + 1))
            return c

        lax.fori_loop(0, subs * qi, body, 0)
        nxt_qi = jnp.minimum(qi + 1, n_blk - 1)
        run(own_blocks(qi), key_block(nxt_qi, 0))
        for h in range(subs):
            o1, o2 = (_normalized(state_refs[2 * h + c][1][...], 2 * HEAD_DIM) for c in range(2))
            ot = o1 - lam * o2
            ot = ot * lax.rsqrt(jnp.mean(ot * ot, axis=0, keepdims=True) + EPS)
            o_ref[0, pl.ds(pl.multiple_of((qi * subs + h) * ts, ts), ts), :] = (
                ot.T * gain_ref[...] * (1.0 - lambda_init)).astype(BF16)
        reset()
        return carry

    reset()
    for i, item in enumerate(key_block(0, 0)[:look]):
        score_refs[i][...], smax_refs[i][...] = qk(item)
    lax.fori_loop(0, n_blk, query_block, 0)


def _attn_a(qt, k, vt, lam_p, gain, lambda_init):
    b, n_sub, w, ts = qt.shape
    s = n_sub * ts
    n_chains = 2 * A_QUERY_SUBS
    return pl.pallas_call(
        functools.partial(_attn_a_kernel, lambda_init=lambda_init),
        grid=(b, w // LANES),
        in_specs=[pl.BlockSpec((1, n_sub, LANES, ts), lambda bi, hi: (bi, 0, hi, 0)),
                  pl.BlockSpec((1, s, LANES), lambda bi, hi: (bi, 0, hi)),
                  pl.BlockSpec((1, n_sub, A_V_ROWS, ts), lambda bi, hi: (bi, 0, hi, 0)),
                  _const_spec(lam_p.shape), _const_spec((1, LANES))],
        out_specs=pl.BlockSpec((1, s, LANES), lambda bi, hi: (bi, 0, hi)),
        out_shape=jax.ShapeDtypeStruct((b, s, w), BF16),
        scratch_shapes=(
            [pltpu.VMEM((n_sub // A_QUERY_SUBS, LANES, n_chains * ts), BF16)]
            + [pltpu.VMEM((A_KEY_PART, ts), F32)] * A_LOOKAHEAD
            + [pltpu.VMEM((1, ts), F32)] * A_LOOKAHEAD
            + [pltpu.VMEM((1, ts), F32), pltpu.VMEM((A_V_ROWS, ts), F32)] * n_chains),
        compiler_params=_params(2),
        name="attn_a",
    )(qt, k, vt, lam_p, gain)


def _attn_b_kernel(qt_ref, k_ref, vt_ref, bias_ref, o_ref, q12_ref, *score_refs):
    n_sub, _, ts = qt_ref.shape[1:]
    hk = KEY_PART
    per_sub = ts // hk
    n_chains = B_QUERY_SUBS * per_sub
    n_blk = n_sub // B_QUERY_SUBS
    lead = B_BAND_PARTS - 1
    assert lead * hk == B_LEFT_CHUNKS * CHUNK and LOOKAHEAD <= n_chains
    for blk in range(n_blk):
        cols = []
        for c in range(n_chains):
            q12 = _split_queries(qt_ref[0, blk * B_QUERY_SUBS + c // per_sub])
            c0 = (c % per_sub) * hk
            cols += [q12[:, c0:c0 + hk], q12[:, ts + c0:ts + c0 + hk]]
        q12_ref[blk] = jnp.concatenate(cols, axis=1)

    items = [(c + d, c) for d in range(lead, -1, -1) for c in range(n_chains)]

    def source(qi, wp):
        part = qi * n_chains + wp - lead
        return jnp.maximum(part, 0) // per_sub, (wp - lead) % per_sub * hk

    def qk(qi, item):
        wp, c = item
        blk, r0 = source(qi, wp)
        cols = slice(c * 2 * hk, (c + 1) * 2 * hk)
        d = wp - c
        kb = k_ref[0, pl.ds(pl.multiple_of(blk * ts + r0, hk), hk), :]
        s = _dot(kb, q12_ref[qi, :, cols]) + bias_ref[0, d * hk:(d + 1) * hk, :]
        if wp < lead:
            s = jnp.where(qi > 0, s, NEG)
        return s

    def query_block(qi, carry):
        scores = [r[...] for r in score_refs]
        state = [(None, None)] * n_chains
        for i, (wp, c) in enumerate(items):
            s = scores.pop(0)
            nxt = i + LOOKAHEAD
            if nxt < len(items):
                scores.append(qk(qi, items[nxt]))
            else:
                score_refs[nxt - len(items)][...] = qk(
                    jnp.minimum(qi + 1, n_blk - 1), items[nxt - len(items)])
            blk, r0 = source(qi, wp)
            state[c] = _online_softmax(s, state[c][0], state[c][1],
                                       vt_ref[0, blk, :, r0:r0 + hk])
        outs = []
        for _, acc in state:
            heads = [_normalized(acc[hh * B_V_ROWS:(hh + 1) * B_V_ROWS,
                                     hh * hk:(hh + 1) * hk], HEAD_DIM) for hh in range(2)]
            outs.append(jnp.concatenate(heads, axis=0))
        for h in range(B_QUERY_SUBS):
            ot = jnp.concatenate(outs[h * per_sub:(h + 1) * per_sub], axis=1)
            o_ref[0, pl.ds(pl.multiple_of((qi * B_QUERY_SUBS + h) * ts, ts), ts), :] = (
                ot.T.astype(BF16))
        return carry

    for i, r in enumerate(score_refs):
        r[...] = qk(0, items[i])
    lax.fori_loop(0, n_blk, query_block, 0)


def _attn_b(qt, k, vt, bias):
    b, n_sub, w, ts = qt.shape
    s = n_sub * ts
    return pl.pallas_call(
        _attn_b_kernel,
        grid=(w // LANES, b),
        in_specs=[pl.BlockSpec((1, n_sub, LANES, ts), lambda pi, bi: (bi, 0, pi, 0)),
                  pl.BlockSpec((1, s, LANES), lambda pi, bi: (bi, 0, pi)),
                  pl.BlockSpec((1, n_sub, 2 * B_V_ROWS, ts), lambda pi, bi: (bi, 0, pi, 0)),
                  pl.BlockSpec((1,) + bias.shape[1:], lambda pi, bi: (pi, 0, 0))],
        out_specs=pl.BlockSpec((1, s, LANES), lambda pi, bi: (bi, 0, pi)),
        out_shape=jax.ShapeDtypeStruct((b, s, w), BF16),
        scratch_shapes=([pltpu.VMEM((n_sub // B_QUERY_SUBS, LANES, 2 * B_QUERY_SUBS * ts), BF16)]
                        + [pltpu.VMEM((KEY_PART, 2 * KEY_PART), F32)] * LOOKAHEAD),
        compiler_params=_params(2),
        name="attn_b",
    )(qt, k, vt, bias)


def _attn_c_kernel(qt_ref, k_ref, vt_ref, o_ref):
    n_blk, _, tq = qt_ref.shape[1:]
    items = [(qi, hd) for qi in range(n_blk) for hd in range(C_HEADS)]

    def qk(item):
        qi, hd = item
        rows = slice(hd * C_DIM, (hd + 1) * C_DIM)
        return _dot(k_ref[0, :, rows], qt_ref[0, qi, rows, :])

    scores = [qk(it) for it in items[:LOOKAHEAD]]
    outs = []
    for i, (qi, hd) in enumerate(items):
        s = scores.pop(0)
        if i + LOOKAHEAD < len(items):
            scores.append(qk(items[i + LOOKAHEAD]))
        _, acc = _online_softmax(s, None, None, vt_ref[0, hd * A_V_ROWS:(hd + 1) * A_V_ROWS, :])
        outs.append(_normalized(acc, C_DIM))
        if hd == C_HEADS - 1:
            o_ref[0, qi * tq:(qi + 1) * tq, :] = jnp.concatenate(outs, axis=0).T.astype(BF16)
            outs = []


def _attn_c(qt, mk, mvt):
    b, n_blk, w, tq = qt.shape
    n_mem = mk.shape[1]
    return pl.pallas_call(
        _attn_c_kernel,
        grid=(b,),
        in_specs=[pl.BlockSpec((1, n_blk, w, tq), lambda bi: (bi, 0, 0, 0)),
                  pl.BlockSpec((1, n_mem, w), lambda bi: (bi, 0, 0)),
                  pl.BlockSpec((1, mvt.shape[1], n_mem), lambda bi: (bi, 0, 0))],
        out_specs=pl.BlockSpec((1, n_blk * tq, w), lambda bi: (bi, 0, 0)),
        out_shape=jax.ShapeDtypeStruct((b, n_blk * tq, w), BF16),
        compiler_params=_params(1),
        name="attn_c",
    )(qt, mk, mvt)


def _mix_kernel(x_ref, g_ref, wgate_ref, bgate_ref, oa_ref, ob_ref, oc_ref, wbr_ref, wout_ref,
                o_ref):
    x = x_ref[...]
    d = x.shape[1]
    h = _rms(x, g_ref[...]).astype(BF16)
    y = None
    for n, br_ref in enumerate((oa_ref, ob_ref, oc_ref)):
        z = _dot(h, wgate_ref[:, n * d:(n + 1) * d]) + bgate_ref[:, n * d:(n + 1) * d]
        term = (1.0 / (1.0 + jnp.exp(-z))) * _dot(br_ref[...], wbr_ref[n])
        y = term if y is None else y + term
    o_ref[...] = x + _dot(y.astype(BF16), wout_ref[...])


def _mix_out(x, g, w_gate, b_gate, o_a, o_b, o_c, w_branch, w_out):
    t, d = x.shape
    tm = FFN_TILE
    tile = lambda i: (i, 0)
    br_spec = pl.BlockSpec((tm, BRANCH_W), tile)
    return pl.pallas_call(
        _mix_kernel,
        grid=(t // tm,),
        in_specs=[pl.BlockSpec((tm, d), tile), _const_spec((1, d)), _const_spec(w_gate.shape),
                  _const_spec(b_gate.shape), br_spec, br_spec, br_spec,
                  _const_spec(w_branch.shape), _const_spec(w_out.shape)],
        out_specs=pl.BlockSpec((tm, d), tile),
        out_shape=jax.ShapeDtypeStruct((t, d), F32),
        compiler_params=_params(1),
        name="mix_out",
    )(x, g, w_gate, b_gate, o_a, o_b, o_c, w_branch, w_out)


def _rope_tables(seq):
    half = HEAD_DIM // 2
    inv = ROPE_THETA ** (-jnp.arange(half, dtype=F32) / half)
    ang = jnp.arange(seq, dtype=F32)[:, None] * inv[None, :]
    return jnp.cos(ang).T, jnp.sin(ang).T


def _group_mean_matrix(group):
    idx = jnp.arange(MXU_TILE) // group
    return jnp.where(idx[:, None] == idx[None, :], 1.0 / group, 0.0).astype(BF16)


def _bias_rows(table):
    h = table.shape[0]
    far = jnp.broadcast_to(table[:, -1:], (h, TOKEN_TILE - REL_CLIP))
    near = table[:, ::-1]
    tail = jnp.broadcast_to(table[:, -1:], (h, 2 * TOKEN_TILE - far.shape[1] - near.shape[1]))
    return jnp.concatenate([far, near, tail], axis=1)[:, None, :]


def kernel(x, mem, ffn1_norm, ffn1_wg, ffn1_wu, ffn1_wd, mix_norm, w_in, a_q_norm, a_k_norm,
           a_lambda, a_sub_norm, b_q_norm, b_k_norm, b_rel_bias, mem_norm, w_mem_kv, c_q_norm,
           c_k_norm, w_gate, b_gate, w_branch, w_out, ffn2_norm, ffn2_wg, ffn2_wu, ffn2_wd,
           final_norm):
    b, s, d = x.shape
    n_mem = mem.shape[1]
    depth = ffn1_norm.shape[0]
    tm = TOKEN_TILE
    w = BRANCH_W
    assert s % tm == 0 and d % MXU_TILE == 0 and w_in.shape[2] == 7 * w

    tables = _rope_tables(s)
    g128 = _group_mean_matrix(C_DIM)
    row = lambda v: v.reshape(1, -1)
    heads = lambda v: jnp.tile(v, w // v.shape[0])

    xt = x.reshape(b * s, d)
    memt = mem.reshape(b * n_mem, d)
    for l in range(depth):
        lambda_init = 0.8 - 0.6 * math.exp(-0.3 * l)
        xt = _ffn(xt, row(ffn1_norm[l]), ffn1_wg[l].astype(BF16), ffn1_wu[l].astype(BF16),
                  ffn1_wd[l].astype(BF16))

        col = lambda v: heads(v).reshape(w, 1)
        aqt, avt, bqt, bvt, cqt, ak, bk = _in_proj(
            xt, row(mix_norm[l]), w_in[l].astype(BF16).T, tables,
            col(a_q_norm[l]), col(a_k_norm[l]), col(b_q_norm[l]), col(b_k_norm[l]),
            col(c_q_norm[l]), s)
        wm = w_mem_kv[l].astype(BF16)
        mk, mvt = _mem_proj(memt, row(mem_norm[l]), wm[:, :w], wm[:, w:].T,
                            row(heads(c_k_norm[l])), g128, n_mem)
        bias = _rel_bias(_bias_rows(b_rel_bias[l]))

        blocks = lambda a: a.reshape(b, s // tm, a.shape[1], tm)
        seq3 = lambda a: a.reshape(b, s, w)
        o_a = _attn_a(blocks(aqt), seq3(ak), blocks(avt), a_lambda[l], row(a_sub_norm[l]),
                      lambda_init)
        o_b = _attn_b(blocks(bqt), seq3(bk), blocks(bvt), bias)
        o_c = _attn_c(blocks(cqt), mk.reshape(b, n_mem, w), mvt)

        flat = lambda a: a.reshape(b * s, w)
        xt = _mix_out(xt, row(mix_norm[l]), w_gate[l].astype(BF16), row(b_gate[l]),
                      flat(o_a), flat(o_b), flat(o_c), w_branch[l].astype(BF16),
                      w_out[l].astype(BF16))

        xt = _ffn(xt, row(ffn2_norm[l]), ffn2_wg[l].astype(BF16), ffn2_wu[l].astype(BF16),
                  ffn2_wd[l].astype(BF16), row(final_norm[l]))
    return xt.reshape(b, s, d)
```

```python
import functools
import math

import jax
import jax.numpy as jnp
from jax import lax
from jax.experimental import pallas as pl
from jax.experimental.pallas import tpu as pltpu

F32 = jnp.float32
BF16 = jnp.bfloat16

CHUNK = 64
HEAD_DIM = 64
A_HEADS = 4
B_HEADS = 8
B_LEFT_CHUNKS = 8
REL_CLIP = 128
C_HEADS = 4
C_DIM = 128
N_BRANCH = 3
BRANCH_W = 512
ROPE_THETA = 10000.0
EPS = 1e-6
NEG = -1e30
LOG2E = math.log2(math.e)

LANES = 128
MXU_TILE = 256
BF16_ROWS = 16
TOKEN_TILE = 512
FFN_TILE = 1024
IN_TILES = 2
CAST_COLS = 512
KEY_PART = MXU_TILE
LOOKAHEAD = 2
A_KEY_PART = MXU_TILE
A_LOOKAHEAD = 2
A_QUERY_SUBS = 4
B_QUERY_SUBS = 4
B_BAND_PARTS = 3
ONES_ROWS = BF16_ROWS
VMEM_LIMIT = 56 * 1024 * 1024

A_V_ROWS = 2 * HEAD_DIM + ONES_ROWS
B_V_ROWS = HEAD_DIM + ONES_ROWS

assert TOKEN_TILE == B_LEFT_CHUNKS * CHUNK


def _dot(a, b):
    return jnp.dot(a, b, preferred_element_type=F32)


def _dot_nt(a, b):
    return lax.dot_general(a, b, (((1,), (1,)), ((), ())), preferred_element_type=F32)


def _rms(x, g):
    return x * lax.rsqrt(jnp.mean(x * x, axis=-1, keepdims=True) + EPS) * g


def _group_rms(z, gmat_ref, gain):
    z2 = (z * z).astype(BF16)
    ms = jnp.concatenate(
        [_dot(z2[:, :MXU_TILE], gmat_ref[...]), _dot(z2[:, MXU_TILE:], gmat_ref[...])], axis=1)
    return z * lax.rsqrt(ms + EPS) * gain


def _heads_t(z, gain, group, scale, cos=None, sin=None):
    outs = []
    for g0 in range(0, z.shape[0], group):
        zz = z[g0:g0 + group]
        r = lax.rsqrt(jnp.mean(zz * zz, axis=0, keepdims=True) + EPS) * scale
        y = zz * r * gain[g0:g0 + group]
        if cos is not None:
            y1, y2 = y[:group // 2], y[group // 2:]
            y = jnp.concatenate([y1 * cos - y2 * sin, y2 * cos + y1 * sin], axis=0)
        outs.append(y)
    return jnp.concatenate(outs, axis=0)


def _with_ones(vt, head_rows):
    ones = jnp.ones((ONES_ROWS, vt.shape[1]), vt.dtype)
    parts = []
    for r0 in range(0, vt.shape[0], head_rows):
        parts += [vt[r0:r0 + head_rows], ones]
    return jnp.concatenate(parts, axis=0)


def _const_spec(shape):
    nd = len(shape)
    return pl.BlockSpec(shape, lambda *_: (0,) * nd, pipeline_mode=pl.Buffered(1))


def _params(n_axes):
    return pltpu.CompilerParams(
        dimension_semantics=("arbitrary",) * n_axes, vmem_limit_bytes=VMEM_LIMIT)


def _ffn_chunks(d_ff):
    step = 3 * MXU_TILE
    return [(c, min(c + step, d_ff)) for c in range(0, d_ff, step)]


def _ffn_kernel(x_ref, g_ref, wg_ref, wu_ref, wd_ref, *rest, final_norm, n_cast):
    rest = list(rest)
    fg_ref = rest.pop(0) if final_norm else None
    cast_in, o_ref, cast_out, act_ref = (rest[:n_cast], rest[n_cast],
                                         rest[n_cast + 1:2 * n_cast + 1], rest[-1])
    for src, dst in zip(cast_in, cast_out):
        dst[...] = src[...].astype(BF16)
    x = x_ref[...]
    h = _rms(x, g_ref[...]).astype(BF16)
    for c0, c1 in _ffn_chunks(wg_ref.shape[1]):
        g = _dot(h, wg_ref[:, c0:c1])
        u = _dot(h, wu_ref[:, c0:c1])
        act_ref[:, c0:c1] = (g * (1.0 / (1.0 + jnp.exp(-g))) * u).astype(BF16)
    out = x + 0.5 * _dot(act_ref[...], wd_ref[...])
    if final_norm:
        out = _rms(out, fg_ref[...])
    o_ref[...] = out


def _ffn(x, g, wg, wu, wd, final_g=None, to_bf16=()):
    t, d = x.shape
    f = wg.shape[1]
    tm = FFN_TILE
    steps = t // tm
    tile = lambda i: (i, 0)
    in_specs = [pl.BlockSpec((tm, d), tile), _const_spec((1, d)),
                _const_spec((d, f)), _const_spec((d, f)), _const_spec((f, d))]
    args = [x, g, wg, wu, wd]
    if final_g is not None:
        in_specs.append(_const_spec((1, d)))
        args.append(final_g)
    flat = [a.reshape(-1, CAST_COLS) for a in to_bf16]
    cast_specs = []
    for a in flat:
        rows = a.shape[0] // steps
        assert rows * steps == a.shape[0] and rows % BF16_ROWS == 0
        cast_specs.append(pl.BlockSpec((rows, CAST_COLS), tile))
    outs = pl.pallas_call(
        functools.partial(_ffn_kernel, final_norm=final_g is not None, n_cast=len(flat)),
        grid=(steps,),
        in_specs=in_specs + cast_specs,
        out_specs=[pl.BlockSpec((tm, d), tile)] + cast_specs,
        out_shape=([jax.ShapeDtypeStruct((t, d), F32)]
                   + [jax.ShapeDtypeStruct(a.shape, BF16) for a in flat]),
        scratch_shapes=[pltpu.VMEM((tm, f), BF16)],
        compiler_params=_params(1),
        name="ffn_final" if final_g is not None else "ffn",
    )(*args, *flat)
    return outs[0], [o.reshape(a.shape) for o, a in zip(outs[1:], to_bf16)]


def _in_kernel(x_ref, g_ref, wt_ref, cost_ref, sint_ref,
               gaq_ref, gak_ref, gbq_ref, gbk_ref, gcq_ref,
               aqt_ref, avt_ref, bqt_ref, bvt_ref, cqt_ref, ak_ref, bk_ref):
    w = BRANCH_W
    tm = TOKEN_TILE
    qk_scale = HEAD_DIM ** -0.5 * LOG2E
    for sub in range(IN_TILES):
        rows = slice(sub * tm, (sub + 1) * tm)
        h = _rms(x_ref[rows, :], g_ref[...]).astype(BF16)
        cos, sin = cost_ref[:, rows], sint_ref[:, rows]

        def seg_t(i):
            return _dot_nt(wt_ref[i * w:(i + 1) * w, :], h)

        aqt_ref[sub] = _heads_t(seg_t(0), gaq_ref[...], HEAD_DIM, qk_scale, cos, sin
                                ).astype(BF16)
        ak_ref[rows, :] = _heads_t(seg_t(1), gak_ref[...], HEAD_DIM, 1.0, cos, sin
                                   ).T.astype(BF16)
        avt_ref[sub] = _with_ones(seg_t(2).astype(BF16), 2 * HEAD_DIM)
        bqt_ref[sub] = _heads_t(seg_t(3), gbq_ref[...], HEAD_DIM, qk_scale).astype(BF16)
        bk_ref[rows, :] = _heads_t(seg_t(4), gbk_ref[...], HEAD_DIM, 1.0).T.astype(BF16)
        bvt_ref[sub] = _with_ones(seg_t(5).astype(BF16), HEAD_DIM)
        cqt_ref[sub] = _heads_t(seg_t(6), gcq_ref[...], C_DIM, C_DIM ** -0.5 * LOG2E
                                ).astype(BF16)


def _in_proj(x, g, w_t, tables, gaq, gak, gbq, gbk, gcq, seq):
    t, d = x.shape
    tm = TOKEN_TILE
    ts = IN_TILES * tm
    n_tiles = t // tm
    n_seq_steps = seq // ts
    w = BRANCH_W
    cos_t, sin_t = tables
    tile = lambda i: (i, 0)
    half = HEAD_DIM // 2
    in_specs = [pl.BlockSpec((ts, d), tile), _const_spec((1, d)), _const_spec(w_t.shape),
                pl.BlockSpec((half, ts), lambda i: (0, i % n_seq_steps)),
                pl.BlockSpec((half, ts), lambda i: (0, i % n_seq_steps))]
    in_specs += [_const_spec((w, 1))] * 5
    t_rows = [w, A_HEADS * A_V_ROWS, w, B_HEADS * B_V_ROWS, w]
    return pl.pallas_call(
        _in_kernel,
        grid=(t // ts,),
        in_specs=in_specs,
        out_specs=([pl.BlockSpec((IN_TILES, r, tm), lambda i: (i, 0, 0)) for r in t_rows]
                   + [pl.BlockSpec((ts, w), tile)] * 2),
        out_shape=([jax.ShapeDtypeStruct((n_tiles, r, tm), BF16) for r in t_rows]
                   + [jax.ShapeDtypeStruct((t, w), BF16)] * 2),
        compiler_params=_params(1),
        name="in_proj",
    )(x, g, w_t, cos_t, sin_t, gaq, gak, gbq, gbk, gcq)


def _mem_kernel(m_ref, g_ref, wk_ref, wvt_ref, gk_ref, g128_ref, mk_ref, mvt_ref):
    h = _rms(m_ref[...], g_ref[...]).astype(BF16)
    mk_ref[...] = _group_rms(_dot(h, wk_ref[...]), g128_ref, gk_ref[...]).astype(BF16)
    mvt_ref[0] = _with_ones(_dot_nt(wvt_ref[...], h).astype(BF16), C_DIM)


def _mem_proj(mem, g, w_k, w_vt, gk, g128, n_mem):
    t, d = mem.shape
    w = BRANCH_W
    v_rows = C_HEADS * A_V_ROWS
    tile = lambda i: (i, 0)
    return pl.pallas_call(
        _mem_kernel,
        grid=(t // n_mem,),
        in_specs=[pl.BlockSpec((n_mem, d), tile), _const_spec((1, d)), _const_spec(w_k.shape),
                  _const_spec(w_vt.shape), _const_spec((1, w)),
                  _const_spec((MXU_TILE, MXU_TILE))],
        out_specs=[pl.BlockSpec((n_mem, w), tile),
                   pl.BlockSpec((1, v_rows, n_mem), lambda i: (i, 0, 0))],
        out_shape=[jax.ShapeDtypeStruct((t, w), BF16),
                   jax.ShapeDtypeStruct((t // n_mem, v_rows, n_mem), BF16)],
        compiler_params=_params(1),
        name="mem_proj",
    )(mem, g, w_k, w_vt, gk, g128)


def _bias_kernel(r_ref, o_ref):
    nq = TOKEN_TILE
    nk = 2 * nq
    hk = KEY_PART
    qc = lax.broadcasted_iota(jnp.int32, (nq, nk), 0) // CHUNK + B_LEFT_CHUNKS
    kc = lax.broadcasted_iota(jnp.int32, (nq, nk), 1) // CHUNK
    valid = (qc >= kc) & (qc - kc <= B_LEFT_CHUNKS)
    for hh in range(2):
        t = pltpu.roll(jnp.broadcast_to(r_ref[hh], (nq, nk)), 0, 1, stride=1, stride_axis=0)
        t = jnp.where(valid, t * LOG2E, NEG).T
        o_ref[0, :, hh * hk:(hh + 1) * hk] = t[:B_BAND_PARTS * hk, :hk]


def _rel_bias(rows):
    h, _, nk = rows.shape
    tile = (B_BAND_PARTS * KEY_PART, 2 * KEY_PART)
    return pl.pallas_call(
        _bias_kernel,
        grid=(h // 2,),
        in_specs=[pl.BlockSpec((2, 1, nk), lambda i: (i, 0, 0))],
        out_specs=pl.BlockSpec((1,) + tile, lambda i: (i, 0, 0)),
        out_shape=jax.ShapeDtypeStruct((h // 2,) + tile, F32),
        compiler_params=_params(1),
        name="rel_bias",
    )(rows)


def _split_queries(qt):
    row = lax.broadcasted_iota(jnp.int32, qt.shape, 0)
    zero = jnp.zeros_like(qt)
    return jnp.concatenate(
        [jnp.where(row < HEAD_DIM, qt, zero), jnp.where(row >= HEAD_DIM, qt, zero)], axis=1)


def _rows(start, size):
    if isinstance(start, int):
        return pl.ds(start, size)
    return pl.ds(pl.multiple_of(start, size), size)


def _column_max(s):
    return jnp.max(s, axis=0, keepdims=True)


def _online_softmax(s, m_prev, acc_prev, vt, s_max=None):
    if s_max is None:
        s_max = _column_max(s)
    m_new = s_max if m_prev is None else jnp.maximum(m_prev, s_max)
    pv = _dot(vt, jnp.exp2(s - m_new).astype(BF16))
    if acc_prev is None:
        return m_new, pv
    return m_new, jnp.exp2(m_prev - m_new) * acc_prev + pv


def _normalized(acc, rows):
    return acc[:rows] / acc[rows:rows + 1]


def _attn_a_kernel(qt_ref, k_ref, vt_ref, lam_ref, gain_ref, o_ref, q12_ref, *scratch,
                   lambda_init):
    n_sub, _, ts = qt_ref.shape[1:]
    subs = A_QUERY_SUBS
    n_blk = n_sub // subs
    n_chains = 2 * subs
    look = A_LOOKAHEAD
    score_refs = scratch[:look]
    smax_refs = scratch[look:2 * look]
    state_refs = [scratch[2 * look + 2 * c:2 * look + 2 * c + 2] for c in range(n_chains)]
    for blk in range(n_blk):
        q12_ref[blk] = jnp.concatenate(
            [_split_queries(qt_ref[0, blk * subs + h]) for h in range(subs)], axis=1)
    lp = lam_ref[...]
    lam = (jnp.exp(jnp.sum(lp[0:1] * lp[1:2], axis=1, keepdims=True))
           - jnp.exp(jnp.sum(lp[2:3] * lp[3:4], axis=1, keepdims=True)) + lambda_init)

    hk = A_KEY_PART
    parts = ts // hk

    def qk(item):
        qi, jb, kp, ch, _ = item
        kb = k_ref[0, _rows(jb * ts + kp * hk, hk), :]
        s = _dot(kb, q12_ref[qi, :, ch * ts:(ch + 1) * ts])
        return s, _column_max(s)

    def reset(h):
        for m_ref, acc_ref in state_refs[2 * h:2 * h + 2]:
            m_ref[...] = jnp.full(m_ref.shape, NEG, F32)
            acc_ref[...] = jnp.zeros(acc_ref.shape, F32)

    def finish(qi, h):
        o1, o2 = (_normalized(state_refs[2 * h + c][1][...], 2 * HEAD_DIM) for c in range(2))
        ot = o1 - lam * o2
        ot = ot * lax.rsqrt(jnp.mean(ot * ot, axis=0, keepdims=True) + EPS)
        o_ref[0, _rows((qi * subs + h) * ts, ts), :] = (
            ot.T * gain_ref[...] * (1.0 - lambda_init)).astype(BF16)
        reset(h)

    def run(seq, following, after=()):
        scores = [(score_refs[i][...], smax_refs[i][...]) for i in range(look)]
        for i, (qi, jb, kp, ch, masked) in enumerate(seq):
            s, s_max = scores.pop(0)
            if masked:
                key_chunk = (lax.broadcasted_iota(jnp.int32, (hk, ts), 0) + kp * hk) // CHUNK
                query_chunk = lax.broadcasted_iota(jnp.int32, (hk, ts), 1) // CHUNK
                s = jnp.where(key_chunk <= query_chunk, s, NEG)
                s_max = None
            nxt = i + look
            if nxt < len(seq):
                scores.append(qk(seq[nxt]))
            else:
                n = nxt - len(seq)
                score_refs[n][...], smax_refs[n][...] = qk(following[n])
            m_ref, acc_ref = state_refs[ch]
            m_ref[...], acc_ref[...] = _online_softmax(
                s, m_ref[...], acc_ref[...], vt_ref[0, jb, :, kp * hk:(kp + 1) * hk], s_max)
            if i in after:
                finish(qi, after[i])

    def key_block(qi, jb):
        return [(qi, jb, kp, ch, False) for kp in range(parts) for ch in range(n_chains)]

    def own_blocks(qi):
        seq, after = [], {}
        for kb in range(subs):
            seq += [(qi, subs * qi + kb, kp, 2 * h + c, h == kb)
                    for kp in range(parts) for h in range(kb, subs) for c in range(2)]
            after[len(seq) - 2 * (subs - kb) + 1] = kb
        return seq, after

    def query_block(qi, carry):
        def body(jb, c):
            run(key_block(qi, jb), key_block(qi, jb + 1))
            return c

        lax.fori_loop(0, subs * qi, body, 0)
        seq, after = own_blocks(qi)
        nxt_qi = jnp.minimum(qi + 1, n_blk - 1)
        run(seq, key_block(nxt_qi, 0), after)
        return carry

    for h in range(subs):
        reset(h)
    for i, item in enumerate(key_block(0, 0)[:look]):
        score_refs[i][...], smax_refs[i][...] = qk(item)
    if n_blk == 1:
        query_block(0, 0)
    else:
        lax.fori_loop(0, n_blk, query_block, 0)


def _attn_a(qt, k, vt, lam_p, gain, lambda_init):
    b, n_sub, w, ts = qt.shape
    s = n_sub * ts
    n_chains = 2 * A_QUERY_SUBS
    return pl.pallas_call(
        functools.partial(_attn_a_kernel, lambda_init=lambda_init),
        grid=(b, w // LANES),
        in_specs=[pl.BlockSpec((1, n_sub, LANES, ts), lambda bi, hi: (bi, 0, hi, 0)),
                  pl.BlockSpec((1, s, LANES), lambda bi, hi: (bi, 0, hi)),
                  pl.BlockSpec((1, n_sub, A_V_ROWS, ts), lambda bi, hi: (bi, 0, hi, 0)),
                  _const_spec(lam_p.shape), _const_spec((1, LANES))],
        out_specs=pl.BlockSpec((1, s, LANES), lambda bi, hi: (bi, 0, hi)),
        out_shape=jax.ShapeDtypeStruct((b, s, w), BF16),
        scratch_shapes=(
            [pltpu.VMEM((n_sub // A_QUERY_SUBS, LANES, n_chains * ts), BF16)]
            + [pltpu.VMEM((A_KEY_PART, ts), F32)] * A_LOOKAHEAD
            + [pltpu.VMEM((1, ts), F32)] * A_LOOKAHEAD
            + [pltpu.VMEM((1, ts), F32), pltpu.VMEM((A_V_ROWS, ts), F32)] * n_chains),
        compiler_params=_params(2),
        name="attn_a",
    )(qt, k, vt, lam_p, gain)


def _attn_b_kernel(qt_ref, k_ref, vt_ref, bias_ref, o_ref, q12_ref, *score_refs):
    n_sub, _, ts = qt_ref.shape[1:]
    hk = KEY_PART
    per_sub = ts // hk
    n_chains = B_QUERY_SUBS * per_sub
    n_blk = n_sub // B_QUERY_SUBS
    lead = B_BAND_PARTS - 1
    assert lead * hk == B_LEFT_CHUNKS * CHUNK and LOOKAHEAD <= n_chains
    for blk in range(n_blk):
        cols = []
        for c in range(n_chains):
            q12 = _split_queries(qt_ref[0, blk * B_QUERY_SUBS + c // per_sub])
            c0 = (c % per_sub) * hk
            cols += [q12[:, c0:c0 + hk], q12[:, ts + c0:ts + c0 + hk]]
        q12_ref[blk] = jnp.concatenate(cols, axis=1)

    items = [(c + d, c) for d in range(lead, -1, -1) for c in range(n_chains)]

    def source(qi, wp):
        part = qi * n_chains + wp - lead
        return jnp.maximum(part, 0) // per_sub, (wp - lead) % per_sub * hk

    def qk(qi, item):
        wp, c = item
        blk, r0 = source(qi, wp)
        cols = slice(c * 2 * hk, (c + 1) * 2 * hk)
        d = wp - c
        kb = k_ref[0, pl.ds(pl.multiple_of(blk * ts + r0, hk), hk), :]
        s = _dot(kb, q12_ref[qi, :, cols]) + bias_ref[0, d * hk:(d + 1) * hk, :]
        if wp < lead:
            s = jnp.where(qi > 0, s, NEG)
        return s

    def query_block(qi, carry):
        scores = [r[...] for r in score_refs]
        state = [(None, None)] * n_chains
        for i, (wp, c) in enumerate(items):
            s = scores.pop(0)
            nxt = i + LOOKAHEAD
            if nxt < len(items):
                scores.append(qk(qi, items[nxt]))
            else:
                score_refs[nxt - len(items)][...] = qk(
                    jnp.minimum(qi + 1, n_blk - 1), items[nxt - len(items)])
            blk, r0 = source(qi, wp)
            state[c] = _online_softmax(s, state[c][0], state[c][1],
                                       vt_ref[0, blk, :, r0:r0 + hk])
        outs = []
        for _, acc in state:
            heads = [_normalized(acc[hh * B_V_ROWS:(hh + 1) * B_V_ROWS,
                                     hh * hk:(hh + 1) * hk], HEAD_DIM) for hh in range(2)]
            outs.append(jnp.concatenate(heads, axis=0))
        for h in range(B_QUERY_SUBS):
            ot = jnp.concatenate(outs[h * per_sub:(h + 1) * per_sub], axis=1)
            o_ref[0, pl.ds(pl.multiple_of((qi * B_QUERY_SUBS + h) * ts, ts), ts), :] = (
                ot.T.astype(BF16))
        return carry

    for i, r in enumerate(score_refs):
        r[...] = qk(0, items[i])
    lax.fori_loop(0, n_blk, query_block, 0)


def _attn_b(qt, k, vt, bias):
    b, n_sub, w, ts = qt.shape
    s = n_sub * ts
    return pl.pallas_call(
        _attn_b_kernel,
        grid=(w // LANES, b),
        in_specs=[pl.BlockSpec((1, n_sub, LANES, ts), lambda pi, bi: (bi, 0, pi, 0)),
                  pl.BlockSpec((1, s, LANES), lambda pi, bi: (bi, 0, pi)),
                  pl.BlockSpec((1, n_sub, 2 * B_V_ROWS, ts), lambda pi, bi: (bi, 0, pi, 0)),
                  pl.BlockSpec((1,) + bias.shape[1:], lambda pi, bi: (pi, 0, 0))],
        out_specs=pl.BlockSpec((1, s, LANES), lambda pi, bi: (bi, 0, pi)),
        out_shape=jax.ShapeDtypeStruct((b, s, w), BF16),
        scratch_shapes=([pltpu.VMEM((n_sub // B_QUERY_SUBS, LANES, 2 * B_QUERY_SUBS * ts), BF16)]
                        + [pltpu.VMEM((KEY_PART, 2 * KEY_PART), F32)] * LOOKAHEAD),
        compiler_params=_params(2),
        name="attn_b",
    )(qt, k, vt, bias)


def _attn_c_kernel(qt_ref, k_ref, vt_ref, o_ref):
    n_blk, _, tq = qt_ref.shape[1:]
    items = [(qi, hd) for qi in range(n_blk) for hd in range(C_HEADS)]

    def qk(item):
        qi, hd = item
        rows = slice(hd * C_DIM, (hd + 1) * C_DIM)
        return _dot(k_ref[0, :, rows], qt_ref[0, qi, rows, :])

    scores = [qk(it) for it in items[:LOOKAHEAD]]
    outs = []
    for i, (qi, hd) in enumerate(items):
        s = scores.pop(0)
        if i + LOOKAHEAD < len(items):
            scores.append(qk(items[i + LOOKAHEAD]))
        _, acc = _online_softmax(s, None, None, vt_ref[0, hd * A_V_ROWS:(hd + 1) * A_V_ROWS, :])
        outs.append(_normalized(acc, C_DIM))
        if hd == C_HEADS - 1:
            o_ref[0, qi * tq:(qi + 1) * tq, :] = jnp.concatenate(outs, axis=0).T.astype(BF16)
            outs = []


def _attn_c(qt, mk, mvt):
    b, n_blk, w, tq = qt.shape
    n_mem = mk.shape[1]
    return pl.pallas_call(
        _attn_c_kernel,
        grid=(b,),
        in_specs=[pl.BlockSpec((1, n_blk, w, tq), lambda bi: (bi, 0, 0, 0)),
                  pl.BlockSpec((1, n_mem, w), lambda bi: (bi, 0, 0)),
                  pl.BlockSpec((1, mvt.shape[1], n_mem), lambda bi: (bi, 0, 0))],
        out_specs=pl.BlockSpec((1, n_blk * tq, w), lambda bi: (bi, 0, 0)),
        out_shape=jax.ShapeDtypeStruct((b, n_blk * tq, w), BF16),
        compiler_params=_params(1),
        name="attn_c",
    )(qt, mk, mvt)


def _mix_kernel(x_ref, g_ref, wgate_ref, bgate_ref, oa_ref, ob_ref, oc_ref, wbr_ref, wout_ref,
                o_ref):
    x = x_ref[...]
    d = x.shape[1]
    h = _rms(x, g_ref[...]).astype(BF16)
    y = None
    for n, br_ref in enumerate((oa_ref, ob_ref, oc_ref)):
        z = _dot(h, wgate_ref[:, n * d:(n + 1) * d]) + bgate_ref[:, n * d:(n + 1) * d]
        term = (1.0 / (1.0 + jnp.exp(-z))) * _dot(br_ref[...], wbr_ref[n])
        y = term if y is None else y + term
    o_ref[...] = x + _dot(y.astype(BF16), wout_ref[...])


def _mix_out(x, g, w_gate, b_gate, o_a, o_b, o_c, w_branch, w_out):
    t, d = x.shape
    tm = FFN_TILE
    tile = lambda i: (i, 0)
    br_spec = pl.BlockSpec((tm, BRANCH_W), tile)
    return pl.pallas_call(
        _mix_kernel,
        grid=(t // tm,),
        in_specs=[pl.BlockSpec((tm, d), tile), _const_spec((1, d)), _const_spec(w_gate.shape),
                  _const_spec(b_gate.shape), br_spec, br_spec, br_spec,
                  _const_spec(w_branch.shape), _const_spec(w_out.shape)],
        out_specs=pl.BlockSpec((tm, d), tile),
        out_shape=jax.ShapeDtypeStruct((t, d), F32),
        compiler_params=_params(1),
        name="mix_out",
    )(x, g, w_gate, b_gate, o_a, o_b, o_c, w_branch, w_out)


def _rope_tables(seq):
    half = HEAD_DIM // 2
    inv = ROPE_THETA ** (-jnp.arange(half, dtype=F32) / half)
    ang = jnp.arange(seq, dtype=F32)[:, None] * inv[None, :]
    return jnp.cos(ang).T, jnp.sin(ang).T


def _group_mean_matrix(group):
    idx = jnp.arange(MXU_TILE) // group
    return jnp.where(idx[:, None] == idx[None, :], 1.0 / group, 0.0).astype(BF16)


def _bias_rows(table):
    h = table.shape[0]
    far = jnp.broadcast_to(table[:, -1:], (h, TOKEN_TILE - REL_CLIP))
    near = table[:, ::-1]
    tail = jnp.broadcast_to(table[:, -1:], (h, 2 * TOKEN_TILE - far.shape[1] - near.shape[1]))
    return jnp.concatenate([far, near, tail], axis=1)[:, None, :]


def kernel(x, mem, ffn1_norm, ffn1_wg, ffn1_wu, ffn1_wd, mix_norm, w_in, a_q_norm, a_k_norm,
           a_lambda, a_sub_norm, b_q_norm, b_k_norm, b_rel_bias, mem_norm, w_mem_kv, c_q_norm,
           c_k_norm, w_gate, b_gate, w_branch, w_out, ffn2_norm, ffn2_wg, ffn2_wu, ffn2_wd,
           final_norm):
    b, s, d = x.shape
    n_mem = mem.shape[1]
    depth = ffn1_norm.shape[0]
    tm = TOKEN_TILE
    w = BRANCH_W
    assert s % tm == 0 and d % MXU_TILE == 0 and w_in.shape[2] == 7 * w

    tables = _rope_tables(s)
    g128 = _group_mean_matrix(C_DIM)
    row = lambda v: v.reshape(1, -1)
    heads = lambda v: jnp.tile(v, w // v.shape[0])

    xt = x.reshape(b * s, d)
    memt = mem.reshape(b * n_mem, d)
    for l in range(depth):
        lambda_init = 0.8 - 0.6 * math.exp(-0.3 * l)
        xt, (wg2, wu2, wd2, w_in_b, w_gate_b, w_branch_b, w_out_b, wm) = _ffn(
            xt, row(ffn1_norm[l]), ffn1_wg[l].astype(BF16), ffn1_wu[l].astype(BF16),
            ffn1_wd[l].astype(BF16),
            to_bf16=(ffn2_wg[l], ffn2_wu[l], ffn2_wd[l], w_in[l], w_gate[l], w_branch[l],
                     w_out[l], w_mem_kv[l]))

        col = lambda v: heads(v).reshape(w, 1)
        aqt, avt, bqt, bvt, cqt, ak, bk = _in_proj(
            xt, row(mix_norm[l]), w_in_b.T, tables,
            col(a_q_norm[l]), col(a_k_norm[l]), col(b_q_norm[l]), col(b_k_norm[l]),
            col(c_q_norm[l]), s)
        mk, mvt = _mem_proj(memt, row(mem_norm[l]), wm[:, :w], wm[:, w:].T,
                            row(heads(c_k_norm[l])), g128, n_mem)
        bias = _rel_bias(_bias_rows(b_rel_bias[l]))

        blocks = lambda a: a.reshape(b, s // tm, a.shape[1], tm)
        seq3 = lambda a: a.reshape(b, s, w)
        o_a = _attn_a(blocks(aqt), seq3(ak), blocks(avt), a_lambda[l], row(a_sub_norm[l]),
                      lambda_init)
        o_b = _attn_b(blocks(bqt), seq3(bk), blocks(bvt), bias)
        o_c = _attn_c(blocks(cqt), mk.reshape(b, n_mem, w), mvt)

        flat = lambda a: a.reshape(b * s, w)
        xt = _mix_out(xt, row(mix_norm[l]), w_gate_b, row(b_gate[l]),
                      flat(o_a), flat(o_b), flat(o_c), w_branch_b, w_out_b)

        xt, _ = _ffn(xt, row(ffn2_norm[l]), wg2, wu2, wd2, row(final_norm[l]))
    return xt.reshape(b, s, d)
```

```python
import functools
import math

import jax
import jax.numpy as jnp
from jax import lax
from jax.experimental import pallas as pl
from jax.experimental.pallas import tpu as pltpu

F32 = jnp.float32
BF16 = jnp.bfloat16

CHUNK = 64
HEAD_DIM = 64
A_HEADS = 4
B_HEADS = 8
B_LEFT_CHUNKS = 8
REL_CLIP = 128
C_HEADS = 4
C_DIM = 128
N_BRANCH = 3
BRANCH_W = 512
ROPE_THETA = 10000.0
EPS = 1e-6
NEG = -1e30
LOG2E = math.log2(math.e)

LANES = 128
MXU_TILE = 256
BF16_ROWS = 16
TOKEN_TILE = 512
FFN_TILE = 1024
IN_TILES = 2
KEY_PART = MXU_TILE
LOOKAHEAD = 2
A_KEY_PART = MXU_TILE
A_LOOKAHEAD = 2
A_QUERY_SUBS = 4
B_QUERY_SUBS = 4
B_BAND_PARTS = 3
ONES_ROWS = BF16_ROWS
VMEM_LIMIT = 56 * 1024 * 1024

A_V_ROWS = 2 * HEAD_DIM + ONES_ROWS
B_V_ROWS = HEAD_DIM + ONES_ROWS

assert TOKEN_TILE == B_LEFT_CHUNKS * CHUNK


def _dot(a, b):
    return jnp.dot(a, b, preferred_element_type=F32)


def _dot_nt(a, b):
    return lax.dot_general(a, b, (((1,), (1,)), ((), ())), preferred_element_type=F32)


def _rms(x, g):
    return x * lax.rsqrt(jnp.mean(x * x, axis=-1, keepdims=True) + EPS) * g


def _group_rms(z, gmat_ref, gain):
    z2 = (z * z).astype(BF16)
    ms = jnp.concatenate(
        [_dot(z2[:, :MXU_TILE], gmat_ref[...]), _dot(z2[:, MXU_TILE:], gmat_ref[...])], axis=1)
    return z * lax.rsqrt(ms + EPS) * gain


def _heads_t(z, gain, group, scale, cos=None, sin=None):
    outs = []
    for g0 in range(0, z.shape[0], group):
        zz = z[g0:g0 + group]
        r = lax.rsqrt(jnp.mean(zz * zz, axis=0, keepdims=True) + EPS) * scale
        y = zz * r * gain[g0:g0 + group]
        if cos is not None:
            y1, y2 = y[:group // 2], y[group // 2:]
            y = jnp.concatenate([y1 * cos - y2 * sin, y2 * cos + y1 * sin], axis=0)
        outs.append(y)
    return jnp.concatenate(outs, axis=0)


def _with_ones(vt, head_rows):
    ones = jnp.ones((ONES_ROWS, vt.shape[1]), vt.dtype)
    parts = []
    for r0 in range(0, vt.shape[0], head_rows):
        parts += [vt[r0:r0 + head_rows], ones]
    return jnp.concatenate(parts, axis=0)


def _const_spec(shape):
    nd = len(shape)
    return pl.BlockSpec(shape, lambda *_: (0,) * nd, pipeline_mode=pl.Buffered(1))


def _params(n_axes):
    return pltpu.CompilerParams(
        dimension_semantics=("arbitrary",) * n_axes, vmem_limit_bytes=VMEM_LIMIT)


def _ffn_chunks(d_ff):
    step = 3 * MXU_TILE
    return [(c, min(c + step, d_ff)) for c in range(0, d_ff, step)]


def _ffn_kernel(x_ref, g_ref, wg_ref, wu_ref, wd_ref, *rest, final_norm, n_cast):
    rest = list(rest)
    fg_ref = rest.pop(0) if final_norm else None
    cast_in, o_ref, cast_out, act_ref = (rest[:n_cast], rest[n_cast],
                                         rest[n_cast + 1:2 * n_cast + 1], rest[-1])
    for src, dst in zip(cast_in, cast_out):
        dst[...] = src[...].astype(BF16)
    x = x_ref[...]
    h = _rms(x, g_ref[...]).astype(BF16)
    for c0, c1 in _ffn_chunks(wg_ref.shape[1]):
        g = _dot(h, wg_ref[:, c0:c1])
        u = _dot(h, wu_ref[:, c0:c1])
        act_ref[:, c0:c1] = (g * (1.0 / (1.0 + jnp.exp(-g))) * u).astype(BF16)
    out = x + 0.5 * _dot(act_ref[...], wd_ref[...])
    if final_norm:
        out = _rms(out, fg_ref[...])
    o_ref[...] = out


def _ffn(x, g, wg, wu, wd, final_g=None, to_bf16=()):
    t, d = x.shape
    f = wg.shape[1]
    tm = FFN_TILE
    steps = t // tm
    tile = lambda i: (i, 0)
    in_specs = [pl.BlockSpec((tm, d), tile), _const_spec((1, d)),
                _const_spec((d, f)), _const_spec((d, f)), _const_spec((f, d))]
    args = [x, g, wg, wu, wd]
    if final_g is not None:
        in_specs.append(_const_spec((1, d)))
        args.append(final_g)
    cast_specs = []
    for a in to_bf16:
        n_blocks = math.gcd(steps, a.shape[0] // BF16_ROWS)
        rows, repeat = a.shape[0] // n_blocks, steps // n_blocks
        cast_specs.append(pl.BlockSpec((rows, a.shape[1]), lambda i, r=repeat: (i // r, 0)))
    outs = pl.pallas_call(
        functools.partial(_ffn_kernel, final_norm=final_g is not None, n_cast=len(to_bf16)),
        grid=(steps,),
        in_specs=in_specs + cast_specs,
        out_specs=[pl.BlockSpec((tm, d), tile)] + cast_specs,
        out_shape=([jax.ShapeDtypeStruct((t, d), F32)]
                   + [jax.ShapeDtypeStruct(a.shape, BF16) for a in to_bf16]),
        scratch_shapes=[pltpu.VMEM((tm, f), BF16)],
        compiler_params=_params(1),
        name="ffn_final" if final_g is not None else "ffn",
    )(*args, *to_bf16)
    return outs[0], outs[1:]


def _in_kernel(x_ref, g_ref, wt_ref, cost_ref, sint_ref,
               gaq_ref, gak_ref, gbq_ref, gbk_ref, gcq_ref,
               aqt_ref, avt_ref, bqt_ref, bvt_ref, cqt_ref, ak_ref, bk_ref):
    w = BRANCH_W
    tm = TOKEN_TILE
    qk_scale = HEAD_DIM ** -0.5 * LOG2E
    for sub in range(IN_TILES):
        rows = slice(sub * tm, (sub + 1) * tm)
        h = _rms(x_ref[rows, :], g_ref[...]).astype(BF16)
        cos, sin = cost_ref[:, rows], sint_ref[:, rows]

        def seg_t(i):
            return _dot_nt(wt_ref[i * w:(i + 1) * w, :], h)

        aqt_ref[sub] = _heads_t(seg_t(0), gaq_ref[...], HEAD_DIM, qk_scale, cos, sin
                                ).astype(BF16)
        ak_ref[rows, :] = _heads_t(seg_t(1), gak_ref[...], HEAD_DIM, 1.0, cos, sin
                                   ).T.astype(BF16)
        avt_ref[sub] = _with_ones(seg_t(2).astype(BF16), 2 * HEAD_DIM)
        bqt_ref[sub] = _heads_t(seg_t(3), gbq_ref[...], HEAD_DIM, qk_scale).astype(BF16)
        bk_ref[rows, :] = _heads_t(seg_t(4), gbk_ref[...], HEAD_DIM, 1.0).T.astype(BF16)
        bvt_ref[sub] = _with_ones(seg_t(5).astype(BF16), HEAD_DIM)
        cqt_ref[sub] = _heads_t(seg_t(6), gcq_ref[...], C_DIM, C_DIM ** -0.5 * LOG2E
                                ).astype(BF16)


def _in_proj(x, g, w_t, tables, gaq, gak, gbq, gbk, gcq, seq):
    t, d = x.shape
    tm = TOKEN_TILE
    ts = IN_TILES * tm
    n_tiles = t // tm
    n_seq_steps = seq // ts
    w = BRANCH_W
    cos_t, sin_t = tables
    tile = lambda i: (i, 0)
    half = HEAD_DIM // 2
    in_specs = [pl.BlockSpec((ts, d), tile), _const_spec((1, d)), _const_spec(w_t.shape),
                pl.BlockSpec((half, ts), lambda i: (0, i % n_seq_steps)),
                pl.BlockSpec((half, ts), lambda i: (0, i % n_seq_steps))]
    in_specs += [_const_spec((w, 1))] * 5
    t_rows = [w, A_HEADS * A_V_ROWS, w, B_HEADS * B_V_ROWS, w]
    return pl.pallas_call(
        _in_kernel,
        grid=(t // ts,),
        in_specs=in_specs,
        out_specs=([pl.BlockSpec((IN_TILES, r, tm), lambda i: (i, 0, 0)) for r in t_rows]
                   + [pl.BlockSpec((ts, w), tile)] * 2),
        out_shape=([jax.ShapeDtypeStruct((n_tiles, r, tm), BF16) for r in t_rows]
                   + [jax.ShapeDtypeStruct((t, w), BF16)] * 2),
        compiler_params=_params(1),
        name="in_proj",
    )(x, g, w_t, cos_t, sin_t, gaq, gak, gbq, gbk, gcq)


def _mem_kernel(m_ref, g_ref, wk_ref, wvt_ref, gk_ref, g128_ref, mk_ref, mvt_ref):
    h = _rms(m_ref[...], g_ref[...]).astype(BF16)
    mk_ref[...] = _group_rms(_dot(h, wk_ref[...]), g128_ref, gk_ref[...]).astype(BF16)
    mvt_ref[0] = _with_ones(_dot_nt(wvt_ref[...], h).astype(BF16), C_DIM)


def _mem_proj(mem, g, w_k, w_vt, gk, g128, n_mem):
    t, d = mem.shape
    w = BRANCH_W
    v_rows = C_HEADS * A_V_ROWS
    tile = lambda i: (i, 0)
    return pl.pallas_call(
        _mem_kernel,
        grid=(t // n_mem,),
        in_specs=[pl.BlockSpec((n_mem, d), tile), _const_spec((1, d)), _const_spec(w_k.shape),
                  _const_spec(w_vt.shape), _const_spec((1, w)),
                  _const_spec((MXU_TILE, MXU_TILE))],
        out_specs=[pl.BlockSpec((n_mem, w), tile),
                   pl.BlockSpec((1, v_rows, n_mem), lambda i: (i, 0, 0))],
        out_shape=[jax.ShapeDtypeStruct((t, w), BF16),
                   jax.ShapeDtypeStruct((t // n_mem, v_rows, n_mem), BF16)],
        compiler_params=_params(1),
        name="mem_proj",
    )(mem, g, w_k, w_vt, gk, g128)


def _bias_kernel(r_ref, o_ref):
    nq = TOKEN_TILE
    nk = 2 * nq
    hk = KEY_PART
    qc = lax.broadcasted_iota(jnp.int32, (nq, nk), 0) // CHUNK + B_LEFT_CHUNKS
    kc = lax.broadcasted_iota(jnp.int32, (nq, nk), 1) // CHUNK
    valid = (qc >= kc) & (qc - kc <= B_LEFT_CHUNKS)
    for hh in range(2):
        t = pltpu.roll(jnp.broadcast_to(r_ref[hh], (nq, nk)), 0, 1, stride=1, stride_axis=0)
        t = jnp.where(valid, t * LOG2E, NEG).T
        o_ref[0, :, hh * hk:(hh + 1) * hk] = t[:B_BAND_PARTS * hk, :hk]


def _rel_bias(rows):
    h, _, nk = rows.shape
    tile = (B_BAND_PARTS * KEY_PART, 2 * KEY_PART)
    return pl.pallas_call(
        _bias_kernel,
        grid=(h // 2,),
        in_specs=[pl.BlockSpec((2, 1, nk), lambda i: (i, 0, 0))],
        out_specs=pl.BlockSpec((1,) + tile, lambda i: (i, 0, 0)),
        out_shape=jax.ShapeDtypeStruct((h // 2,) + tile, F32),
        compiler_params=_params(1),
        name="rel_bias",
    )(rows)


def _split_queries(qt):
    row = lax.broadcasted_iota(jnp.int32, qt.shape, 0)
    zero = jnp.zeros_like(qt)
    return jnp.concatenate(
        [jnp.where(row < HEAD_DIM, qt, zero), jnp.where(row >= HEAD_DIM, qt, zero)], axis=1)


def _rows(start, size):
    if isinstance(start, int):
        return pl.ds(start, size)
    return pl.ds(pl.multiple_of(start, size), size)


def _column_max(s):
    return jnp.max(s, axis=0, keepdims=True)


def _online_softmax(s, m_prev, acc_prev, vt, s_max=None):
    if s_max is None:
        s_max = _column_max(s)
    m_new = s_max if m_prev is None else jnp.maximum(m_prev, s_max)
    pv = _dot(vt, jnp.exp2(s - m_new).astype(BF16))
    if acc_prev is None:
        return m_new, pv
    return m_new, jnp.exp2(m_prev - m_new) * acc_prev + pv


def _normalized(acc, rows):
    return acc[:rows] / acc[rows:rows + 1]


def _attn_a_kernel(qt_ref, k_ref, vt_ref, lam_ref, gain_ref, o_ref, q12_ref, *scratch,
                   lambda_init):
    n_sub, _, ts = qt_ref.shape[1:]
    subs = A_QUERY_SUBS
    n_blk = n_sub // subs
    n_chains = 2 * subs
    look = A_LOOKAHEAD
    score_refs = scratch[:look]
    smax_refs = scratch[look:2 * look]
    state_refs = [scratch[2 * look + 2 * c:2 * look + 2 * c + 2] for c in range(n_chains)]
    for blk in range(n_blk):
        q12_ref[blk] = jnp.concatenate(
            [_split_queries(qt_ref[0, blk * subs + h]) for h in range(subs)], axis=1)
    lp = lam_ref[...]
    lam = (jnp.exp(jnp.sum(lp[0:1] * lp[1:2], axis=1, keepdims=True))
           - jnp.exp(jnp.sum(lp[2:3] * lp[3:4], axis=1, keepdims=True)) + lambda_init)

    hk = A_KEY_PART
    parts = ts // hk

    def qk(item):
        qi, jb, kp, ch, _ = item
        kb = k_ref[0, _rows(jb * ts + kp * hk, hk), :]
        s = _dot(kb, q12_ref[qi, :, ch * ts:(ch + 1) * ts])
        return s, _column_max(s)

    def reset(h):
        for m_ref, acc_ref in state_refs[2 * h:2 * h + 2]:
            m_ref[...] = jnp.full(m_ref.shape, NEG, F32)
            acc_ref[...] = jnp.zeros(acc_ref.shape, F32)

    def finish(qi, h):
        o1, o2 = (_normalized(state_refs[2 * h + c][1][...], 2 * HEAD_DIM) for c in range(2))
        ot = o1 - lam * o2
        ot = ot * lax.rsqrt(jnp.mean(ot * ot, axis=0, keepdims=True) + EPS)
        o_ref[0, _rows((qi * subs + h) * ts, ts), :] = (
            ot.T * gain_ref[...] * (1.0 - lambda_init)).astype(BF16)
        reset(h)

    def run(seq, following, after=()):
        scores = [(score_refs[i][...], smax_refs[i][...]) for i in range(look)]
        for i, (qi, jb, kp, ch, masked) in enumerate(seq):
            s, s_max = scores.pop(0)
            if masked:
                key_chunk = (lax.broadcasted_iota(jnp.int32, (hk, ts), 0) + kp * hk) // CHUNK
                query_chunk = lax.broadcasted_iota(jnp.int32, (hk, ts), 1) // CHUNK
                s = jnp.where(key_chunk <= query_chunk, s, NEG)
                s_max = None
            nxt = i + look
            if nxt < len(seq):
                scores.append(qk(seq[nxt]))
            else:
                n = nxt - len(seq)
                score_refs[n][...], smax_refs[n][...] = qk(following[n])
            m_ref, acc_ref = state_refs[ch]
            m_ref[...], acc_ref[...] = _online_softmax(
                s, m_ref[...], acc_ref[...], vt_ref[0, jb, :, kp * hk:(kp + 1) * hk], s_max)
            if i in after:
                finish(qi, after[i])

    def key_block(qi, jb):
        return [(qi, jb, kp, ch, False) for kp in range(parts) for ch in range(n_chains)]

    def own_blocks(qi):
        seq, after = [], {}
        for kb in range(subs):
            seq += [(qi, subs * qi + kb, kp, 2 * h + c, h == kb)
                    for kp in range(parts) for h in range(kb, subs) for c in range(2)]
            after[len(seq) - 2 * (subs - kb) + 1] = kb
        return seq, after

    def query_block(qi, carry):
        def body(jb, c):
            run(key_block(qi, jb), key_block(qi, jb + 1))
            return c

        lax.fori_loop(0, subs * qi, body, 0)
        seq, after = own_blocks(qi)
        nxt_qi = jnp.minimum(qi + 1, n_blk - 1)
        run(seq, key_block(nxt_qi, 0), after)
        return carry

    for h in range(subs):
        reset(h)
    for i, item in enumerate(key_block(0, 0)[:look]):
        score_refs[i][...], smax_refs[i][...] = qk(item)
    if n_blk == 1:
        query_block(0, 0)
    else:
        lax.fori_loop(0, n_blk, query_block, 0)


def _attn_a(qt, k, vt, lam_p, gain, lambda_init):
    b, n_sub, w, ts = qt.shape
    s = n_sub * ts
    n_chains = 2 * A_QUERY_SUBS
    return pl.pallas_call(
        functools.partial(_attn_a_kernel, lambda_init=lambda_init),
        grid=(b, w // LANES),
        in_specs=[pl.BlockSpec((1, n_sub, LANES, ts), lambda bi, hi: (bi, 0, hi, 0)),
                  pl.BlockSpec((1, s, LANES), lambda bi, hi: (bi, 0, hi)),
                  pl.BlockSpec((1, n_sub, A_V_ROWS, ts), lambda bi, hi: (bi, 0, hi, 0)),
                  _const_spec(lam_p.shape), _const_spec((1, LANES))],
        out_specs=pl.BlockSpec((1, s, LANES), lambda bi, hi: (bi, 0, hi)),
        out_shape=jax.ShapeDtypeStruct((b, s, w), BF16),
        scratch_shapes=(
            [pltpu.VMEM((n_sub // A_QUERY_SUBS, LANES, n_chains * ts), BF16)]
            + [pltpu.VMEM((A_KEY_PART, ts), F32)] * A_LOOKAHEAD
            + [pltpu.VMEM((1, ts), F32)] * A_LOOKAHEAD
            + [pltpu.VMEM((1, ts), F32), pltpu.VMEM((A_V_ROWS, ts), F32)] * n_chains),
        compiler_params=_params(2),
        name="attn_a",
    )(qt, k, vt, lam_p, gain)


def _attn_b_kernel(qt_ref, k_ref, vt_ref, bias_ref, o_ref, q12_ref, *score_refs):
    n_sub, _, ts = qt_ref.shape[1:]
    hk = KEY_PART
    per_sub = ts // hk
    n_chains = B_QUERY_SUBS * per_sub
    n_blk = n_sub // B_QUERY_SUBS
    lead = B_BAND_PARTS - 1
    assert lead * hk == B_LEFT_CHUNKS * CHUNK and LOOKAHEAD <= n_chains
    for blk in range(n_blk):
        cols = []
        for c in range(n_chains):
            q12 = _split_queries(qt_ref[0, blk * B_QUERY_SUBS + c // per_sub])
            c0 = (c % per_sub) * hk
            cols += [q12[:, c0:c0 + hk], q12[:, ts + c0:ts + c0 + hk]]
        q12_ref[blk] = jnp.concatenate(cols, axis=1)

    items = [(c + d, c) for d in range(lead, -1, -1) for c in range(n_chains)]

    def source(qi, wp):
        part = qi * n_chains + wp - lead
        return jnp.maximum(part, 0) // per_sub, (wp - lead) % per_sub * hk

    def qk(qi, item):
        wp, c = item
        blk, r0 = source(qi, wp)
        cols = slice(c * 2 * hk, (c + 1) * 2 * hk)
        d = wp - c
        kb = k_ref[0, pl.ds(pl.multiple_of(blk * ts + r0, hk), hk), :]
        s = _dot(kb, q12_ref[qi, :, cols]) + bias_ref[0, d * hk:(d + 1) * hk, :]
        if wp < lead:
            s = jnp.where(qi > 0, s, NEG)
        return s

    def query_block(qi, carry):
        scores = [r[...] for r in score_refs]
        state = [(None, None)] * n_chains
        for i, (wp, c) in enumerate(items):
            s = scores.pop(0)
            nxt = i + LOOKAHEAD
            if nxt < len(items):
                scores.append(qk(qi, items[nxt]))
            else:
                score_refs[nxt - len(items)][...] = qk(
                    jnp.minimum(qi + 1, n_blk - 1), items[nxt - len(items)])
            blk, r0 = source(qi, wp)
            state[c] = _online_softmax(s, state[c][0], state[c][1],
                                       vt_ref[0, blk, :, r0:r0 + hk])
        outs = []
        for _, acc in state:
            heads = [_normalized(acc[hh * B_V_ROWS:(hh + 1) * B_V_ROWS,
                                     hh * hk:(hh + 1) * hk], HEAD_DIM) for hh in range(2)]
            outs.append(jnp.concatenate(heads, axis=0))
        for h in range(B_QUERY_SUBS):
            ot = jnp.concatenate(outs[h * per_sub:(h + 1) * per_sub], axis=1)
            o_ref[0, pl.ds(pl.multiple_of((qi * B_QUERY_SUBS + h) * ts, ts), ts), :] = (
                ot.T.astype(BF16))
        return carry

    for i, r in enumerate(score_refs):
        r[...] = qk(0, items[i])
    lax.fori_loop(0, n_blk, query_block, 0)


def _attn_b(qt, k, vt, bias):
    b, n_sub, w, ts = qt.shape
    s = n_sub * ts
    return pl.pallas_call(
        _attn_b_kernel,
        grid=(w // LANES, b),
        in_specs=[pl.BlockSpec((1, n_sub, LANES, ts), lambda pi, bi: (bi, 0, pi, 0)),
                  pl.BlockSpec((1, s, LANES), lambda pi, bi: (bi, 0, pi)),
                  pl.BlockSpec((1, n_sub, 2 * B_V_ROWS, ts), lambda pi, bi: (bi, 0, pi, 0)),
                  pl.BlockSpec((1,) + bias.shape[1:], lambda pi, bi: (pi, 0, 0))],
        out_specs=pl.BlockSpec((1, s, LANES), lambda pi, bi: (bi, 0, pi)),
        out_shape=jax.ShapeDtypeStruct((b, s, w), BF16),
        scratch_shapes=([pltpu.VMEM((n_sub // B_QUERY_SUBS, LANES, 2 * B_QUERY_SUBS * ts), BF16)]
                        + [pltpu.VMEM((KEY_PART, 2 * KEY_PART), F32)] * LOOKAHEAD),
        compiler_params=_params(2),
        name="attn_b",
    )(qt, k, vt, bias)


def _attn_c_kernel(qt_ref, k_ref, vt_ref, o_ref):
    n_blk, _, tq = qt_ref.shape[1:]
    items = [(qi, hd) for qi in range(n_blk) for hd in range(C_HEADS)]

    def qk(item):
        qi, hd = item
        rows = slice(hd * C_DIM, (hd + 1) * C_DIM)
        return _dot(k_ref[0, :, rows], qt_ref[0, qi, rows, :])

    scores = [qk(it) for it in items[:LOOKAHEAD]]
    outs = []
    for i, (qi, hd) in enumerate(items):
        s = scores.pop(0)
        if i + LOOKAHEAD < len(items):
            scores.append(qk(items[i + LOOKAHEAD]))
        _, acc = _online_softmax(s, None, None, vt_ref[0, hd * A_V_ROWS:(hd + 1) * A_V_ROWS, :])
        outs.append(_normalized(acc, C_DIM))
        if hd == C_HEADS - 1:
            o_ref[0, qi * tq:(qi + 1) * tq, :] = jnp.concatenate(outs, axis=0).T.astype(BF16)
            outs = []


def _attn_c(qt, mk, mvt):
    b, n_blk, w, tq = qt.shape
    n_mem = mk.shape[1]
    return pl.pallas_call(
        _attn_c_kernel,
        grid=(b,),
        in_specs=[pl.BlockSpec((1, n_blk, w, tq), lambda bi: (bi, 0, 0, 0)),
                  pl.BlockSpec((1, n_mem, w), lambda bi: (bi, 0, 0)),
                  pl.BlockSpec((1, mvt.shape[1], n_mem), lambda bi: (bi, 0, 0))],
        out_specs=pl.BlockSpec((1, n_blk * tq, w), lambda bi: (bi, 0, 0)),
        out_shape=jax.ShapeDtypeStruct((b, n_blk * tq, w), BF16),
        compiler_params=_params(1),
        name="attn_c",
    )(qt, mk, mvt)


def _mix_kernel(x_ref, g_ref, wgate_ref, bgate_ref, oa_ref, ob_ref, oc_ref, wbr_ref, wout_ref,
                o_ref):
    x = x_ref[...]
    d = x.shape[1]
    h = _rms(x, g_ref[...]).astype(BF16)
    y = None
    for n, br_ref in enumerate((oa_ref, ob_ref, oc_ref)):
        z = _dot(h, wgate_ref[:, n * d:(n + 1) * d]) + bgate_ref[:, n * d:(n + 1) * d]
        term = (1.0 / (1.0 + jnp.exp(-z))) * _dot(br_ref[...], wbr_ref[n])
        y = term if y is None else y + term
    o_ref[...] = x + _dot(y.astype(BF16), wout_ref[...])


def _mix_out(x, g, w_gate, b_gate, o_a, o_b, o_c, w_branch, w_out):
    t, d = x.shape
    tm = FFN_TILE
    tile = lambda i: (i, 0)
    br_spec = pl.BlockSpec((tm, BRANCH_W), tile)
    return pl.pallas_call(
        _mix_kernel,
        grid=(t // tm,),
        in_specs=[pl.BlockSpec((tm, d), tile), _const_spec((1, d)), _const_spec(w_gate.shape),
                  _const_spec(b_gate.shape), br_spec, br_spec, br_spec,
                  _const_spec(w_branch.shape), _const_spec(w_out.shape)],
        out_specs=pl.BlockSpec((tm, d), tile),
        out_shape=jax.ShapeDtypeStruct((t, d), F32),
        compiler_params=_params(1),
        name="mix_out",
    )(x, g, w_gate, b_gate, o_a, o_b, o_c, w_branch, w_out)


def _rope_tables(seq):
    half = HEAD_DIM // 2
    inv = ROPE_THETA ** (-jnp.arange(half, dtype=F32) / half)
    ang = jnp.arange(seq, dtype=F32)[:, None] * inv[None, :]
    return jnp.cos(ang).T, jnp.sin(ang).T


def _group_mean_matrix(group):
    idx = jnp.arange(MXU_TILE) // group
    return jnp.where(idx[:, None] == idx[None, :], 1.0 / group, 0.0).astype(BF16)


def _bias_rows(table):
    h = table.shape[0]
    far = jnp.broadcast_to(table[:, -1:], (h, TOKEN_TILE - REL_CLIP))
    near = table[:, ::-1]
    tail = jnp.broadcast_to(table[:, -1:], (h, 2 * TOKEN_TILE - far.shape[1] - near.shape[1]))
    return jnp.concatenate([far, near, tail], axis=1)[:, None, :]


def kernel(x, mem, ffn1_norm, ffn1_wg, ffn1_wu, ffn1_wd, mix_norm, w_in, a_q_norm, a_k_norm,
           a_lambda, a_sub_norm, b_q_norm, b_k_norm, b_rel_bias, mem_norm, w_mem_kv, c_q_norm,
           c_k_norm, w_gate, b_gate, w_branch, w_out, ffn2_norm, ffn2_wg, ffn2_wu, ffn2_wd,
           final_norm):
    b, s, d = x.shape
    n_mem = mem.shape[1]
    depth = ffn1_norm.shape[0]
    tm = TOKEN_TILE
    w = BRANCH_W
    assert s % tm == 0 and d % MXU_TILE == 0 and w_in.shape[2] == 7 * w

    tables = _rope_tables(s)
    g128 = _group_mean_matrix(C_DIM)
    row = lambda v: v.reshape(1, -1)
    heads = lambda v: jnp.tile(v, w // v.shape[0])

    xt = x.reshape(b * s, d)
    memt = mem.reshape(b * n_mem, d)
    for l in range(depth):
        lambda_init = 0.8 - 0.6 * math.exp(-0.3 * l)
        xt, (wg2, wu2, wd2, w_in_b, w_gate_b, w_branch_b, w_out_b, wm) = _ffn(
            xt, row(ffn1_norm[l]), ffn1_wg[l].astype(BF16), ffn1_wu[l].astype(BF16),
            ffn1_wd[l].astype(BF16),
            to_bf16=(ffn2_wg[l], ffn2_wu[l], ffn2_wd[l], w_in[l], w_gate[l],
                     w_branch[l].reshape(-1, d), w_out[l], w_mem_kv[l]))
        w_branch_b = w_branch_b.reshape(w_branch[l].shape)

        col = lambda v: heads(v).reshape(w, 1)
        aqt, avt, bqt, bvt, cqt, ak, bk = _in_proj(
            xt, row(mix_norm[l]), w_in_b.T, tables,
            col(a_q_norm[l]), col(a_k_norm[l]), col(b_q_norm[l]), col(b_k_norm[l]),
            col(c_q_norm[l]), s)
        mk, mvt = _mem_proj(memt, row(mem_norm[l]), wm[:, :w], wm[:, w:].T,
                            row(heads(c_k_norm[l])), g128, n_mem)
        bias = _rel_bias(_bias_rows(b_rel_bias[l]))

        blocks = lambda a: a.reshape(b, s // tm, a.shape[1], tm)
        seq3 = lambda a: a.reshape(b, s, w)
        o_a = _attn_a(blocks(aqt), seq3(ak), blocks(avt), a_lambda[l], row(a_sub_norm[l]),
                      lambda_init)
        o_b = _attn_b(blocks(bqt), seq3(bk), blocks(bvt), bias)
        o_c = _attn_c(blocks(cqt), mk.reshape(b, n_mem, w), mvt)

        flat = lambda a: a.reshape(b * s, w)
        xt = _mix_out(xt, row(mix_norm[l]), w_gate_b, row(b_gate[l]),
                      flat(o_a), flat(o_b), flat(o_c), w_branch_b, w_out_b)

        xt, _ = _ffn(xt, row(ffn2_norm[l]), wg2, wu2, wd2, row(final_norm[l]))
    return xt.reshape(b, s, d)
```

```python
import functools
import math

import jax
import jax.numpy as jnp
from jax import lax
from jax.experimental import pallas as pl
from jax.experimental.pallas import tpu as pltpu

F32 = jnp.float32
BF16 = jnp.bfloat16

CHUNK = 64
HEAD_DIM = 64
A_HEADS = 4
B_HEADS = 8
B_LEFT_CHUNKS = 8
REL_CLIP = 128
C_HEADS = 4
C_DIM = 128
N_BRANCH = 3
BRANCH_W = 512
ROPE_THETA = 10000.0
EPS = 1e-6
NEG = -1e30
LOG2E = math.log2(math.e)

LANES = 128
MXU_TILE = 256
BF16_ROWS = 16
TOKEN_TILE = 512
FFN_TILE = 1024
IN_TILES = 2
KEY_PART = MXU_TILE
LOOKAHEAD = 2
A_KEY_PART = MXU_TILE
A_LOOKAHEAD = 2
A_QUERY_SUBS = 4
A_CHAIN_SOFTMAXES = 1
B_QUERY_SUBS = 4
B_BAND_PARTS = 3
ONES_ROWS = BF16_ROWS
VMEM_LIMIT = 56 * 1024 * 1024

A_V_ROWS = 2 * HEAD_DIM + ONES_ROWS
B_V_ROWS = HEAD_DIM + ONES_ROWS

assert TOKEN_TILE == B_LEFT_CHUNKS * CHUNK


def _dot(a, b):
    return jnp.dot(a, b, preferred_element_type=F32)


def _dot_nt(a, b):
    return lax.dot_general(a, b, (((1,), (1,)), ((), ())), preferred_element_type=F32)


def _rms(x, g):
    return x * lax.rsqrt(jnp.mean(x * x, axis=-1, keepdims=True) + EPS) * g


def _group_rms(z, gmat_ref, gain):
    z2 = (z * z).astype(BF16)
    ms = jnp.concatenate(
        [_dot(z2[:, :MXU_TILE], gmat_ref[...]), _dot(z2[:, MXU_TILE:], gmat_ref[...])], axis=1)
    return z * lax.rsqrt(ms + EPS) * gain


def _heads_t(z, gain, group, scale, cos=None, sin=None):
    outs = []
    for g0 in range(0, z.shape[0], group):
        zz = z[g0:g0 + group]
        r = lax.rsqrt(jnp.mean(zz * zz, axis=0, keepdims=True) + EPS) * scale
        y = zz * r * gain[g0:g0 + group]
        if cos is not None:
            y1, y2 = y[:group // 2], y[group // 2:]
            y = jnp.concatenate([y1 * cos - y2 * sin, y2 * cos + y1 * sin], axis=0)
        outs.append(y)
    return jnp.concatenate(outs, axis=0)


def _with_ones(vt, head_rows):
    ones = jnp.ones((ONES_ROWS, vt.shape[1]), vt.dtype)
    parts = []
    for r0 in range(0, vt.shape[0], head_rows):
        parts += [vt[r0:r0 + head_rows], ones]
    return jnp.concatenate(parts, axis=0)


def _const_spec(shape):
    nd = len(shape)
    return pl.BlockSpec(shape, lambda *_: (0,) * nd, pipeline_mode=pl.Buffered(1))


def _params(n_axes):
    return pltpu.CompilerParams(
        dimension_semantics=("arbitrary",) * n_axes, vmem_limit_bytes=VMEM_LIMIT)


def _ffn_chunks(d_ff):
    step = 3 * MXU_TILE
    return [(c, min(c + step, d_ff)) for c in range(0, d_ff, step)]


def _ffn_kernel(x_ref, g_ref, wg_ref, wu_ref, wd_ref, *rest, final_norm, n_cast):
    rest = list(rest)
    fg_ref = rest.pop(0) if final_norm else None
    cast_in, o_ref, cast_out, act_ref = (rest[:n_cast], rest[n_cast],
                                         rest[n_cast + 1:2 * n_cast + 1], rest[-1])
    for src, dst in zip(cast_in, cast_out):
        dst[...] = src[...].astype(BF16)
    x = x_ref[...]
    h = _rms(x, g_ref[...]).astype(BF16)
    for c0, c1 in _ffn_chunks(wg_ref.shape[1]):
        g = _dot(h, wg_ref[:, c0:c1])
        u = _dot(h, wu_ref[:, c0:c1])
        act_ref[:, c0:c1] = (g * (1.0 / (1.0 + jnp.exp(-g))) * u).astype(BF16)
    out = x + 0.5 * _dot(act_ref[...], wd_ref[...])
    if final_norm:
        out = _rms(out, fg_ref[...])
    o_ref[...] = out


def _ffn(x, g, wg, wu, wd, final_g=None, to_bf16=()):
    t, d = x.shape
    f = wg.shape[1]
    tm = FFN_TILE
    steps = t // tm
    tile = lambda i: (i, 0)
    in_specs = [pl.BlockSpec((tm, d), tile), _const_spec((1, d)),
                _const_spec((d, f)), _const_spec((d, f)), _const_spec((f, d))]
    args = [x, g, wg, wu, wd]
    if final_g is not None:
        in_specs.append(_const_spec((1, d)))
        args.append(final_g)
    cast_specs = []
    for a in to_bf16:
        n_blocks = math.gcd(steps, a.shape[0] // BF16_ROWS)
        rows, repeat = a.shape[0] // n_blocks, steps // n_blocks
        cast_specs.append(pl.BlockSpec((rows, a.shape[1]), lambda i, r=repeat: (i // r, 0)))
    outs = pl.pallas_call(
        functools.partial(_ffn_kernel, final_norm=final_g is not None, n_cast=len(to_bf16)),
        grid=(steps,),
        in_specs=in_specs + cast_specs,
        out_specs=[pl.BlockSpec((tm, d), tile)] + cast_specs,
        out_shape=([jax.ShapeDtypeStruct((t, d), F32)]
                   + [jax.ShapeDtypeStruct(a.shape, BF16) for a in to_bf16]),
        scratch_shapes=[pltpu.VMEM((tm, f), BF16)],
        compiler_params=_params(1),
        name="ffn_final" if final_g is not None else "ffn",
    )(*args, *to_bf16)
    return outs[0], outs[1:]


def _in_kernel(x_ref, g_ref, wt_ref, cost_ref, sint_ref,
               gaq_ref, gak_ref, gbq_ref, gbk_ref, gcq_ref,
               aqt_ref, avt_ref, bqt_ref, bvt_ref, cqt_ref, ak_ref, bk_ref):
    w = BRANCH_W
    tm = TOKEN_TILE
    qk_scale = HEAD_DIM ** -0.5 * LOG2E
    for sub in range(IN_TILES):
        rows = slice(sub * tm, (sub + 1) * tm)
        h = _rms(x_ref[rows, :], g_ref[...]).astype(BF16)
        cos, sin = cost_ref[:, rows], sint_ref[:, rows]

        def seg_t(i):
            return _dot_nt(wt_ref[i * w:(i + 1) * w, :], h)

        aqt_ref[sub] = _heads_t(seg_t(0), gaq_ref[...], HEAD_DIM, qk_scale, cos, sin
                                ).astype(BF16)
        ak_ref[rows, :] = _heads_t(seg_t(1), gak_ref[...], HEAD_DIM, 1.0, cos, sin
                                   ).T.astype(BF16)
        avt_ref[sub] = _with_ones(seg_t(2).astype(BF16), 2 * HEAD_DIM)
        bqt_ref[sub] = _heads_t(seg_t(3), gbq_ref[...], HEAD_DIM, qk_scale).astype(BF16)
        bk_ref[rows, :] = _heads_t(seg_t(4), gbk_ref[...], HEAD_DIM, 1.0).T.astype(BF16)
        bvt_ref[sub] = _with_ones(seg_t(5).astype(BF16), HEAD_DIM)
        cqt_ref[sub] = _heads_t(seg_t(6), gcq_ref[...], C_DIM, C_DIM ** -0.5 * LOG2E
                                ).astype(BF16)


def _in_proj(x, g, w_t, tables, gaq, gak, gbq, gbk, gcq, seq):
    t, d = x.shape
    tm = TOKEN_TILE
    ts = IN_TILES * tm
    n_tiles = t // tm
    n_seq_steps = seq // ts
    w = BRANCH_W
    cos_t, sin_t = tables
    tile = lambda i: (i, 0)
    half = HEAD_DIM // 2
    in_specs = [pl.BlockSpec((ts, d), tile), _const_spec((1, d)), _const_spec(w_t.shape),
                pl.BlockSpec((half, ts), lambda i: (0, i % n_seq_steps)),
                pl.BlockSpec((half, ts), lambda i: (0, i % n_seq_steps))]
    in_specs += [_const_spec((w, 1))] * 5
    t_rows = [w, A_HEADS * A_V_ROWS, w, B_HEADS * B_V_ROWS, w]
    return pl.pallas_call(
        _in_kernel,
        grid=(t // ts,),
        in_specs=in_specs,
        out_specs=([pl.BlockSpec((IN_TILES, r, tm), lambda i: (i, 0, 0)) for r in t_rows]
                   + [pl.BlockSpec((ts, w), tile)] * 2),
        out_shape=([jax.ShapeDtypeStruct((n_tiles, r, tm), BF16) for r in t_rows]
                   + [jax.ShapeDtypeStruct((t, w), BF16)] * 2),
        compiler_params=_params(1),
        name="in_proj",
    )(x, g, w_t, cos_t, sin_t, gaq, gak, gbq, gbk, gcq)


def _mem_kernel(m_ref, g_ref, wk_ref, wvt_ref, gk_ref, g128_ref, mk_ref, mvt_ref):
    h = _rms(m_ref[...], g_ref[...]).astype(BF16)
    mk_ref[...] = _group_rms(_dot(h, wk_ref[...]), g128_ref, gk_ref[...]).astype(BF16)
    mvt_ref[0] = _with_ones(_dot_nt(wvt_ref[...], h).astype(BF16), C_DIM)


def _mem_proj(mem, g, w_k, w_vt, gk, g128, n_mem):
    t, d = mem.shape
    w = BRANCH_W
    v_rows = C_HEADS * A_V_ROWS
    tile = lambda i: (i, 0)
    return pl.pallas_call(
        _mem_kernel,
        grid=(t // n_mem,),
        in_specs=[pl.BlockSpec((n_mem, d), tile), _const_spec((1, d)), _const_spec(w_k.shape),
                  _const_spec(w_vt.shape), _const_spec((1, w)),
                  _const_spec((MXU_TILE, MXU_TILE))],
        out_specs=[pl.BlockSpec((n_mem, w), tile),
                   pl.BlockSpec((1, v_rows, n_mem), lambda i: (i, 0, 0))],
        out_shape=[jax.ShapeDtypeStruct((t, w), BF16),
                   jax.ShapeDtypeStruct((t // n_mem, v_rows, n_mem), BF16)],
        compiler_params=_params(1),
        name="mem_proj",
    )(mem, g, w_k, w_vt, gk, g128)


def _bias_kernel(r_ref, o_ref):
    nq = TOKEN_TILE
    nk = 2 * nq
    hk = KEY_PART
    qc = lax.broadcasted_iota(jnp.int32, (nq, nk), 0) // CHUNK + B_LEFT_CHUNKS
    kc = lax.broadcasted_iota(jnp.int32, (nq, nk), 1) // CHUNK
    valid = (qc >= kc) & (qc - kc <= B_LEFT_CHUNKS)
    for hh in range(2):
        t = pltpu.roll(jnp.broadcast_to(r_ref[hh], (nq, nk)), 0, 1, stride=1, stride_axis=0)
        t = jnp.where(valid, t * LOG2E, NEG).T
        o_ref[0, :, hh * hk:(hh + 1) * hk] = t[:B_BAND_PARTS * hk, :hk]


def _rel_bias(rows):
    h, _, nk = rows.shape
    tile = (B_BAND_PARTS * KEY_PART, 2 * KEY_PART)
    return pl.pallas_call(
        _bias_kernel,
        grid=(h // 2,),
        in_specs=[pl.BlockSpec((2, 1, nk), lambda i: (i, 0, 0))],
        out_specs=pl.BlockSpec((1,) + tile, lambda i: (i, 0, 0)),
        out_shape=jax.ShapeDtypeStruct((h // 2,) + tile, F32),
        compiler_params=_params(1),
        name="rel_bias",
    )(rows)


def _split_queries(qt):
    row = lax.broadcasted_iota(jnp.int32, qt.shape, 0)
    zero = jnp.zeros_like(qt)
    return jnp.concatenate(
        [jnp.where(row < HEAD_DIM, qt, zero), jnp.where(row >= HEAD_DIM, qt, zero)], axis=1)


def _rows(start, size):
    if isinstance(start, int):
        return pl.ds(start, size)
    return pl.ds(pl.multiple_of(start, size), size)


def _column_max(s):
    return jnp.max(s, axis=0, keepdims=True)


def _online_softmax(s, m_prev, acc_prev, vt, s_max=None):
    if s_max is None:
        s_max = _column_max(s)
    m_new = s_max if m_prev is None else jnp.maximum(m_prev, s_max)
    pv = _dot(vt, jnp.exp2(s - m_new).astype(BF16))
    if acc_prev is None:
        return m_new, pv
    return m_new, jnp.exp2(m_prev - m_new) * acc_prev + pv


def _normalized(acc, rows):
    return acc[:rows] / acc[rows:rows + 1]


def _attn_a_kernel(qt_ref, k_ref, vt_ref, lam_ref, gain_ref, o_ref, q12_ref, *scratch,
                   lambda_init):
    n_sub, _, ts = qt_ref.shape[1:]
    subs = A_QUERY_SUBS
    n_blk = n_sub // subs
    per = A_CHAIN_SOFTMAXES
    cw = per * ts
    n_chains = 2 * subs // per
    look = A_LOOKAHEAD
    score_refs = scratch[:look]
    smax_refs = scratch[look:2 * look]
    state_refs = [scratch[2 * look + 2 * c:2 * look + 2 * c + 2] for c in range(n_chains)]
    hk = A_KEY_PART
    parts = ts // hk
    assert per == 1 and parts == 2
    tail_base = n_chains * cw
    for blk in range(n_blk):
        q12 = [_split_queries(qt_ref[0, blk * subs + h]) for h in range(subs)]
        tails = [jnp.concatenate([q[:, ts - hk:ts], q[:, 2 * ts - hk:]], axis=1) for q in q12]
        q12_ref[blk] = jnp.concatenate(q12 + tails, axis=1)
    lp = lam_ref[...]
    lam = (jnp.exp(jnp.sum(lp[0:1] * lp[1:2], axis=1, keepdims=True))
           - jnp.exp(jnp.sum(lp[2:3] * lp[3:4], axis=1, keepdims=True)) + lambda_init)


    def qk(item):
        qi, jb, kp, ch, _, tail = item
        c0 = tail_base + ch * ts if tail else ch * cw
        kb = k_ref[0, _rows(jb * ts + kp * hk, hk), :]
        s = _dot(kb, q12_ref[qi, :, c0:c0 + cw])
        return s, _column_max(s)

    chains_per_sub = 2 // per

    def reset(h):
        for m_ref, acc_ref in state_refs[chains_per_sub * h:chains_per_sub * (h + 1)]:
            m_ref[...] = jnp.full(m_ref.shape, NEG, F32)
            acc_ref[...] = jnp.zeros(acc_ref.shape, F32)

    def finish(qi, h):
        o12 = jnp.concatenate(
            [_normalized(acc_ref[...], 2 * HEAD_DIM)
             for _, acc_ref in state_refs[chains_per_sub * h:chains_per_sub * (h + 1)]], axis=1)
        o1, o2 = o12[:, :ts], o12[:, ts:]
        ot = o1 - lam * o2
        ot = ot * lax.rsqrt(jnp.mean(ot * ot, axis=0, keepdims=True) + EPS)
        o_ref[0, _rows((qi * subs + h) * ts, ts), :] = (
            ot.T * gain_ref[...] * (1.0 - lambda_init)).astype(BF16)
        reset(h)

    def run(seq, following, after=()):
        scores = [(score_refs[i][...], smax_refs[i][...]) for i in range(look)]
        for i, (qi, jb, kp, ch, masked, tail) in enumerate(seq):
            s, s_max = scores.pop(0)
            if masked:
                key_chunk = (lax.broadcasted_iota(jnp.int32, (hk, cw), 0) + kp * hk) // CHUNK
                query = lax.broadcasted_iota(jnp.int32, (hk, cw), 1)
                query_chunk = (query % hk + ts - hk if tail else query) // CHUNK
                s = jnp.where(key_chunk <= query_chunk, s, NEG)
                s_max = None
            nxt = i + look
            if nxt < len(seq):
                scores.append(qk(seq[nxt]))
            else:
                n = nxt - len(seq)
                score_refs[n][...], smax_refs[n][...] = qk(following[n])
            vt = vt_ref[0, jb, :, kp * hk:(kp + 1) * hk]
            if tail:
                refs = state_refs[2 * ch:2 * ch + 2]
                m_new, acc_new = _online_softmax(
                    s, jnp.concatenate([m_ref[:, ts - hk:] for m_ref, _ in refs], axis=1),
                    jnp.concatenate([acc_ref[:, ts - hk:] for _, acc_ref in refs], axis=1),
                    vt, s_max)
                for c, (m_ref, acc_ref) in enumerate(refs):
                    m_ref[:, ts - hk:] = m_new[:, c * hk:(c + 1) * hk]
                    acc_ref[:, ts - hk:] = acc_new[:, c * hk:(c + 1) * hk]
            else:
                m_ref, acc_ref = state_refs[ch]
                m_ref[...], acc_ref[...] = _online_softmax(s, m_ref[...], acc_ref[...], vt, s_max)
            if i in after:
                finish(qi, after[i])

    def key_block(qi, jb):
        return [(qi, jb, kp, ch, False, False) for kp in range(parts) for ch in range(n_chains)]

    def own_blocks(qi):
        seq, after = [], {}
        for kb in range(subs):
            for kp in range(parts):
                for h in range(kb, subs):
                    if h == kb and kp == parts - 1:
                        seq.append((qi, subs * qi + kb, kp, h, True, True))
                        after[len(seq) - 1] = kb
                    else:
                        seq += [(qi, subs * qi + kb, kp, 2 * h + c, h == kb, False)
                                for c in range(2)]
        return seq, after

    def query_block(qi, carry):
        def body(jb, c):
            run(key_block(qi, jb), key_block(qi, jb + 1))
            return c

        lax.fori_loop(0, subs * qi, body, 0)
        seq, after = own_blocks(qi)
        nxt_qi = jnp.minimum(qi + 1, n_blk - 1)
        run(seq, key_block(nxt_qi, 0), after)
        return carry

    for h in range(subs):
        reset(h)
    for i, item in enumerate(key_block(0, 0)[:look]):
        score_refs[i][...], smax_refs[i][...] = qk(item)
    if n_blk == 1:
        query_block(0, 0)
    else:
        lax.fori_loop(0, n_blk, query_block, 0)


def _attn_a(qt, k, vt, lam_p, gain, lambda_init):
    b, n_sub, w, ts = qt.shape
    s = n_sub * ts
    n_chains = 2 * A_QUERY_SUBS // A_CHAIN_SOFTMAXES
    cw = A_CHAIN_SOFTMAXES * ts
    return pl.pallas_call(
        functools.partial(_attn_a_kernel, lambda_init=lambda_init),
        grid=(b, w // LANES),
        in_specs=[pl.BlockSpec((1, n_sub, LANES, ts), lambda bi, hi: (bi, 0, hi, 0)),
                  pl.BlockSpec((1, s, LANES), lambda bi, hi: (bi, 0, hi)),
                  pl.BlockSpec((1, n_sub, A_V_ROWS, ts), lambda bi, hi: (bi, 0, hi, 0)),
                  _const_spec(lam_p.shape), _const_spec((1, LANES))],
        out_specs=pl.BlockSpec((1, s, LANES), lambda bi, hi: (bi, 0, hi)),
        out_shape=jax.ShapeDtypeStruct((b, s, w), BF16),
        scratch_shapes=(
            [pltpu.VMEM((n_sub // A_QUERY_SUBS, LANES, n_chains * cw + A_QUERY_SUBS * ts), BF16)]
            + [pltpu.VMEM((A_KEY_PART, cw), F32)] * A_LOOKAHEAD
            + [pltpu.VMEM((1, cw), F32)] * A_LOOKAHEAD
            + [pltpu.VMEM((1, cw), F32), pltpu.VMEM((A_V_ROWS, cw), F32)] * n_chains),
        compiler_params=_params(2),
        name="attn_a",
    )(qt, k, vt, lam_p, gain)


def _attn_b_kernel(qt_ref, k_ref, vt_ref, bias_ref, o_ref, q12_ref, *score_refs):
    n_sub, _, ts = qt_ref.shape[1:]
    hk = KEY_PART
    per_sub = ts // hk
    n_chains = B_QUERY_SUBS * per_sub
    n_blk = n_sub // B_QUERY_SUBS
    lead = B_BAND_PARTS - 1
    assert lead * hk == B_LEFT_CHUNKS * CHUNK and LOOKAHEAD <= n_chains
    for blk in range(n_blk):
        cols = []
        for c in range(n_chains):
            q12 = _split_queries(qt_ref[0, blk * B_QUERY_SUBS + c // per_sub])
            c0 = (c % per_sub) * hk
            cols += [q12[:, c0:c0 + hk], q12[:, ts + c0:ts + c0 + hk]]
        q12_ref[blk] = jnp.concatenate(cols, axis=1)

    items = [(c + d, c) for d in range(lead, -1, -1) for c in range(n_chains)]

    def source(qi, wp):
        part = qi * n_chains + wp - lead
        return jnp.maximum(part, 0) // per_sub, (wp - lead) % per_sub * hk

    def qk(qi, item):
        wp, c = item
        blk, r0 = source(qi, wp)
        cols = slice(c * 2 * hk, (c + 1) * 2 * hk)
        d = wp - c
        kb = k_ref[0, pl.ds(pl.multiple_of(blk * ts + r0, hk), hk), :]
        s = _dot(kb, q12_ref[qi, :, cols]) + bias_ref[0, d * hk:(d + 1) * hk, :]
        if wp < lead:
            s = jnp.where(qi > 0, s, NEG)
        return s

    def query_block(qi, carry):
        scores = [r[...] for r in score_refs]
        state = [(None, None)] * n_chains
        for i, (wp, c) in enumerate(items):
            s = scores.pop(0)
            nxt = i + LOOKAHEAD
            if nxt < len(items):
                scores.append(qk(qi, items[nxt]))
            else:
                score_refs[nxt - len(items)][...] = qk(
                    jnp.minimum(qi + 1, n_blk - 1), items[nxt - len(items)])
            blk, r0 = source(qi, wp)
            state[c] = _online_softmax(s, state[c][0], state[c][1],
                                       vt_ref[0, blk, :, r0:r0 + hk])
        outs = []
        for _, acc in state:
            heads = [_normalized(acc[hh * B_V_ROWS:(hh + 1) * B_V_ROWS,
                                     hh * hk:(hh + 1) * hk], HEAD_DIM) for hh in range(2)]
            outs.append(jnp.concatenate(heads, axis=0))
        for h in range(B_QUERY_SUBS):
            ot = jnp.concatenate(outs[h * per_sub:(h + 1) * per_sub], axis=1)
            o_ref[0, pl.ds(pl.multiple_of((qi * B_QUERY_SUBS + h) * ts, ts), ts), :] = (
                ot.T.astype(BF16))
        return carry

    for i, r in enumerate(score_refs):
        r[...] = qk(0, items[i])
    lax.fori_loop(0, n_blk, query_block, 0)


def _attn_b(qt, k, vt, bias):
    b, n_sub, w, ts = qt.shape
    s = n_sub * ts
    return pl.pallas_call(
        _attn_b_kernel,
        grid=(w // LANES, b),
        in_specs=[pl.BlockSpec((1, n_sub, LANES, ts), lambda pi, bi: (bi, 0, pi, 0)),
                  pl.BlockSpec((1, s, LANES), lambda pi, bi: (bi, 0, pi)),
                  pl.BlockSpec((1, n_sub, 2 * B_V_ROWS, ts), lambda pi, bi: (bi, 0, pi, 0)),
                  pl.BlockSpec((1,) + bias.shape[1:], lambda pi, bi: (pi, 0, 0))],
        out_specs=pl.BlockSpec((1, s, LANES), lambda pi, bi: (bi, 0, pi)),
        out_shape=jax.ShapeDtypeStruct((b, s, w), BF16),
        scratch_shapes=([pltpu.VMEM((n_sub // B_QUERY_SUBS, LANES, 2 * B_QUERY_SUBS * ts), BF16)]
                        + [pltpu.VMEM((KEY_PART, 2 * KEY_PART), F32)] * LOOKAHEAD),
        compiler_params=_params(2),
        name="attn_b",
    )(qt, k, vt, bias)


def _attn_c_kernel(qt_ref, k_ref, vt_ref, o_ref):
    n_blk, _, tq = qt_ref.shape[1:]
    items = [(qi, hd) for qi in range(n_blk) for hd in range(C_HEADS)]

    def qk(item):
        qi, hd = item
        rows = slice(hd * C_DIM, (hd + 1) * C_DIM)
        return _dot(k_ref[0, :, rows], qt_ref[0, qi, rows, :])

    scores = [qk(it) for it in items[:LOOKAHEAD]]
    outs = []
    for i, (qi, hd) in enumerate(items):
        s = scores.pop(0)
        if i + LOOKAHEAD < len(items):
            scores.append(qk(items[i + LOOKAHEAD]))
        _, acc = _online_softmax(s, None, None, vt_ref[0, hd * A_V_ROWS:(hd + 1) * A_V_ROWS, :])
        outs.append(_normalized(acc, C_DIM))
        if hd == C_HEADS - 1:
            o_ref[0, qi * tq:(qi + 1) * tq, :] = jnp.concatenate(outs, axis=0).T.astype(BF16)
            outs = []


def _attn_c(qt, mk, mvt):
    b, n_blk, w, tq = qt.shape
    n_mem = mk.shape[1]
    return pl.pallas_call(
        _attn_c_kernel,
        grid=(b,),
        in_specs=[pl.BlockSpec((1, n_blk, w, tq), lambda bi: (bi, 0, 0, 0)),
                  pl.BlockSpec((1, n_mem, w), lambda bi: (bi, 0, 0)),
                  pl.BlockSpec((1, mvt.shape[1], n_mem), lambda bi: (bi, 0, 0))],
        out_specs=pl.BlockSpec((1, n_blk * tq, w), lambda bi: (bi, 0, 0)),
        out_shape=jax.ShapeDtypeStruct((b, n_blk * tq, w), BF16),
        compiler_params=_params(1),
        name="attn_c",
    )(qt, mk, mvt)


def _mix_kernel(x_ref, g_ref, wgate_ref, bgate_ref, oa_ref, ob_ref, oc_ref, wbr_ref, wout_ref,
                o_ref):
    x = x_ref[...]
    d = x.shape[1]
    h = _rms(x, g_ref[...]).astype(BF16)
    y = None
    for n, br_ref in enumerate((oa_ref, ob_ref, oc_ref)):
        z = _dot(h, wgate_ref[:, n * d:(n + 1) * d]) + bgate_ref[:, n * d:(n + 1) * d]
        term = (1.0 / (1.0 + jnp.exp(-z))) * _dot(br_ref[...], wbr_ref[n])
        y = term if y is None else y + term
    o_ref[...] = x + _dot(y.astype(BF16), wout_ref[...])


def _mix_out(x, g, w_gate, b_gate, o_a, o_b, o_c, w_branch, w_out):
    t, d = x.shape
    tm = FFN_TILE
    tile = lambda i: (i, 0)
    br_spec = pl.BlockSpec((tm, BRANCH_W), tile)
    return pl.pallas_call(
        _mix_kernel,
        grid=(t // tm,),
        in_specs=[pl.BlockSpec((tm, d), tile), _const_spec((1, d)), _const_spec(w_gate.shape),
                  _const_spec(b_gate.shape), br_spec, br_spec, br_spec,
                  _const_spec(w_branch.shape), _const_spec(w_out.shape)],
        out_specs=pl.BlockSpec((tm, d), tile),
        out_shape=jax.ShapeDtypeStruct((t, d), F32),
        compiler_params=_params(1),
        name="mix_out",
    )(x, g, w_gate, b_gate, o_a, o_b, o_c, w_branch, w_out)


def _rope_tables(seq):
    half = HEAD_DIM // 2
    inv = ROPE_THETA ** (-jnp.arange(half, dtype=F32) / half)
    ang = jnp.arange(seq, dtype=F32)[:, None] * inv[None, :]
    return jnp.cos(ang).T, jnp.sin(ang).T


def _group_mean_matrix(group):
    idx = jnp.arange(MXU_TILE) // group
    return jnp.where(idx[:, None] == idx[None, :], 1.0 / group, 0.0).astype(BF16)


def _bias_rows(table):
    h = table.shape[0]
    far = jnp.broadcast_to(table[:, -1:], (h, TOKEN_TILE - REL_CLIP))
    near = table[:, ::-1]
    tail = jnp.broadcast_to(table[:, -1:], (h, 2 * TOKEN_TILE - far.shape[1] - near.shape[1]))
    return jnp.concatenate([far, near, tail], axis=1)[:, None, :]


def kernel(x, mem, ffn1_norm, ffn1_wg, ffn1_wu, ffn1_wd, mix_norm, w_in, a_q_norm, a_k_norm,
           a_lambda, a_sub_norm, b_q_norm, b_k_norm, b_rel_bias, mem_norm, w_mem_kv, c_q_norm,
           c_k_norm, w_gate, b_gate, w_branch, w_out, ffn2_norm, ffn2_wg, ffn2_wu, ffn2_wd,
           final_norm):
    b, s, d = x.shape
    n_mem = mem.shape[1]
    depth = ffn1_norm.shape[0]
    tm = TOKEN_TILE
    w = BRANCH_W
    assert s % tm == 0 and d % MXU_TILE == 0 and w_in.shape[2] == 7 * w

    tables = _rope_tables(s)
    g128 = _group_mean_matrix(C_DIM)
    row = lambda v: v.reshape(1, -1)
    heads = lambda v: jnp.tile(v, w // v.shape[0])

    xt = x.reshape(b * s, d)
    memt = mem.reshape(b * n_mem, d)
    for l in range(depth):
        lambda_init = 0.8 - 0.6 * math.exp(-0.3 * l)
        xt, (wg2, wu2, wd2, w_in_b, w_gate_b, w_branch_b, w_out_b, wm) = _ffn(
            xt, row(ffn1_norm[l]), ffn1_wg[l].astype(BF16), ffn1_wu[l].astype(BF16),
            ffn1_wd[l].astype(BF16),
            to_bf16=(ffn2_wg[l], ffn2_wu[l], ffn2_wd[l], w_in[l], w_gate[l],
                     w_branch[l].reshape(-1, d), w_out[l], w_mem_kv[l]))
        w_branch_b = w_branch_b.reshape(w_branch[l].shape)

        col = lambda v: heads(v).reshape(w, 1)
        aqt, avt, bqt, bvt, cqt, ak, bk = _in_proj(
            xt, row(mix_norm[l]), w_in_b.T, tables,
            col(a_q_norm[l]), col(a_k_norm[l]), col(b_q_norm[l]), col(b_k_norm[l]),
            col(c_q_norm[l]), s)
        mk, mvt = _mem_proj(memt, row(mem_norm[l]), wm[:, :w], wm[:, w:].T,
                            row(heads(c_k_norm[l])), g128, n_mem)
        bias = _rel_bias(_bias_rows(b_rel_bias[l]))

        blocks = lambda a: a.reshape(b, s // tm, a.shape[1], tm)
        seq3 = lambda a: a.reshape(b, s, w)
        o_a = _attn_a(blocks(aqt), seq3(ak), blocks(avt), a_lambda[l], row(a_sub_norm[l]),
                      lambda_init)
        o_b = _attn_b(blocks(bqt), seq3(bk), blocks(bvt), bias)
        o_c = _attn_c(blocks(cqt), mk.reshape(b, n_mem, w), mvt)

        flat = lambda a: a.reshape(b * s, w)
        xt = _mix_out(xt, row(mix_norm[l]), w_gate_b, row(b_gate[l]),
                      flat(o_a), flat(o_b), flat(o_c), w_branch_b, w_out_b)

        xt, _ = _ffn(xt, row(ffn2_norm[l]), wg2, wu2, wd2, row(final_norm[l]))
    return xt.reshape(b, s, d)
```

```python
import functools
import math

import jax
import jax.numpy as jnp
from jax import lax
from jax.experimental import pallas as pl
from jax.experimental.pallas import tpu as pltpu

F32 = jnp.float32
BF16 = jnp.bfloat16

CHUNK = 64
HEAD_DIM = 64
A_HEADS = 4
B_HEADS = 8
B_LEFT_CHUNKS = 8
REL_CLIP = 128
C_HEADS = 4
C_DIM = 128
N_BRANCH = 3
BRANCH_W = 512
ROPE_THETA = 10000.0
EPS = 1e-6
NEG = -1e30
LOG2E = math.log2(math.e)

LANES = 128
MXU_TILE = 256
BF16_ROWS = 16
TOKEN_TILE = 512
FFN_TILE = 1024
IN_TILES = 2
KEY_PART = MXU_TILE
LOOKAHEAD = 2
A_KEY_PART = MXU_TILE
A_LOOKAHEAD = 2
A_QUERY_SUBS = 4
A_CHAIN_SOFTMAXES = 1
B_QUERY_SUBS = 4
B_BAND_PARTS = 3
ONES_ROWS = BF16_ROWS
VMEM_LIMIT = 56 * 1024 * 1024

A_V_ROWS = 2 * HEAD_DIM + ONES_ROWS
B_V_ROWS = HEAD_DIM + ONES_ROWS

assert TOKEN_TILE == B_LEFT_CHUNKS * CHUNK


def _dot(a, b):
    return jnp.dot(a, b, preferred_element_type=F32)


def _dot_nt(a, b):
    return lax.dot_general(a, b, (((1,), (1,)), ((), ())), preferred_element_type=F32)


def _rms(x, g):
    return x * lax.rsqrt(jnp.mean(x * x, axis=-1, keepdims=True) + EPS) * g


def _group_rms(z, gmat_ref, gain):
    z2 = (z * z).astype(BF16)
    ms = jnp.concatenate(
        [_dot(z2[:, :MXU_TILE], gmat_ref[...]), _dot(z2[:, MXU_TILE:], gmat_ref[...])], axis=1)
    return z * lax.rsqrt(ms + EPS) * gain


def _heads_t(z, gain, group, scale, cos=None, sin=None):
    outs = []
    for g0 in range(0, z.shape[0], group):
        zz = z[g0:g0 + group]
        r = lax.rsqrt(jnp.mean(zz * zz, axis=0, keepdims=True) + EPS) * scale
        y = zz * r * gain[g0:g0 + group]
        if cos is not None:
            y1, y2 = y[:group // 2], y[group // 2:]
            y = jnp.concatenate([y1 * cos - y2 * sin, y2 * cos + y1 * sin], axis=0)
        outs.append(y)
    return jnp.concatenate(outs, axis=0)


def _with_ones(vt, head_rows):
    ones = jnp.ones((ONES_ROWS, vt.shape[1]), vt.dtype)
    parts = []
    for r0 in range(0, vt.shape[0], head_rows):
        parts += [vt[r0:r0 + head_rows], ones]
    return jnp.concatenate(parts, axis=0)


def _const_spec(shape):
    nd = len(shape)
    return pl.BlockSpec(shape, lambda *_: (0,) * nd, pipeline_mode=pl.Buffered(1))


def _params(n_axes):
    return pltpu.CompilerParams(
        dimension_semantics=("arbitrary",) * n_axes, vmem_limit_bytes=VMEM_LIMIT)


def _ffn_chunks(d_ff):
    step = 3 * MXU_TILE
    return [(c, min(c + step, d_ff)) for c in range(0, d_ff, step)]


def _ffn_kernel(x_ref, g_ref, wg_ref, wu_ref, wd_ref, *rest, final_norm, n_cast):
    rest = list(rest)
    fg_ref = rest.pop(0) if final_norm else None
    cast_in, o_ref, cast_out, act_ref = (rest[:n_cast], rest[n_cast],
                                         rest[n_cast + 1:2 * n_cast + 1], rest[-1])
    for src, dst in zip(cast_in, cast_out):
        dst[...] = src[...].astype(BF16)
    x = x_ref[...]
    h = _rms(x, g_ref[...]).astype(BF16)
    for c0, c1 in _ffn_chunks(wg_ref.shape[1]):
        g = _dot(h, wg_ref[:, c0:c1])
        u = _dot(h, wu_ref[:, c0:c1])
        act_ref[:, c0:c1] = (g * (1.0 / (1.0 + jnp.exp(-g))) * u).astype(BF16)
    out = x + 0.5 * _dot(act_ref[...], wd_ref[...])
    if final_norm:
        out = _rms(out, fg_ref[...])
    o_ref[...] = out


def _ffn(x, g, wg, wu, wd, final_g=None, to_bf16=()):
    t, d = x.shape
    f = wg.shape[1]
    tm = FFN_TILE
    steps = t // tm
    tile = lambda i: (i, 0)
    in_specs = [pl.BlockSpec((tm, d), tile), _const_spec((1, d)),
                _const_spec((d, f)), _const_spec((d, f)), _const_spec((f, d))]
    args = [x, g, wg, wu, wd]
    if final_g is not None:
        in_specs.append(_const_spec((1, d)))
        args.append(final_g)
    cast_specs = []
    for a in to_bf16:
        n_blocks = math.gcd(steps, a.shape[0] // BF16_ROWS)
        rows, repeat = a.shape[0] // n_blocks, steps // n_blocks
        cast_specs.append(pl.BlockSpec((rows, a.shape[1]), lambda i, r=repeat: (i // r, 0)))
    outs = pl.pallas_call(
        functools.partial(_ffn_kernel, final_norm=final_g is not None, n_cast=len(to_bf16)),
        grid=(steps,),
        in_specs=in_specs + cast_specs,
        out_specs=[pl.BlockSpec((tm, d), tile)] + cast_specs,
        out_shape=([jax.ShapeDtypeStruct((t, d), F32)]
                   + [jax.ShapeDtypeStruct(a.shape, BF16) for a in to_bf16]),
        scratch_shapes=[pltpu.VMEM((tm, f), BF16)],
        compiler_params=_params(1),
        name="ffn_final" if final_g is not None else "ffn",
    )(*args, *to_bf16)
    return outs[0], outs[1:]


def _in_kernel(x_ref, g_ref, wt_ref, cost_ref, sint_ref,
               gaq_ref, gak_ref, gbq_ref, gbk_ref, gcq_ref,
               aqt_ref, avt_ref, bqt_ref, bvt_ref, cqt_ref, ak_ref, bk_ref):
    w = BRANCH_W
    tm = TOKEN_TILE
    qk_scale = HEAD_DIM ** -0.5 * LOG2E
    for sub in range(IN_TILES):
        rows = slice(sub * tm, (sub + 1) * tm)
        h = _rms(x_ref[rows, :], g_ref[...]).astype(BF16)
        cos, sin = cost_ref[:, rows], sint_ref[:, rows]

        def seg_t(i):
            return _dot_nt(wt_ref[i * w:(i + 1) * w, :], h)

        aqt_ref[sub] = _heads_t(seg_t(0), gaq_ref[...], HEAD_DIM, qk_scale, cos, sin
                                ).astype(BF16)
        ak_ref[rows, :] = _heads_t(seg_t(1), gak_ref[...], HEAD_DIM, 1.0, cos, sin
                                   ).T.astype(BF16)
        avt_ref[sub] = _with_ones(seg_t(2).astype(BF16), 2 * HEAD_DIM)
        bqt_ref[sub] = _heads_t(seg_t(3), gbq_ref[...], HEAD_DIM, qk_scale).astype(BF16)
        bk_ref[rows, :] = _heads_t(seg_t(4), gbk_ref[...], HEAD_DIM, 1.0).T.astype(BF16)
        bvt_ref[sub] = _with_ones(seg_t(5).astype(BF16), HEAD_DIM)
        cqt_ref[sub] = _heads_t(seg_t(6), gcq_ref[...], C_DIM, C_DIM ** -0.5 * LOG2E
                                ).astype(BF16)


def _in_proj(x, g, w_t, tables, gaq, gak, gbq, gbk, gcq, seq):
    t, d = x.shape
    tm = TOKEN_TILE
    ts = IN_TILES * tm
    n_tiles = t // tm
    n_seq_steps = seq // ts
    w = BRANCH_W
    cos_t, sin_t = tables
    tile = lambda i: (i, 0)
    half = HEAD_DIM // 2
    in_specs = [pl.BlockSpec((ts, d), tile), _const_spec((1, d)), _const_spec(w_t.shape),
                pl.BlockSpec((half, ts), lambda i: (0, i % n_seq_steps)),
                pl.BlockSpec((half, ts), lambda i: (0, i % n_seq_steps))]
    in_specs += [_const_spec((w, 1))] * 5
    t_rows = [w, A_HEADS * A_V_ROWS, w, B_HEADS * B_V_ROWS, w]
    return pl.pallas_call(
        _in_kernel,
        grid=(t // ts,),
        in_specs=in_specs,
        out_specs=([pl.BlockSpec((IN_TILES, r, tm), lambda i: (i, 0, 0)) for r in t_rows]
                   + [pl.BlockSpec((ts, w), tile)] * 2),
        out_shape=([jax.ShapeDtypeStruct((n_tiles, r, tm), BF16) for r in t_rows]
                   + [jax.ShapeDtypeStruct((t, w), BF16)] * 2),
        compiler_params=_params(1),
        name="in_proj",
    )(x, g, w_t, cos_t, sin_t, gaq, gak, gbq, gbk, gcq)


def _mem_kernel(m_ref, g_ref, wk_ref, wvt_ref, gk_ref, g128_ref, mk_ref, mvt_ref):
    h = _rms(m_ref[...], g_ref[...]).astype(BF16)
    mk_ref[...] = _group_rms(_dot(h, wk_ref[...]), g128_ref, gk_ref[...]).astype(BF16)
    mvt_ref[0] = _with_ones(_dot_nt(wvt_ref[...], h).astype(BF16), C_DIM)


def _mem_proj(mem, g, w_k, w_vt, gk, g128, n_mem):
    t, d = mem.shape
    w = BRANCH_W
    v_rows = C_HEADS * A_V_ROWS
    tile = lambda i: (i, 0)
    return pl.pallas_call(
        _mem_kernel,
        grid=(t // n_mem,),
        in_specs=[pl.BlockSpec((n_mem, d), tile), _const_spec((1, d)), _const_spec(w_k.shape),
                  _const_spec(w_vt.shape), _const_spec((1, w)),
                  _const_spec((MXU_TILE, MXU_TILE))],
        out_specs=[pl.BlockSpec((n_mem, w), tile),
                   pl.BlockSpec((1, v_rows, n_mem), lambda i: (i, 0, 0))],
        out_shape=[jax.ShapeDtypeStruct((t, w), BF16),
                   jax.ShapeDtypeStruct((t // n_mem, v_rows, n_mem), BF16)],
        compiler_params=_params(1),
        name="mem_proj",
    )(mem, g, w_k, w_vt, gk, g128)


def _bias_kernel(r_ref, o_ref):
    nq = TOKEN_TILE
    nk = 2 * nq
    hk = KEY_PART
    qc = lax.broadcasted_iota(jnp.int32, (nq, nk), 0) // CHUNK + B_LEFT_CHUNKS
    kc = lax.broadcasted_iota(jnp.int32, (nq, nk), 1) // CHUNK
    valid = (qc >= kc) & (qc - kc <= B_LEFT_CHUNKS)
    for hh in range(2):
        t = pltpu.roll(jnp.broadcast_to(r_ref[hh], (nq, nk)), 0, 1, stride=1, stride_axis=0)
        t = jnp.where(valid, t * LOG2E, NEG).T
        o_ref[0, :, hh * hk:(hh + 1) * hk] = t[:B_BAND_PARTS * hk, :hk]


def _rel_bias(rows):
    h, _, nk = rows.shape
    tile = (B_BAND_PARTS * KEY_PART, 2 * KEY_PART)
    return pl.pallas_call(
        _bias_kernel,
        grid=(h // 2,),
        in_specs=[pl.BlockSpec((2, 1, nk), lambda i: (i, 0, 0))],
        out_specs=pl.BlockSpec((1,) + tile, lambda i: (i, 0, 0)),
        out_shape=jax.ShapeDtypeStruct((h // 2,) + tile, F32),
        compiler_params=_params(1),
        name="rel_bias",
    )(rows)


def _split_queries(qt):
    row = lax.broadcasted_iota(jnp.int32, qt.shape, 0)
    zero = jnp.zeros_like(qt)
    return jnp.concatenate(
        [jnp.where(row < HEAD_DIM, qt, zero), jnp.where(row >= HEAD_DIM, qt, zero)], axis=1)


def _rows(start, size):
    if isinstance(start, int):
        return pl.ds(start, size)
    return pl.ds(pl.multiple_of(start, size), size)


def _column_max(s):
    return jnp.max(s, axis=0, keepdims=True)


def _online_softmax(s, m_prev, acc_prev, vt, s_max=None):
    if s_max is None:
        s_max = _column_max(s)
    m_new = s_max if m_prev is None else jnp.maximum(m_prev, s_max)
    pv = _dot(vt, jnp.exp2(s - m_new).astype(BF16))
    if acc_prev is None:
        return m_new, pv
    return m_new, jnp.exp2(m_prev - m_new) * acc_prev + pv


def _normalized(acc, rows):
    return acc[:rows] / acc[rows:rows + 1]


def _attn_a_kernel(qt_ref, k_ref, vt_ref, lam_ref, gain_ref, o_ref, q12_ref, *scratch,
                   lambda_init):
    n_sub, _, ts = qt_ref.shape[1:]
    subs = A_QUERY_SUBS
    n_blk = n_sub // subs
    per = A_CHAIN_SOFTMAXES
    cw = per * ts
    n_chains = 2 * subs // per
    score_refs = scratch[:2]
    smax_refs = scratch[2:4]
    tail_ref = scratch[4]
    state_refs = [scratch[5 + 2 * c:7 + 2 * c] for c in range(n_chains)]
    hk = A_KEY_PART
    parts = ts // hk
    assert per == 1 and parts == 2
    tail_base = n_chains * cw
    for blk in range(n_blk):
        q12 = [_split_queries(qt_ref[0, blk * subs + h]) for h in range(subs)]
        tails = [jnp.concatenate([q[:, ts - hk:ts], q[:, 2 * ts - hk:]], axis=1) for q in q12]
        q12_ref[blk] = jnp.concatenate(q12 + tails, axis=1)
    lp = lam_ref[...]
    lam = (jnp.exp(jnp.sum(lp[0:1] * lp[1:2], axis=1, keepdims=True))
           - jnp.exp(jnp.sum(lp[2:3] * lp[3:4], axis=1, keepdims=True)) + lambda_init)


    def issue(part):
        qi, jb, kp, first, _, tail = part
        kb = k_ref[0, _rows(jb * ts + kp * hk, hk), :]
        if first < n_chains:
            s = _dot(kb, q12_ref[qi, :, first * ts:tail_base])
            score_refs[kp][:, first * ts:] = s
            smax_refs[kp][:, first * ts:] = _column_max(s)
        if tail is not None:
            c0 = tail_base + tail * ts
            tail_ref[...] = _dot(kb, q12_ref[qi, :, c0:c0 + ts])

    chains_per_sub = 2 // per

    def reset(h):
        for m_ref, acc_ref in state_refs[chains_per_sub * h:chains_per_sub * (h + 1)]:
            m_ref[...] = jnp.full(m_ref.shape, NEG, F32)
            acc_ref[...] = jnp.zeros(acc_ref.shape, F32)

    def finish(qi, h):
        o12 = jnp.concatenate(
            [_normalized(acc_ref[...], 2 * HEAD_DIM)
             for _, acc_ref in state_refs[chains_per_sub * h:chains_per_sub * (h + 1)]], axis=1)
        o1, o2 = o12[:, :ts], o12[:, ts:]
        ot = o1 - lam * o2
        ot = ot * lax.rsqrt(jnp.mean(ot * ot, axis=0, keepdims=True) + EPS)
        o_ref[0, _rows((qi * subs + h) * ts, ts), :] = (
            ot.T * gain_ref[...] * (1.0 - lambda_init)).astype(BF16)
        reset(h)

    def visible(kp, tail):
        key_chunk = (lax.broadcasted_iota(jnp.int32, (hk, ts), 0) + kp * hk) // CHUNK
        query = lax.broadcasted_iota(jnp.int32, (hk, ts), 1)
        return key_chunk <= (query % hk + ts - hk if tail else query) // CHUNK

    def consume(part):
        qi, jb, kp, first, masked, tail = part
        vt = vt_ref[0, jb, :, kp * hk:(kp + 1) * hk]
        if tail is not None:
            s = jnp.where(visible(kp, True), tail_ref[...], NEG)
            refs = state_refs[2 * tail:2 * tail + 2]
            m_new, acc_new = _online_softmax(
                s, jnp.concatenate([m_ref[:, ts - hk:] for m_ref, _ in refs], axis=1),
                jnp.concatenate([acc_ref[:, ts - hk:] for _, acc_ref in refs], axis=1), vt)
            for c, (m_ref, acc_ref) in enumerate(refs):
                m_ref[:, ts - hk:] = m_new[:, c * hk:(c + 1) * hk]
                acc_ref[:, ts - hk:] = acc_new[:, c * hk:(c + 1) * hk]
            finish(qi, tail)
        for ch in range(first, n_chains):
            cols = slice(ch * ts, (ch + 1) * ts)
            s, s_max = score_refs[kp][:, cols], smax_refs[kp][:, cols]
            if ch // 2 == masked:
                s, s_max = jnp.where(visible(kp, False), s, NEG), None
            m_ref, acc_ref = state_refs[ch]
            m_ref[...], acc_ref[...] = _online_softmax(s, m_ref[...], acc_ref[...], vt, s_max)

    def run(seq, following):
        for i, part in enumerate(seq):
            issue(seq[i + 1] if i + 1 < len(seq) else following)
            consume(part)

    def key_block(qi, jb):
        return [(qi, jb, kp, 0, None, None) for kp in range(parts)]

    def own_blocks(qi):
        return [part for kb in range(subs) for part in (
            (qi, subs * qi + kb, 0, 2 * kb, kb, None),
            (qi, subs * qi + kb, 1, 2 * kb + 2, None, kb))]

    def query_block(qi, carry):
        def body(jb, c):
            run(key_block(qi, jb), key_block(qi, jb + 1)[0])
            return c

        lax.fori_loop(0, subs * qi, body, 0)
        nxt_qi = jnp.minimum(qi + 1, n_blk - 1)
        run(own_blocks(qi), key_block(nxt_qi, 0)[0])
        return carry

    for h in range(subs):
        reset(h)
    issue(key_block(0, 0)[0])
    if n_blk == 1:
        query_block(0, 0)
    else:
        lax.fori_loop(0, n_blk, query_block, 0)


def _attn_a(qt, k, vt, lam_p, gain, lambda_init):
    b, n_sub, w, ts = qt.shape
    s = n_sub * ts
    n_chains = 2 * A_QUERY_SUBS // A_CHAIN_SOFTMAXES
    cw = A_CHAIN_SOFTMAXES * ts
    return pl.pallas_call(
        functools.partial(_attn_a_kernel, lambda_init=lambda_init),
        grid=(b, w // LANES),
        in_specs=[pl.BlockSpec((1, n_sub, LANES, ts), lambda bi, hi: (bi, 0, hi, 0)),
                  pl.BlockSpec((1, s, LANES), lambda bi, hi: (bi, 0, hi)),
                  pl.BlockSpec((1, n_sub, A_V_ROWS, ts), lambda bi, hi: (bi, 0, hi, 0)),
                  _const_spec(lam_p.shape), _const_spec((1, LANES))],
        out_specs=pl.BlockSpec((1, s, LANES), lambda bi, hi: (bi, 0, hi)),
        out_shape=jax.ShapeDtypeStruct((b, s, w), BF16),
        scratch_shapes=(
            [pltpu.VMEM((n_sub // A_QUERY_SUBS, LANES, n_chains * cw + A_QUERY_SUBS * ts), BF16)]
            + [pltpu.VMEM((A_KEY_PART, n_chains * cw), F32)] * 2
            + [pltpu.VMEM((1, n_chains * cw), F32)] * 2
            + [pltpu.VMEM((A_KEY_PART, ts), F32)]
            + [pltpu.VMEM((1, cw), F32), pltpu.VMEM((A_V_ROWS, cw), F32)] * n_chains),
        compiler_params=_params(2),
        name="attn_a",
    )(qt, k, vt, lam_p, gain)


def _attn_b_kernel(qt_ref, k_ref, vt_ref, bias_ref, o_ref, q12_ref, *score_refs):
    n_sub, _, ts = qt_ref.shape[1:]
    hk = KEY_PART
    per_sub = ts // hk
    n_chains = B_QUERY_SUBS * per_sub
    n_blk = n_sub // B_QUERY_SUBS
    lead = B_BAND_PARTS - 1
    assert lead * hk == B_LEFT_CHUNKS * CHUNK and LOOKAHEAD <= n_chains
    for blk in range(n_blk):
        cols = []
        for c in range(n_chains):
            q12 = _split_queries(qt_ref[0, blk * B_QUERY_SUBS + c // per_sub])
            c0 = (c % per_sub) * hk
            cols += [q12[:, c0:c0 + hk], q12[:, ts + c0:ts + c0 + hk]]
        q12_ref[blk] = jnp.concatenate(cols, axis=1)

    items = [(c + d, c) for d in range(lead, -1, -1) for c in range(n_chains)]

    def source(qi, wp):
        part = qi * n_chains + wp - lead
        return jnp.maximum(part, 0) // per_sub, (wp - lead) % per_sub * hk

    def qk(qi, item):
        wp, c = item
        blk, r0 = source(qi, wp)
        cols = slice(c * 2 * hk, (c + 1) * 2 * hk)
        d = wp - c
        kb = k_ref[0, pl.ds(pl.multiple_of(blk * ts + r0, hk), hk), :]
        s = _dot(kb, q12_ref[qi, :, cols]) + bias_ref[0, d * hk:(d + 1) * hk, :]
        if wp < lead:
            s = jnp.where(qi > 0, s, NEG)
        return s

    def query_block(qi, carry):
        scores = [r[...] for r in score_refs]
        state = [(None, None)] * n_chains
        for i, (wp, c) in enumerate(items):
            s = scores.pop(0)
            nxt = i + LOOKAHEAD
            if nxt < len(items):
                scores.append(qk(qi, items[nxt]))
            else:
                score_refs[nxt - len(items)][...] = qk(
                    jnp.minimum(qi + 1, n_blk - 1), items[nxt - len(items)])
            blk, r0 = source(qi, wp)
            state[c] = _online_softmax(s, state[c][0], state[c][1],
                                       vt_ref[0, blk, :, r0:r0 + hk])
        outs = []
        for _, acc in state:
            heads = [_normalized(acc[hh * B_V_ROWS:(hh + 1) * B_V_ROWS,
                                     hh * hk:(hh + 1) * hk], HEAD_DIM) for hh in range(2)]
            outs.append(jnp.concatenate(heads, axis=0))
        for h in range(B_QUERY_SUBS):
            ot = jnp.concatenate(outs[h * per_sub:(h + 1) * per_sub], axis=1)
            o_ref[0, pl.ds(pl.multiple_of((qi * B_QUERY_SUBS + h) * ts, ts), ts), :] = (
                ot.T.astype(BF16))
        return carry

    for i, r in enumerate(score_refs):
        r[...] = qk(0, items[i])
    lax.fori_loop(0, n_blk, query_block, 0)


def _attn_b(qt, k, vt, bias):
    b, n_sub, w, ts = qt.shape
    s = n_sub * ts
    return pl.pallas_call(
        _attn_b_kernel,
        grid=(w // LANES, b),
        in_specs=[pl.BlockSpec((1, n_sub, LANES, ts), lambda pi, bi: (bi, 0, pi, 0)),
                  pl.BlockSpec((1, s, LANES), lambda pi, bi: (bi, 0, pi)),
                  pl.BlockSpec((1, n_sub, 2 * B_V_ROWS, ts), lambda pi, bi: (bi, 0, pi, 0)),
                  pl.BlockSpec((1,) + bias.shape[1:], lambda pi, bi: (pi, 0, 0))],
        out_specs=pl.BlockSpec((1, s, LANES), lambda pi, bi: (bi, 0, pi)),
        out_shape=jax.ShapeDtypeStruct((b, s, w), BF16),
        scratch_shapes=([pltpu.VMEM((n_sub // B_QUERY_SUBS, LANES, 2 * B_QUERY_SUBS * ts), BF16)]
                        + [pltpu.VMEM((KEY_PART, 2 * KEY_PART), F32)] * LOOKAHEAD),
        compiler_params=_params(2),
        name="attn_b",
    )(qt, k, vt, bias)


def _attn_c_kernel(qt_ref, k_ref, vt_ref, o_ref):
    n_blk, _, tq = qt_ref.shape[1:]
    items = [(qi, hd) for qi in range(n_blk) for hd in range(C_HEADS)]

    def qk(item):
        qi, hd = item
        rows = slice(hd * C_DIM, (hd + 1) * C_DIM)
        return _dot(k_ref[0, :, rows], qt_ref[0, qi, rows, :])

    scores = [qk(it) for it in items[:LOOKAHEAD]]
    outs = []
    for i, (qi, hd) in enumerate(items):
        s = scores.pop(0)
        if i + LOOKAHEAD < len(items):
            scores.append(qk(items[i + LOOKAHEAD]))
        _, acc = _online_softmax(s, None, None, vt_ref[0, hd * A_V_ROWS:(hd + 1) * A_V_ROWS, :])
        outs.append(_normalized(acc, C_DIM))
        if hd == C_HEADS - 1:
            o_ref[0, qi * tq:(qi + 1) * tq, :] = jnp.concatenate(outs, axis=0).T.astype(BF16)
            outs = []


def _attn_c(qt, mk, mvt):
    b, n_blk, w, tq = qt.shape
    n_mem = mk.shape[1]
    return pl.pallas_call(
        _attn_c_kernel,
        grid=(b,),
        in_specs=[pl.BlockSpec((1, n_blk, w, tq), lambda bi: (bi, 0, 0, 0)),
                  pl.BlockSpec((1, n_mem, w), lambda bi: (bi, 0, 0)),
                  pl.BlockSpec((1, mvt.shape[1], n_mem), lambda bi: (bi, 0, 0))],
        out_specs=pl.BlockSpec((1, n_blk * tq, w), lambda bi: (bi, 0, 0)),
        out_shape=jax.ShapeDtypeStruct((b, n_blk * tq, w), BF16),
        compiler_params=_params(1),
        name="attn_c",
    )(qt, mk, mvt)


def _mix_kernel(x_ref, g_ref, wgate_ref, bgate_ref, oa_ref, ob_ref, oc_ref, wbr_ref, wout_ref,
                o_ref):
    x = x_ref[...]
    d = x.shape[1]
    h = _rms(x, g_ref[...]).astype(BF16)
    y = None
    for n, br_ref in enumerate((oa_ref, ob_ref, oc_ref)):
        z = _dot(h, wgate_ref[:, n * d:(n + 1) * d]) + bgate_ref[:, n * d:(n + 1) * d]
        term = (1.0 / (1.0 + jnp.exp(-z))) * _dot(br_ref[...], wbr_ref[n])
        y = term if y is None else y + term
    o_ref[...] = x + _dot(y.astype(BF16), wout_ref[...])


def _mix_out(x, g, w_gate, b_gate, o_a, o_b, o_c, w_branch, w_out):
    t, d = x.shape
    tm = FFN_TILE
    tile = lambda i: (i, 0)
    br_spec = pl.BlockSpec((tm, BRANCH_W), tile)
    return pl.pallas_call(
        _mix_kernel,
        grid=(t // tm,),
        in_specs=[pl.BlockSpec((tm, d), tile), _const_spec((1, d)), _const_spec(w_gate.shape),
                  _const_spec(b_gate.shape), br_spec, br_spec, br_spec,
                  _const_spec(w_branch.shape), _const_spec(w_out.shape)],
        out_specs=pl.BlockSpec((tm, d), tile),
        out_shape=jax.ShapeDtypeStruct((t, d), F32),
        compiler_params=_params(1),
        name="mix_out",
    )(x, g, w_gate, b_gate, o_a, o_b, o_c, w_branch, w_out)


def _rope_tables(seq):
    half = HEAD_DIM // 2
    inv = ROPE_THETA ** (-jnp.arange(half, dtype=F32) / half)
    ang = jnp.arange(seq, dtype=F32)[:, None] * inv[None, :]
    return jnp.cos(ang).T, jnp.sin(ang).T


def _group_mean_matrix(group):
    idx = jnp.arange(MXU_TILE) // group
    return jnp.where(idx[:, None] == idx[None, :], 1.0 / group, 0.0).astype(BF16)


def _bias_rows(table):
    h = table.shape[0]
    far = jnp.broadcast_to(table[:, -1:], (h, TOKEN_TILE - REL_CLIP))
    near = table[:, ::-1]
    tail = jnp.broadcast_to(table[:, -1:], (h, 2 * TOKEN_TILE - far.shape[1] - near.shape[1]))
    return jnp.concatenate([far, near, tail], axis=1)[:, None, :]


def kernel(x, mem, ffn1_norm, ffn1_wg, ffn1_wu, ffn1_wd, mix_norm, w_in, a_q_norm, a_k_norm,
           a_lambda, a_sub_norm, b_q_norm, b_k_norm, b_rel_bias, mem_norm, w_mem_kv, c_q_norm,
           c_k_norm, w_gate, b_gate, w_branch, w_out, ffn2_norm, ffn2_wg, ffn2_wu, ffn2_wd,
           final_norm):
    b, s, d = x.shape
    n_mem = mem.shape[1]
    depth = ffn1_norm.shape[0]
    tm = TOKEN_TILE
    w = BRANCH_W
    assert s % tm == 0 and d % MXU_TILE == 0 and w_in.shape[2] == 7 * w

    tables = _rope_tables(s)
    g128 = _group_mean_matrix(C_DIM)
    row = lambda v: v.reshape(1, -1)
    heads = lambda v: jnp.tile(v, w // v.shape[0])

    xt = x.reshape(b * s, d)
    memt = mem.reshape(b * n_mem, d)
    for l in range(depth):
        lambda_init = 0.8 - 0.6 * math.exp(-0.3 * l)
        xt, (wg2, wu2, wd2, w_in_b, w_gate_b, w_branch_b, w_out_b, wm) = _ffn(
            xt, row(ffn1_norm[l]), ffn1_wg[l].astype(BF16), ffn1_wu[l].astype(BF16),
            ffn1_wd[l].astype(BF16),
            to_bf16=(ffn2_wg[l], ffn2_wu[l], ffn2_wd[l], w_in[l], w_gate[l],
                     w_branch[l].reshape(-1, d), w_out[l], w_mem_kv[l]))
        w_branch_b = w_branch_b.reshape(w_branch[l].shape)

        col = lambda v: heads(v).reshape(w, 1)
        aqt, avt, bqt, bvt, cqt, ak, bk = _in_proj(
            xt, row(mix_norm[l]), w_in_b.T, tables,
            col(a_q_norm[l]), col(a_k_norm[l]), col(b_q_norm[l]), col(b_k_norm[l]),
            col(c_q_norm[l]), s)
        mk, mvt = _mem_proj(memt, row(mem_norm[l]), wm[:, :w], wm[:, w:].T,
                            row(heads(c_k_norm[l])), g128, n_mem)
        bias = _rel_bias(_bias_rows(b_rel_bias[l]))

        blocks = lambda a: a.reshape(b, s // tm, a.shape[1], tm)
        seq3 = lambda a: a.reshape(b, s, w)
        o_a = _attn_a(blocks(aqt), seq3(ak), blocks(avt), a_lambda[l], row(a_sub_norm[l]),
                      lambda_init)
        o_b = _attn_b(blocks(bqt), seq3(bk), blocks(bvt), bias)
        o_c = _attn_c(blocks(cqt), mk.reshape(b, n_mem, w), mvt)

        flat = lambda a: a.reshape(b * s, w)
        xt = _mix_out(xt, row(mix_norm[l]), w_gate_b, row(b_gate[l]),
                      flat(o_a), flat(o_b), flat(o_c), w_branch_b, w_out_b)

        xt, _ = _ffn(xt, row(ffn2_norm[l]), wg2, wu2, wd2, row(final_norm[l]))
    return xt.reshape(b, s, d)
```

```python
import functools
import math

import jax
import jax.numpy as jnp
from jax import lax
from jax.experimental import pallas as pl
from jax.experimental.pallas import tpu as pltpu

F32 = jnp.float32
BF16 = jnp.bfloat16

CHUNK = 64
HEAD_DIM = 64
A_HEADS = 4
B_HEADS = 8
B_LEFT_CHUNKS = 8
REL_CLIP = 128
C_HEADS = 4
C_DIM = 128
N_BRANCH = 3
BRANCH_W = 512
ROPE_THETA = 10000.0
EPS = 1e-6
NEG = -1e30
LOG2E = math.log2(math.e)

LANES = 128
MXU_TILE = 256
BF16_ROWS = 16
TOKEN_TILE = 512
FFN_TILE = 1024
IN_TILES = 2
KEY_PART = MXU_TILE
LOOKAHEAD = 2
A_KEY_PART = 2 * MXU_TILE
A_LOOKAHEAD = 2
A_QUERY_SUBS = 4
A_CHAIN_SOFTMAXES = 1
B_QUERY_SUBS = 4
B_BAND_PARTS = 3
ONES_ROWS = BF16_ROWS
VMEM_LIMIT = 56 * 1024 * 1024

A_V_ROWS = 2 * HEAD_DIM + ONES_ROWS
B_V_ROWS = HEAD_DIM + ONES_ROWS

assert TOKEN_TILE == B_LEFT_CHUNKS * CHUNK


def _dot(a, b):
    return jnp.dot(a, b, preferred_element_type=F32)


def _dot_nt(a, b):
    return lax.dot_general(a, b, (((1,), (1,)), ((), ())), preferred_element_type=F32)


def _rms(x, g):
    return x * lax.rsqrt(jnp.mean(x * x, axis=-1, keepdims=True) + EPS) * g


def _group_rms(z, gmat_ref, gain):
    z2 = (z * z).astype(BF16)
    ms = jnp.concatenate(
        [_dot(z2[:, :MXU_TILE], gmat_ref[...]), _dot(z2[:, MXU_TILE:], gmat_ref[...])], axis=1)
    return z * lax.rsqrt(ms + EPS) * gain


def _heads_t(z, gain, group, scale, cos=None, sin=None):
    outs = []
    for g0 in range(0, z.shape[0], group):
        zz = z[g0:g0 + group]
        r = lax.rsqrt(jnp.mean(zz * zz, axis=0, keepdims=True) + EPS) * scale
        y = zz * r * gain[g0:g0 + group]
        if cos is not None:
            y1, y2 = y[:group // 2], y[group // 2:]
            y = jnp.concatenate([y1 * cos - y2 * sin, y2 * cos + y1 * sin], axis=0)
        outs.append(y)
    return jnp.concatenate(outs, axis=0)


def _with_ones(vt, head_rows):
    ones = jnp.ones((ONES_ROWS, vt.shape[1]), vt.dtype)
    parts = []
    for r0 in range(0, vt.shape[0], head_rows):
        parts += [vt[r0:r0 + head_rows], ones]
    return jnp.concatenate(parts, axis=0)


def _const_spec(shape):
    nd = len(shape)
    return pl.BlockSpec(shape, lambda *_: (0,) * nd, pipeline_mode=pl.Buffered(1))


def _params(n_axes):
    return pltpu.CompilerParams(
        dimension_semantics=("arbitrary",) * n_axes, vmem_limit_bytes=VMEM_LIMIT)


def _ffn_chunks(d_ff):
    step = 3 * MXU_TILE
    return [(c, min(c + step, d_ff)) for c in range(0, d_ff, step)]


def _ffn_kernel(x_ref, g_ref, wg_ref, wu_ref, wd_ref, *rest, final_norm, n_cast):
    rest = list(rest)
    fg_ref = rest.pop(0) if final_norm else None
    cast_in, o_ref, cast_out, act_ref = (rest[:n_cast], rest[n_cast],
                                         rest[n_cast + 1:2 * n_cast + 1], rest[-1])
    for src, dst in zip(cast_in, cast_out):
        dst[...] = src[...].astype(BF16)
    x = x_ref[...]
    h = _rms(x, g_ref[...]).astype(BF16)
    for c0, c1 in _ffn_chunks(wg_ref.shape[1]):
        g = _dot(h, wg_ref[:, c0:c1])
        u = _dot(h, wu_ref[:, c0:c1])
        act_ref[:, c0:c1] = (g * (1.0 / (1.0 + jnp.exp(-g))) * u).astype(BF16)
    out = x + 0.5 * _dot(act_ref[...], wd_ref[...])
    if final_norm:
        out = _rms(out, fg_ref[...])
    o_ref[...] = out


def _ffn(x, g, wg, wu, wd, final_g=None, to_bf16=()):
    t, d = x.shape
    f = wg.shape[1]
    tm = FFN_TILE
    steps = t // tm
    tile = lambda i: (i, 0)
    in_specs = [pl.BlockSpec((tm, d), tile), _const_spec((1, d)),
                _const_spec((d, f)), _const_spec((d, f)), _const_spec((f, d))]
    args = [x, g, wg, wu, wd]
    if final_g is not None:
        in_specs.append(_const_spec((1, d)))
        args.append(final_g)
    cast_specs = []
    for a in to_bf16:
        n_blocks = math.gcd(steps, a.shape[0] // BF16_ROWS)
        rows, repeat = a.shape[0] // n_blocks, steps // n_blocks
        cast_specs.append(pl.BlockSpec((rows, a.shape[1]), lambda i, r=repeat: (i // r, 0)))
    outs = pl.pallas_call(
        functools.partial(_ffn_kernel, final_norm=final_g is not None, n_cast=len(to_bf16)),
        grid=(steps,),
        in_specs=in_specs + cast_specs,
        out_specs=[pl.BlockSpec((tm, d), tile)] + cast_specs,
        out_shape=([jax.ShapeDtypeStruct((t, d), F32)]
                   + [jax.ShapeDtypeStruct(a.shape, BF16) for a in to_bf16]),
        scratch_shapes=[pltpu.VMEM((tm, f), BF16)],
        compiler_params=_params(1),
        name="ffn_final" if final_g is not None else "ffn",
    )(*args, *to_bf16)
    return outs[0], outs[1:]


def _in_kernel(x_ref, g_ref, wt_ref, cost_ref, sint_ref,
               gaq_ref, gak_ref, gbq_ref, gbk_ref, gcq_ref,
               aqt_ref, avt_ref, bqt_ref, bvt_ref, cqt_ref, ak_ref, bk_ref):
    w = BRANCH_W
    tm = TOKEN_TILE
    qk_scale = HEAD_DIM ** -0.5 * LOG2E
    for sub in range(IN_TILES):
        rows = slice(sub * tm, (sub + 1) * tm)
        h = _rms(x_ref[rows, :], g_ref[...]).astype(BF16)
        cos, sin = cost_ref[:, rows], sint_ref[:, rows]

        def seg_t(i):
            return _dot_nt(wt_ref[i * w:(i + 1) * w, :], h)

        aqt_ref[sub] = _heads_t(seg_t(0), gaq_ref[...], HEAD_DIM, qk_scale, cos, sin
                                ).astype(BF16)
        ak_ref[rows, :] = _heads_t(seg_t(1), gak_ref[...], HEAD_DIM, 1.0, cos, sin
                                   ).T.astype(BF16)
        avt_ref[sub] = _with_ones(seg_t(2).astype(BF16), 2 * HEAD_DIM)
        bqt_ref[sub] = _heads_t(seg_t(3), gbq_ref[...], HEAD_DIM, qk_scale).astype(BF16)
        bk_ref[rows, :] = _heads_t(seg_t(4), gbk_ref[...], HEAD_DIM, 1.0).T.astype(BF16)
        bvt_ref[sub] = _with_ones(seg_t(5).astype(BF16), HEAD_DIM)
        cqt_ref[sub] = _heads_t(seg_t(6), gcq_ref[...], C_DIM, C_DIM ** -0.5 * LOG2E
                                ).astype(BF16)


def _in_proj(x, g, w_t, tables, gaq, gak, gbq, gbk, gcq, seq):
    t, d = x.shape
    tm = TOKEN_TILE
    ts = IN_TILES * tm
    n_tiles = t // tm
    n_seq_steps = seq // ts
    w = BRANCH_W
    cos_t, sin_t = tables
    tile = lambda i: (i, 0)
    half = HEAD_DIM // 2
    in_specs = [pl.BlockSpec((ts, d), tile), _const_spec((1, d)), _const_spec(w_t.shape),
                pl.BlockSpec((half, ts), lambda i: (0, i % n_seq_steps)),
                pl.BlockSpec((half, ts), lambda i: (0, i % n_seq_steps))]
    in_specs += [_const_spec((w, 1))] * 5
    t_rows = [w, A_HEADS * A_V_ROWS, w, B_HEADS * B_V_ROWS, w]
    return pl.pallas_call(
        _in_kernel,
        grid=(t // ts,),
        in_specs=in_specs,
        out_specs=([pl.BlockSpec((IN_TILES, r, tm), lambda i: (i, 0, 0)) for r in t_rows]
                   + [pl.BlockSpec((ts, w), tile)] * 2),
        out_shape=([jax.ShapeDtypeStruct((n_tiles, r, tm), BF16) for r in t_rows]
                   + [jax.ShapeDtypeStruct((t, w), BF16)] * 2),
        compiler_params=_params(1),
        name="in_proj",
    )(x, g, w_t, cos_t, sin_t, gaq, gak, gbq, gbk, gcq)


def _mem_kernel(m_ref, g_ref, wk_ref, wvt_ref, gk_ref, g128_ref, mk_ref, mvt_ref):
    h = _rms(m_ref[...], g_ref[...]).astype(BF16)
    mk_ref[...] = _group_rms(_dot(h, wk_ref[...]), g128_ref, gk_ref[...]).astype(BF16)
    mvt_ref[0] = _with_ones(_dot_nt(wvt_ref[...], h).astype(BF16), C_DIM)


def _mem_proj(mem, g, w_k, w_vt, gk, g128, n_mem):
    t, d = mem.shape
    w = BRANCH_W
    v_rows = C_HEADS * A_V_ROWS
    tile = lambda i: (i, 0)
    return pl.pallas_call(
        _mem_kernel,
        grid=(t // n_mem,),
        in_specs=[pl.BlockSpec((n_mem, d), tile), _const_spec((1, d)), _const_spec(w_k.shape),
                  _const_spec(w_vt.shape), _const_spec((1, w)),
                  _const_spec((MXU_TILE, MXU_TILE))],
        out_specs=[pl.BlockSpec((n_mem, w), tile),
                   pl.BlockSpec((1, v_rows, n_mem), lambda i: (i, 0, 0))],
        out_shape=[jax.ShapeDtypeStruct((t, w), BF16),
                   jax.ShapeDtypeStruct((t // n_mem, v_rows, n_mem), BF16)],
        compiler_params=_params(1),
        name="mem_proj",
    )(mem, g, w_k, w_vt, gk, g128)


def _bias_kernel(r_ref, o_ref):
    nq = TOKEN_TILE
    nk = 2 * nq
    hk = KEY_PART
    qc = lax.broadcasted_iota(jnp.int32, (nq, nk), 0) // CHUNK + B_LEFT_CHUNKS
    kc = lax.broadcasted_iota(jnp.int32, (nq, nk), 1) // CHUNK
    valid = (qc >= kc) & (qc - kc <= B_LEFT_CHUNKS)
    for hh in range(2):
        t = pltpu.roll(jnp.broadcast_to(r_ref[hh], (nq, nk)), 0, 1, stride=1, stride_axis=0)
        t = jnp.where(valid, t * LOG2E, NEG).T
        o_ref[0, :, hh * hk:(hh + 1) * hk] = t[:B_BAND_PARTS * hk, :hk]


def _rel_bias(rows):
    h, _, nk = rows.shape
    tile = (B_BAND_PARTS * KEY_PART, 2 * KEY_PART)
    return pl.pallas_call(
        _bias_kernel,
        grid=(h // 2,),
        in_specs=[pl.BlockSpec((2, 1, nk), lambda i: (i, 0, 0))],
        out_specs=pl.BlockSpec((1,) + tile, lambda i: (i, 0, 0)),
        out_shape=jax.ShapeDtypeStruct((h // 2,) + tile, F32),
        compiler_params=_params(1),
        name="rel_bias",
    )(rows)


def _split_queries(qt):
    row = lax.broadcasted_iota(jnp.int32, qt.shape, 0)
    zero = jnp.zeros_like(qt)
    return jnp.concatenate(
        [jnp.where(row < HEAD_DIM, qt, zero), jnp.where(row >= HEAD_DIM, qt, zero)], axis=1)


def _rows(start, size):
    if isinstance(start, int):
        return pl.ds(start, size)
    return pl.ds(pl.multiple_of(start, size), size)


def _column_max(s):
    return jnp.max(s, axis=0, keepdims=True)


def _online_softmax(s, m_prev, acc_prev, vt, s_max=None):
    if s_max is None:
        s_max = _column_max(s)
    m_new = s_max if m_prev is None else jnp.maximum(m_prev, s_max)
    pv = _dot(vt, jnp.exp2(s - m_new).astype(BF16))
    if acc_prev is None:
        return m_new, pv
    return m_new, jnp.exp2(m_prev - m_new) * acc_prev + pv


def _normalized(acc, rows):
    return acc[:rows] / acc[rows:rows + 1]


def _attn_a_kernel(qt_ref, k_ref, vt_ref, lam_ref, gain_ref, o_ref, q12_ref, *scratch,
                   lambda_init):
    n_sub, _, ts = qt_ref.shape[1:]
    subs = A_QUERY_SUBS
    n_blk = n_sub // subs
    per = A_CHAIN_SOFTMAXES
    cw = per * ts
    n_chains = 2 * subs // per
    look = A_LOOKAHEAD
    score_refs = scratch[:look]
    smax_refs = scratch[look:2 * look]
    state_refs = [scratch[2 * look + 2 * c:2 * look + 2 * c + 2] for c in range(n_chains)]
    hk = A_KEY_PART
    parts = ts // hk
    assert per == 1
    tail_base = n_chains * cw
    for blk in range(n_blk):
        q12 = [_split_queries(qt_ref[0, blk * subs + h]) for h in range(subs)]
        th = ts // 2
        tails = [jnp.concatenate([q[:, ts - th:ts], q[:, 2 * ts - th:]], axis=1) for q in q12]
        q12_ref[blk] = jnp.concatenate(q12 + tails, axis=1)
    lp = lam_ref[...]
    lam = (jnp.exp(jnp.sum(lp[0:1] * lp[1:2], axis=1, keepdims=True))
           - jnp.exp(jnp.sum(lp[2:3] * lp[3:4], axis=1, keepdims=True)) + lambda_init)


    def qk(item):
        qi, jb, kp, ch, _, tail = item
        c0 = tail_base + ch * ts if tail else ch * cw
        kb = k_ref[0, _rows(jb * ts + kp * hk, hk), :]
        s = _dot(kb, q12_ref[qi, :, c0:c0 + cw])
        return s, _column_max(s)

    chains_per_sub = 2 // per

    def reset(h):
        for m_ref, acc_ref in state_refs[chains_per_sub * h:chains_per_sub * (h + 1)]:
            m_ref[...] = jnp.full(m_ref.shape, NEG, F32)
            acc_ref[...] = jnp.zeros(acc_ref.shape, F32)

    def finish(qi, h):
        o12 = jnp.concatenate(
            [_normalized(acc_ref[...], 2 * HEAD_DIM)
             for _, acc_ref in state_refs[chains_per_sub * h:chains_per_sub * (h + 1)]], axis=1)
        o1, o2 = o12[:, :ts], o12[:, ts:]
        ot = o1 - lam * o2
        ot = ot * lax.rsqrt(jnp.mean(ot * ot, axis=0, keepdims=True) + EPS)
        o_ref[0, _rows((qi * subs + h) * ts, ts), :] = (
            ot.T * gain_ref[...] * (1.0 - lambda_init)).astype(BF16)
        reset(h)

    def run(seq, following, after=()):
        scores = [(score_refs[i][...], smax_refs[i][...]) for i in range(look)]
        for i, (qi, jb, kp, ch, masked, tail) in enumerate(seq):
            s, s_max = scores.pop(0)
            if masked:
                key_chunk = (lax.broadcasted_iota(jnp.int32, (hk, cw), 0) + kp * hk) // CHUNK
                query = lax.broadcasted_iota(jnp.int32, (hk, cw), 1)
                query_chunk = (query % hk + ts - hk if tail else query) // CHUNK
                s = jnp.where(key_chunk <= query_chunk, s, NEG)
                s_max = None
            nxt = i + look
            if nxt < len(seq):
                scores.append(qk(seq[nxt]))
            else:
                n = nxt - len(seq)
                score_refs[n][...], smax_refs[n][...] = qk(following[n])
            vt = vt_ref[0, jb, :, kp * hk:(kp + 1) * hk]
            if tail:
                refs = state_refs[2 * ch:2 * ch + 2]
                m_new, acc_new = _online_softmax(
                    s, jnp.concatenate([m_ref[:, ts - hk:] for m_ref, _ in refs], axis=1),
                    jnp.concatenate([acc_ref[:, ts - hk:] for _, acc_ref in refs], axis=1),
                    vt, s_max)
                for c, (m_ref, acc_ref) in enumerate(refs):
                    m_ref[:, ts - hk:] = m_new[:, c * hk:(c + 1) * hk]
                    acc_ref[:, ts - hk:] = acc_new[:, c * hk:(c + 1) * hk]
            else:
                m_ref, acc_ref = state_refs[ch]
                m_ref[...], acc_ref[...] = _online_softmax(s, m_ref[...], acc_ref[...], vt, s_max)
            if i in after:
                finish(qi, after[i])

    def key_block(qi, jb):
        return [(qi, jb, kp, ch, False, False) for kp in range(parts) for ch in range(n_chains)]

    def own_blocks(qi):
        seq, after = [], {}
        for kb in range(subs):
            for kp in range(parts):
                for h in range(kb, subs):
                    if h == kb and kp == parts - 1 and parts == 2:
                        seq.append((qi, subs * qi + kb, kp, h, True, True))
                        after[len(seq) - 1] = kb
                    else:
                        seq += [(qi, subs * qi + kb, kp, 2 * h + c, h == kb, False)
                                for c in range(2)]
                        if h == kb and kp == parts - 1:
                            after[len(seq) - 1] = kb
        return seq, after

    def query_block(qi, carry):
        def body(jb, c):
            run(key_block(qi, jb), key_block(qi, jb + 1))
            return c

        lax.fori_loop(0, subs * qi, body, 0)
        seq, after = own_blocks(qi)
        nxt_qi = jnp.minimum(qi + 1, n_blk - 1)
        run(seq, key_block(nxt_qi, 0), after)
        return carry

    for h in range(subs):
        reset(h)
    for i, item in enumerate(key_block(0, 0)[:look]):
        score_refs[i][...], smax_refs[i][...] = qk(item)
    if n_blk == 1:
        query_block(0, 0)
    else:
        lax.fori_loop(0, n_blk, query_block, 0)


def _attn_a(qt, k, vt, lam_p, gain, lambda_init):
    b, n_sub, w, ts = qt.shape
    s = n_sub * ts
    n_chains = 2 * A_QUERY_SUBS // A_CHAIN_SOFTMAXES
    cw = A_CHAIN_SOFTMAXES * ts
    return pl.pallas_call(
        functools.partial(_attn_a_kernel, lambda_init=lambda_init),
        grid=(b, w // LANES),
        in_specs=[pl.BlockSpec((1, n_sub, LANES, ts), lambda bi, hi: (bi, 0, hi, 0)),
                  pl.BlockSpec((1, s, LANES), lambda bi, hi: (bi, 0, hi)),
                  pl.BlockSpec((1, n_sub, A_V_ROWS, ts), lambda bi, hi: (bi, 0, hi, 0)),
                  _const_spec(lam_p.shape), _const_spec((1, LANES))],
        out_specs=pl.BlockSpec((1, s, LANES), lambda bi, hi: (bi, 0, hi)),
        out_shape=jax.ShapeDtypeStruct((b, s, w), BF16),
        scratch_shapes=(
            [pltpu.VMEM((n_sub // A_QUERY_SUBS, LANES, n_chains * cw + A_QUERY_SUBS * ts), BF16)]
            + [pltpu.VMEM((A_KEY_PART, cw), F32)] * A_LOOKAHEAD
            + [pltpu.VMEM((1, cw), F32)] * A_LOOKAHEAD
            + [pltpu.VMEM((1, cw), F32), pltpu.VMEM((A_V_ROWS, cw), F32)] * n_chains),
        compiler_params=_params(2),
        name="attn_a",
    )(qt, k, vt, lam_p, gain)


def _attn_b_kernel(qt_ref, k_ref, vt_ref, bias_ref, o_ref, q12_ref, *score_refs):
    n_sub, _, ts = qt_ref.shape[1:]
    hk = KEY_PART
    per_sub = ts // hk
    n_chains = B_QUERY_SUBS * per_sub
    n_blk = n_sub // B_QUERY_SUBS
    lead = B_BAND_PARTS - 1
    assert lead * hk == B_LEFT_CHUNKS * CHUNK and LOOKAHEAD <= n_chains
    for blk in range(n_blk):
        cols = []
        for c in range(n_chains):
            q12 = _split_queries(qt_ref[0, blk * B_QUERY_SUBS + c // per_sub])
            c0 = (c % per_sub) * hk
            cols += [q12[:, c0:c0 + hk], q12[:, ts + c0:ts + c0 + hk]]
        q12_ref[blk] = jnp.concatenate(cols, axis=1)

    items = [(c + d, c) for d in range(lead, -1, -1) for c in range(n_chains)]

    def source(qi, wp):
        part = qi * n_chains + wp - lead
        return jnp.maximum(part, 0) // per_sub, (wp - lead) % per_sub * hk

    def qk(qi, item):
        wp, c = item
        blk, r0 = source(qi, wp)
        cols = slice(c * 2 * hk, (c + 1) * 2 * hk)
        d = wp - c
        kb = k_ref[0, pl.ds(pl.multiple_of(blk * ts + r0, hk), hk), :]
        s = _dot(kb, q12_ref[qi, :, cols]) + bias_ref[0, d * hk:(d + 1) * hk, :]
        if wp < lead:
            s = jnp.where(qi > 0, s, NEG)
        return s

    def query_block(qi, carry):
        scores = [r[...] for r in score_refs]
        state = [(None, None)] * n_chains
        for i, (wp, c) in enumerate(items):
            s = scores.pop(0)
            nxt = i + LOOKAHEAD
            if nxt < len(items):
                scores.append(qk(qi, items[nxt]))
            else:
                score_refs[nxt - len(items)][...] = qk(
                    jnp.minimum(qi + 1, n_blk - 1), items[nxt - len(items)])
            blk, r0 = source(qi, wp)
            state[c] = _online_softmax(s, state[c][0], state[c][1],
                                       vt_ref[0, blk, :, r0:r0 + hk])
        outs = []
        for _, acc in state:
            heads = [_normalized(acc[hh * B_V_ROWS:(hh + 1) * B_V_ROWS,
                                     hh * hk:(hh + 1) * hk], HEAD_DIM) for hh in range(2)]
            outs.append(jnp.concatenate(heads, axis=0))
        for h in range(B_QUERY_SUBS):
            ot = jnp.concatenate(outs[h * per_sub:(h + 1) * per_sub], axis=1)
            o_ref[0, pl.ds(pl.multiple_of((qi * B_QUERY_SUBS + h) * ts, ts), ts), :] = (
                ot.T.astype(BF16))
        return carry

    for i, r in enumerate(score_refs):
        r[...] = qk(0, items[i])
    lax.fori_loop(0, n_blk, query_block, 0)


def _attn_b(qt, k, vt, bias):
    b, n_sub, w, ts = qt.shape
    s = n_sub * ts
    return pl.pallas_call(
        _attn_b_kernel,
        grid=(w // LANES, b),
        in_specs=[pl.BlockSpec((1, n_sub, LANES, ts), lambda pi, bi: (bi, 0, pi, 0)),
                  pl.BlockSpec((1, s, LANES), lambda pi, bi: (bi, 0, pi)),
                  pl.BlockSpec((1, n_sub, 2 * B_V_ROWS, ts), lambda pi, bi: (bi, 0, pi, 0)),
                  pl.BlockSpec((1,) + bias.shape[1:], lambda pi, bi: (pi, 0, 0))],
        out_specs=pl.BlockSpec((1, s, LANES), lambda pi, bi: (bi, 0, pi)),
        out_shape=jax.ShapeDtypeStruct((b, s, w), BF16),
        scratch_shapes=([pltpu.VMEM((n_sub // B_QUERY_SUBS, LANES, 2 * B_QUERY_SUBS * ts), BF16)]
                        + [pltpu.VMEM((KEY_PART, 2 * KEY_PART), F32)] * LOOKAHEAD),
        compiler_params=_params(2),
        name="attn_b",
    )(qt, k, vt, bias)


def _attn_c_kernel(qt_ref, k_ref, vt_ref, o_ref):
    n_blk, _, tq = qt_ref.shape[1:]
    items = [(qi, hd) for qi in range(n_blk) for hd in range(C_HEADS)]

    def qk(item):
        qi, hd = item
        rows = slice(hd * C_DIM, (hd + 1) * C_DIM)
        return _dot(k_ref[0, :, rows], qt_ref[0, qi, rows, :])

    scores = [qk(it) for it in items[:LOOKAHEAD]]
    outs = []
    for i, (qi, hd) in enumerate(items):
        s = scores.pop(0)
        if i + LOOKAHEAD < len(items):
            scores.append(qk(items[i + LOOKAHEAD]))
        _, acc = _online_softmax(s, None, None, vt_ref[0, hd * A_V_ROWS:(hd + 1) * A_V_ROWS, :])
        outs.append(_normalized(acc, C_DIM))
        if hd == C_HEADS - 1:
            o_ref[0, qi * tq:(qi + 1) * tq, :] = jnp.concatenate(outs, axis=0).T.astype(BF16)
            outs = []


def _attn_c(qt, mk, mvt):
    b, n_blk, w, tq = qt.shape
    n_mem = mk.shape[1]
    return pl.pallas_call(
        _attn_c_kernel,
        grid=(b,),
        in_specs=[pl.BlockSpec((1, n_blk, w, tq), lambda bi: (bi, 0, 0, 0)),
                  pl.BlockSpec((1, n_mem, w), lambda bi: (bi, 0, 0)),
                  pl.BlockSpec((1, mvt.shape[1], n_mem), lambda bi: (bi, 0, 0))],
        out_specs=pl.BlockSpec((1, n_blk * tq, w), lambda bi: (bi, 0, 0)),
        out_shape=jax.ShapeDtypeStruct((b, n_blk * tq, w), BF16),
        compiler_params=_params(1),
        name="attn_c",
    )(qt, mk, mvt)


def _mix_kernel(x_ref, g_ref, wgate_ref, bgate_ref, oa_ref, ob_ref, oc_ref, wbr_ref, wout_ref,
                o_ref):
    x = x_ref[...]
    d = x.shape[1]
    h = _rms(x, g_ref[...]).astype(BF16)
    y = None
    for n, br_ref in enumerate((oa_ref, ob_ref, oc_ref)):
        z = _dot(h, wgate_ref[:, n * d:(n + 1) * d]) + bgate_ref[:, n * d:(n + 1) * d]
        term = (1.0 / (1.0 + jnp.exp(-z))) * _dot(br_ref[...], wbr_ref[n])
        y = term if y is None else y + term
    o_ref[...] = x + _dot(y.astype(BF16), wout_ref[...])


def _mix_out(x, g, w_gate, b_gate, o_a, o_b, o_c, w_branch, w_out):
    t, d = x.shape
    tm = FFN_TILE
    tile = lambda i: (i, 0)
    br_spec = pl.BlockSpec((tm, BRANCH_W), tile)
    return pl.pallas_call(
        _mix_kernel,
        grid=(t // tm,),
        in_specs=[pl.BlockSpec((tm, d), tile), _const_spec((1, d)), _const_spec(w_gate.shape),
                  _const_spec(b_gate.shape), br_spec, br_spec, br_spec,
                  _const_spec(w_branch.shape), _const_spec(w_out.shape)],
        out_specs=pl.BlockSpec((tm, d), tile),
        out_shape=jax.ShapeDtypeStruct((t, d), F32),
        compiler_params=_params(1),
        name="mix_out",
    )(x, g, w_gate, b_gate, o_a, o_b, o_c, w_branch, w_out)


def _rope_tables(seq):
    half = HEAD_DIM // 2
    inv = ROPE_THETA ** (-jnp.arange(half, dtype=F32) / half)
    ang = jnp.arange(seq, dtype=F32)[:, None] * inv[None, :]
    return jnp.cos(ang).T, jnp.sin(ang).T


def _group_mean_matrix(group):
    idx = jnp.arange(MXU_TILE) // group
    return jnp.where(idx[:, None] == idx[None, :], 1.0 / group, 0.0).astype(BF16)


def _bias_rows(table):
    h = table.shape[0]
    far = jnp.broadcast_to(table[:, -1:], (h, TOKEN_TILE - REL_CLIP))
    near = table[:, ::-1]
    tail = jnp.broadcast_to(table[:, -1:], (h, 2 * TOKEN_TILE - far.shape[1] - near.shape[1]))
    return jnp.concatenate([far, near, tail], axis=1)[:, None, :]


def kernel(x, mem, ffn1_norm, ffn1_wg, ffn1_wu, ffn1_wd, mix_norm, w_in, a_q_norm, a_k_norm,
           a_lambda, a_sub_norm, b_q_norm, b_k_norm, b_rel_bias, mem_norm, w_mem_kv, c_q_norm,
           c_k_norm, w_gate, b_gate, w_branch, w_out, ffn2_norm, ffn2_wg, ffn2_wu, ffn2_wd,
           final_norm):
    b, s, d = x.shape
    n_mem = mem.shape[1]
    depth = ffn1_norm.shape[0]
    tm = TOKEN_TILE
    w = BRANCH_W
    assert s % tm == 0 and d % MXU_TILE == 0 and w_in.shape[2] == 7 * w

    tables = _rope_tables(s)
    g128 = _group_mean_matrix(C_DIM)
    row = lambda v: v.reshape(1, -1)
    heads = lambda v: jnp.tile(v, w // v.shape[0])

    xt = x.reshape(b * s, d)
    memt = mem.reshape(b * n_mem, d)
    for l in range(depth):
        lambda_init = 0.8 - 0.6 * math.exp(-0.3 * l)
        xt, (wg2, wu2, wd2, w_in_b, w_gate_b, w_branch_b, w_out_b, wm) = _ffn(
            xt, row(ffn1_norm[l]), ffn1_wg[l].astype(BF16), ffn1_wu[l].astype(BF16),
            ffn1_wd[l].astype(BF16),
            to_bf16=(ffn2_wg[l], ffn2_wu[l], ffn2_wd[l], w_in[l], w_gate[l],
                     w_branch[l].reshape(-1, d), w_out[l], w_mem_kv[l]))
        w_branch_b = w_branch_b.reshape(w_branch[l].shape)

        col = lambda v: heads(v).reshape(w, 1)
        aqt, avt, bqt, bvt, cqt, ak, bk = _in_proj(
            xt, row(mix_norm[l]), w_in_b.T, tables,
            col(a_q_norm[l]), col(a_k_norm[l]), col(b_q_norm[l]), col(b_k_norm[l]),
            col(c_q_norm[l]), s)
        mk, mvt = _mem_proj(memt, row(mem_norm[l]), wm[:, :w], wm[:, w:].T,
                            row(heads(c_k_norm[l])), g128, n_mem)
        bias = _rel_bias(_bias_rows(b_rel_bias[l]))

        blocks = lambda a: a.reshape(b, s // tm, a.shape[1], tm)
        seq3 = lambda a: a.reshape(b, s, w)
        o_a = _attn_a(blocks(aqt), seq3(ak), blocks(avt), a_lambda[l], row(a_sub_norm[l]),
                      lambda_init)
        o_b = _attn_b(blocks(bqt), seq3(bk), blocks(bvt), bias)
        o_c = _attn_c(blocks(cqt), mk.reshape(b, n_mem, w), mvt)

        flat = lambda a: a.reshape(b * s, w)
        xt = _mix_out(xt, row(mix_norm[l]), w_gate_b, row(b_gate[l]),
                      flat(o_a), flat(o_b), flat(o_c), w_branch_b, w_out_b)

        xt, _ = _ffn(xt, row(ffn2_norm[l]), wg2, wu2, wd2, row(final_norm[l]))
    return xt.reshape(b, s, d)
```

```python
import functools
import math

import jax
import jax.numpy as jnp
from jax import lax
from jax.experimental import pallas as pl
from jax.experimental.pallas import tpu as pltpu

F32 = jnp.float32
BF16 = jnp.bfloat16

CHUNK = 64
HEAD_DIM = 64
A_HEADS = 4
B_HEADS = 8
B_LEFT_CHUNKS = 8
REL_CLIP = 128
C_HEADS = 4
C_DIM = 128
N_BRANCH = 3
BRANCH_W = 512
ROPE_THETA = 10000.0
EPS = 1e-6
NEG = -1e30
LOG2E = math.log2(math.e)

LANES = 128
MXU_TILE = 256
BF16_ROWS = 16
TOKEN_TILE = 512
FFN_TILE = 1024
IN_TILES = 2
KEY_PART = MXU_TILE
LOOKAHEAD = 2
A_KEY_PART = MXU_TILE
A_LOOKAHEAD = 2
A_QUERY_SUBS = 4
A_CHAIN_SOFTMAXES = 1
B_QUERY_SUBS = 4
B_BAND_PARTS = 3
ONES_ROWS = BF16_ROWS
VMEM_LIMIT = 56 * 1024 * 1024

A_V_ROWS = 2 * HEAD_DIM + ONES_ROWS
B_V_ROWS = 2 * HEAD_DIM + ONES_ROWS

assert TOKEN_TILE == B_LEFT_CHUNKS * CHUNK


def _dot(a, b):
    return jnp.dot(a, b, preferred_element_type=F32)


def _dot_nt(a, b):
    return lax.dot_general(a, b, (((1,), (1,)), ((), ())), preferred_element_type=F32)


def _rms(x, g):
    return x * lax.rsqrt(jnp.mean(x * x, axis=-1, keepdims=True) + EPS) * g


def _group_rms(z, gmat_ref, gain):
    z2 = (z * z).astype(BF16)
    ms = jnp.concatenate(
        [_dot(z2[:, :MXU_TILE], gmat_ref[...]), _dot(z2[:, MXU_TILE:], gmat_ref[...])], axis=1)
    return z * lax.rsqrt(ms + EPS) * gain


def _heads_t(z, gain, group, scale, cos=None, sin=None):
    outs = []
    for g0 in range(0, z.shape[0], group):
        zz = z[g0:g0 + group]
        r = lax.rsqrt(jnp.mean(zz * zz, axis=0, keepdims=True) + EPS) * scale
        y = zz * r * gain[g0:g0 + group]
        if cos is not None:
            y1, y2 = y[:group // 2], y[group // 2:]
            y = jnp.concatenate([y1 * cos - y2 * sin, y2 * cos + y1 * sin], axis=0)
        outs.append(y)
    return jnp.concatenate(outs, axis=0)


def _with_ones(vt, head_rows):
    ones = jnp.ones((ONES_ROWS, vt.shape[1]), vt.dtype)
    parts = []
    for r0 in range(0, vt.shape[0], head_rows):
        parts += [vt[r0:r0 + head_rows], ones]
    return jnp.concatenate(parts, axis=0)


def _const_spec(shape):
    nd = len(shape)
    return pl.BlockSpec(shape, lambda *_: (0,) * nd, pipeline_mode=pl.Buffered(1))


def _cast_specs(arrays, steps):
    specs = []
    for a in arrays:
        n_blocks = math.gcd(steps, a.shape[0] // BF16_ROWS)
        rows, repeat = a.shape[0] // n_blocks, steps // n_blocks
        specs.append(pl.BlockSpec((rows, a.shape[1]), lambda i, r=repeat: (i // r, 0)))
    return specs


def _cast_side_job(srcs, dsts):
    for src, dst in zip(srcs, dsts):
        dst[...] = src[...].astype(BF16)


def _params(n_axes):
    return pltpu.CompilerParams(
        dimension_semantics=("arbitrary",) * n_axes, vmem_limit_bytes=VMEM_LIMIT)


def _ffn_chunks(d_ff):
    step = 3 * MXU_TILE
    return [(c, min(c + step, d_ff)) for c in range(0, d_ff, step)]


def _ffn_kernel(x_ref, g_ref, wg_ref, wu_ref, wd_ref, *rest, final_norm, n_cast):
    rest = list(rest)
    fg_ref = rest.pop(0) if final_norm else None
    cast_in, o_ref, cast_out, act_ref = (rest[:n_cast], rest[n_cast],
                                         rest[n_cast + 1:2 * n_cast + 1], rest[-1])
    _cast_side_job(cast_in, cast_out)
    x = x_ref[...]
    h = _rms(x, g_ref[...]).astype(BF16)
    for c0, c1 in _ffn_chunks(wg_ref.shape[1]):
        g = _dot(h, wg_ref[:, c0:c1])
        u = _dot(h, wu_ref[:, c0:c1])
        act_ref[:, c0:c1] = (g * (1.0 / (1.0 + jnp.exp(-g))) * u).astype(BF16)
    out = x + 0.5 * _dot(act_ref[...], wd_ref[...])
    if final_norm:
        out = _rms(out, fg_ref[...])
    o_ref[...] = out


def _ffn(x, g, wg, wu, wd, final_g=None, to_bf16=()):
    t, d = x.shape
    f = wg.shape[1]
    tm = FFN_TILE
    steps = t // tm
    tile = lambda i: (i, 0)
    in_specs = [pl.BlockSpec((tm, d), tile), _const_spec((1, d)),
                _const_spec((d, f)), _const_spec((d, f)), _const_spec((f, d))]
    args = [x, g, wg, wu, wd]
    if final_g is not None:
        in_specs.append(_const_spec((1, d)))
        args.append(final_g)
    cast_specs = _cast_specs(to_bf16, steps)
    outs = pl.pallas_call(
        functools.partial(_ffn_kernel, final_norm=final_g is not None, n_cast=len(to_bf16)),
        grid=(steps,),
        in_specs=in_specs + cast_specs,
        out_specs=[pl.BlockSpec((tm, d), tile)] + cast_specs,
        out_shape=([jax.ShapeDtypeStruct((t, d), F32)]
                   + [jax.ShapeDtypeStruct(a.shape, BF16) for a in to_bf16]),
        scratch_shapes=[pltpu.VMEM((tm, f), BF16)],
        compiler_params=_params(1),
        name="ffn_final" if final_g is not None else "ffn",
    )(*args, *to_bf16)
    return outs[0], outs[1:]


def _in_kernel(x_ref, g_ref, wt_ref, cost_ref, sint_ref,
               gaq_ref, gak_ref, gbq_ref, gbk_ref, gcq_ref,
               aqt_ref, avt_ref, bqt_ref, bvt_ref, cqt_ref, ak_ref, bk_ref):
    w = BRANCH_W
    tm = TOKEN_TILE
    qk_scale = HEAD_DIM ** -0.5 * LOG2E
    for sub in range(IN_TILES):
        rows = slice(sub * tm, (sub + 1) * tm)
        h = _rms(x_ref[rows, :], g_ref[...]).astype(BF16)
        cos, sin = cost_ref[:, rows], sint_ref[:, rows]

        def seg_t(i):
            return _dot_nt(wt_ref[i * w:(i + 1) * w, :], h)

        aqt_ref[sub] = _heads_t(seg_t(0), gaq_ref[...], HEAD_DIM, qk_scale, cos, sin
                                ).astype(BF16)
        ak_ref[rows, :] = _heads_t(seg_t(1), gak_ref[...], HEAD_DIM, 1.0, cos, sin
                                   ).T.astype(BF16)
        avt_ref[sub] = _with_ones(seg_t(2).astype(BF16), 2 * HEAD_DIM)
        bqt_ref[sub] = _heads_t(seg_t(3), gbq_ref[...], HEAD_DIM, qk_scale).astype(BF16)
        bk_ref[rows, :] = _heads_t(seg_t(4), gbk_ref[...], HEAD_DIM, 1.0).T.astype(BF16)
        bvt_ref[sub] = _with_ones(seg_t(5).astype(BF16), 2 * HEAD_DIM)
        cqt_ref[sub] = _heads_t(seg_t(6), gcq_ref[...], C_DIM, C_DIM ** -0.5 * LOG2E
                                ).astype(BF16)


def _in_proj(x, g, w_t, tables, gaq, gak, gbq, gbk, gcq, seq):
    t, d = x.shape
    tm = TOKEN_TILE
    ts = IN_TILES * tm
    n_tiles = t // tm
    n_seq_steps = seq // ts
    w = BRANCH_W
    cos_t, sin_t = tables
    tile = lambda i: (i, 0)
    half = HEAD_DIM // 2
    in_specs = [pl.BlockSpec((ts, d), tile), _const_spec((1, d)), _const_spec(w_t.shape),
                pl.BlockSpec((half, ts), lambda i: (0, i % n_seq_steps)),
                pl.BlockSpec((half, ts), lambda i: (0, i % n_seq_steps))]
    in_specs += [_const_spec((w, 1))] * 5
    t_rows = [w, A_HEADS * A_V_ROWS, w, B_HEADS // 2 * B_V_ROWS, w]
    return pl.pallas_call(
        _in_kernel,
        grid=(t // ts,),
        in_specs=in_specs,
        out_specs=([pl.BlockSpec((IN_TILES, r, tm), lambda i: (i, 0, 0)) for r in t_rows]
                   + [pl.BlockSpec((ts, w), tile)] * 2),
        out_shape=([jax.ShapeDtypeStruct((n_tiles, r, tm), BF16) for r in t_rows]
                   + [jax.ShapeDtypeStruct((t, w), BF16)] * 2),
        compiler_params=_params(1),
        name="in_proj",
    )(x, g, w_t, cos_t, sin_t, gaq, gak, gbq, gbk, gcq)


def _mem_kernel(m_ref, g_ref, wkv_ref, gk_ref, g128_ref, *rest, n_cast):
    cast_in, (mk_ref, mvt_ref), cast_out = rest[:n_cast], rest[n_cast:n_cast + 2], rest[n_cast + 2:]
    _cast_side_job(cast_in, cast_out)
    w = BRANCH_W
    h = _rms(m_ref[...], g_ref[...]).astype(BF16)
    wkv = wkv_ref[...].astype(BF16)
    mk_ref[...] = _group_rms(_dot(h, wkv[:, :w]), g128_ref, gk_ref[...]).astype(BF16)
    mvt_ref[0] = _with_ones(_dot(h, wkv[:, w:]).T.astype(BF16), C_DIM)


def _mem_proj(mem, g, w_kv, gk, g128, n_mem, to_bf16=()):
    t, d = mem.shape
    w = BRANCH_W
    v_rows = C_HEADS * A_V_ROWS
    steps = t // n_mem
    tile = lambda i: (i, 0)
    cast_specs = _cast_specs(to_bf16, steps)
    outs = pl.pallas_call(
        functools.partial(_mem_kernel, n_cast=len(to_bf16)),
        grid=(steps,),
        in_specs=[pl.BlockSpec((n_mem, d), tile), _const_spec((1, d)), _const_spec(w_kv.shape),
                  _const_spec((1, w)), _const_spec((MXU_TILE, MXU_TILE))] + cast_specs,
        out_specs=[pl.BlockSpec((n_mem, w), tile),
                   pl.BlockSpec((1, v_rows, n_mem), lambda i: (i, 0, 0))] + cast_specs,
        out_shape=([jax.ShapeDtypeStruct((t, w), BF16),
                    jax.ShapeDtypeStruct((steps, v_rows, n_mem), BF16)]
                   + [jax.ShapeDtypeStruct(a.shape, BF16) for a in to_bf16]),
        compiler_params=_params(1),
        name="mem_proj",
    )(mem, g, w_kv, gk, g128, *to_bf16)
    return outs[0], outs[1], outs[2:]


def _bias_kernel(r_ref, o_ref):
    nq = TOKEN_TILE
    nk = 2 * nq
    hk = KEY_PART
    qc = lax.broadcasted_iota(jnp.int32, (nq, nk), 0) // CHUNK + B_LEFT_CHUNKS
    kc = lax.broadcasted_iota(jnp.int32, (nq, nk), 1) // CHUNK
    valid = (qc >= kc) & (qc - kc <= B_LEFT_CHUNKS)
    for hh in range(2):
        t = pltpu.roll(jnp.broadcast_to(r_ref[hh], (nq, nk)), 0, 1, stride=1, stride_axis=0)
        t = jnp.where(valid, t * LOG2E, NEG).T
        o_ref[0, :, hh * hk:(hh + 1) * hk] = t[:B_BAND_PARTS * hk, :hk]


def _rel_bias(rows):
    h, _, nk = rows.shape
    tile = (B_BAND_PARTS * KEY_PART, 2 * KEY_PART)
    return pl.pallas_call(
        _bias_kernel,
        grid=(h // 2,),
        in_specs=[pl.BlockSpec((2, 1, nk), lambda i: (i, 0, 0))],
        out_specs=pl.BlockSpec((1,) + tile, lambda i: (i, 0, 0)),
        out_shape=jax.ShapeDtypeStruct((h // 2,) + tile, F32),
        compiler_params=_params(1),
        name="rel_bias",
    )(rows)


def _split_queries(qt):
    row = lax.broadcasted_iota(jnp.int32, qt.shape, 0)
    zero = jnp.zeros_like(qt)
    return jnp.concatenate(
        [jnp.where(row < HEAD_DIM, qt, zero), jnp.where(row >= HEAD_DIM, qt, zero)], axis=1)


def _rows(start, size):
    if isinstance(start, int):
        return pl.ds(start, size)
    return pl.ds(pl.multiple_of(start, size), size)


def _column_max(s):
    return jnp.max(s, axis=0, keepdims=True)


def _online_softmax(s, m_prev, acc_prev, vt, s_max=None):
    if s_max is None:
        s_max = _column_max(s)
    m_new = s_max if m_prev is None else jnp.maximum(m_prev, s_max)
    pv = _dot(vt, jnp.exp2(s - m_new).astype(BF16))
    if acc_prev is None:
        return m_new, pv
    return m_new, jnp.exp2(m_prev - m_new) * acc_prev + pv


def _normalized(acc, rows):
    return acc[:rows] / acc[rows:rows + 1]


def _attn_a_kernel(qt_ref, k_ref, vt_ref, lam_ref, gain_ref, o_ref, q12_ref, *scratch,
                   lambda_init):
    n_sub, _, ts = qt_ref.shape[1:]
    subs = A_QUERY_SUBS
    n_blk = n_sub // subs
    per = A_CHAIN_SOFTMAXES
    cw = per * ts
    n_chains = 2 * subs // per
    look = A_LOOKAHEAD
    score_refs = scratch[:look]
    smax_refs = scratch[look:2 * look]
    state_refs = [scratch[2 * look + 2 * c:2 * look + 2 * c + 2] for c in range(n_chains)]
    hk = A_KEY_PART
    parts = ts // hk
    assert per == 1 and parts == 2
    tail_base = n_chains * cw
    for blk in range(n_blk):
        q12 = [_split_queries(qt_ref[0, blk * subs + h]) for h in range(subs)]
        tails = [jnp.concatenate([q[:, ts - hk:ts], q[:, 2 * ts - hk:]], axis=1) for q in q12]
        q12_ref[blk] = jnp.concatenate(q12 + tails, axis=1)
    lp = lam_ref[...]
    lam = (jnp.exp(jnp.sum(lp[0:1] * lp[1:2], axis=1, keepdims=True))
           - jnp.exp(jnp.sum(lp[2:3] * lp[3:4], axis=1, keepdims=True)) + lambda_init)


    def qk(item):
        qi, jb, kp, ch, _, tail = item
        c0 = tail_base + ch * ts if tail else ch * cw
        kb = k_ref[0, _rows(jb * ts + kp * hk, hk), :]
        s = _dot(kb, q12_ref[qi, :, c0:c0 + cw])
        return s, _column_max(s)

    chains_per_sub = 2 // per

    def reset(h):
        for m_ref, acc_ref in state_refs[chains_per_sub * h:chains_per_sub * (h + 1)]:
            m_ref[...] = jnp.full(m_ref.shape, NEG, F32)
            acc_ref[...] = jnp.zeros(acc_ref.shape, F32)

    def finish(qi, h):
        o12 = jnp.concatenate(
            [_normalized(acc_ref[...], 2 * HEAD_DIM)
             for _, acc_ref in state_refs[chains_per_sub * h:chains_per_sub * (h + 1)]], axis=1)
        o1, o2 = o12[:, :ts], o12[:, ts:]
        ot = o1 - lam * o2
        ot = ot * lax.rsqrt(jnp.mean(ot * ot, axis=0, keepdims=True) + EPS)
        o_ref[0, _rows((qi * subs + h) * ts, ts), :] = (
            ot.T * gain_ref[...] * (1.0 - lambda_init)).astype(BF16)
        reset(h)

    def run(seq, following, after=()):
        scores = [(score_refs[i][...], smax_refs[i][...]) for i in range(look)]
        for i, (qi, jb, kp, ch, masked, tail) in enumerate(seq):
            s, s_max = scores.pop(0)
            if masked:
                key_chunk = (lax.broadcasted_iota(jnp.int32, (hk, cw), 0) + kp * hk) // CHUNK
                query = lax.broadcasted_iota(jnp.int32, (hk, cw), 1)
                query_chunk = (query % hk + ts - hk if tail else query) // CHUNK
                s = jnp.where(key_chunk <= query_chunk, s, NEG)
                s_max = None
            nxt = i + look
            if nxt < len(seq):
                scores.append(qk(seq[nxt]))
            else:
                n = nxt - len(seq)
                score_refs[n][...], smax_refs[n][...] = qk(following[n])
            vt = vt_ref[0, jb, :, kp * hk:(kp + 1) * hk]
            if tail:
                refs = state_refs[2 * ch:2 * ch + 2]
                m_new, acc_new = _online_softmax(
                    s, jnp.concatenate([m_ref[:, ts - hk:] for m_ref, _ in refs], axis=1),
                    jnp.concatenate([acc_ref[:, ts - hk:] for _, acc_ref in refs], axis=1),
                    vt, s_max)
                for c, (m_ref, acc_ref) in enumerate(refs):
                    m_ref[:, ts - hk:] = m_new[:, c * hk:(c + 1) * hk]
                    acc_ref[:, ts - hk:] = acc_new[:, c * hk:(c + 1) * hk]
            else:
                m_ref, acc_ref = state_refs[ch]
                m_ref[...], acc_ref[...] = _online_softmax(s, m_ref[...], acc_ref[...], vt, s_max)
            if i in after:
                finish(qi, after[i])

    def key_block(qi, jb):
        return [(qi, jb, kp, ch, False, False) for kp in range(parts) for ch in range(n_chains)]

    def own_blocks(qi):
        seq, after = [], {}
        for kb in range(subs):
            for kp in range(parts):
                for h in range(kb, subs):
                    if h == kb and kp == parts - 1:
                        seq.append((qi, subs * qi + kb, kp, h, True, True))
                        after[len(seq) - 1] = kb
                    else:
                        seq += [(qi, subs * qi + kb, kp, 2 * h + c, h == kb, False)
                                for c in range(2)]
        return seq, after

    def query_block(qi, carry):
        def body(jb, c):
            run(key_block(qi, jb), key_block(qi, jb + 1))
            return c

        lax.fori_loop(0, subs * qi, body, 0)
        seq, after = own_blocks(qi)
        nxt_qi = jnp.minimum(qi + 1, n_blk - 1)
        run(seq, key_block(nxt_qi, 0), after)
        return carry

    for h in range(subs):
        reset(h)
    for i, item in enumerate(key_block(0, 0)[:look]):
        score_refs[i][...], smax_refs[i][...] = qk(item)
    if n_blk == 1:
        query_block(0, 0)
    else:
        lax.fori_loop(0, n_blk, query_block, 0)


def _attn_a(qt, k, vt, lam_p, gain, lambda_init):
    b, n_sub, w, ts = qt.shape
    s = n_sub * ts
    n_chains = 2 * A_QUERY_SUBS // A_CHAIN_SOFTMAXES
    cw = A_CHAIN_SOFTMAXES * ts
    return pl.pallas_call(
        functools.partial(_attn_a_kernel, lambda_init=lambda_init),
        grid=(b, w // LANES),
        in_specs=[pl.BlockSpec((1, n_sub, LANES, ts), lambda bi, hi: (bi, 0, hi, 0)),
                  pl.BlockSpec((1, s, LANES), lambda bi, hi: (bi, 0, hi)),
                  pl.BlockSpec((1, n_sub, A_V_ROWS, ts), lambda bi, hi: (bi, 0, hi, 0)),
                  _const_spec(lam_p.shape), _const_spec((1, LANES))],
        out_specs=pl.BlockSpec((1, s, LANES), lambda bi, hi: (bi, 0, hi)),
        out_shape=jax.ShapeDtypeStruct((b, s, w), BF16),
        scratch_shapes=(
            [pltpu.VMEM((n_sub // A_QUERY_SUBS, LANES, n_chains * cw + A_QUERY_SUBS * ts), BF16)]
            + [pltpu.VMEM((A_KEY_PART, cw), F32)] * A_LOOKAHEAD
            + [pltpu.VMEM((1, cw), F32)] * A_LOOKAHEAD
            + [pltpu.VMEM((1, cw), F32), pltpu.VMEM((A_V_ROWS, cw), F32)] * n_chains),
        compiler_params=_params(2),
        name="attn_a",
    )(qt, k, vt, lam_p, gain)


def _attn_b_kernel(qt_ref, k_ref, vt_ref, bias_ref, o_ref, q12_ref, *score_refs):
    n_sub, _, ts = qt_ref.shape[1:]
    hk = KEY_PART
    per_sub = ts // hk
    n_chains = B_QUERY_SUBS * per_sub
    n_blk = n_sub // B_QUERY_SUBS
    lead = B_BAND_PARTS - 1
    assert lead * hk == B_LEFT_CHUNKS * CHUNK and LOOKAHEAD <= n_chains
    for blk in range(n_blk):
        cols = []
        for c in range(n_chains):
            q12 = _split_queries(qt_ref[0, blk * B_QUERY_SUBS + c // per_sub])
            c0 = (c % per_sub) * hk
            cols += [q12[:, c0:c0 + hk], q12[:, ts + c0:ts + c0 + hk]]
        q12_ref[blk] = jnp.concatenate(cols, axis=1)

    items = [(c + d, c) for d in range(lead, -1, -1) for c in range(n_chains)]

    def source(qi, wp):
        part = qi * n_chains + wp - lead
        return jnp.maximum(part, 0) // per_sub, (wp - lead) % per_sub * hk

    def qk(qi, item):
        wp, c = item
        blk, r0 = source(qi, wp)
        cols = slice(c * 2 * hk, (c + 1) * 2 * hk)
        d = wp - c
        kb = k_ref[0, pl.ds(pl.multiple_of(blk * ts + r0, hk), hk), :]
        s = _dot(kb, q12_ref[qi, :, cols]) + bias_ref[0, d * hk:(d + 1) * hk, :]
        if wp < lead:
            s = jnp.where(qi > 0, s, NEG)
        return s

    def query_block(qi, carry):
        scores = [r[...] for r in score_refs]
        state = [(None, None)] * n_chains
        for i, (wp, c) in enumerate(items):
            s = scores.pop(0)
            nxt = i + LOOKAHEAD
            if nxt < len(items):
                scores.append(qk(qi, items[nxt]))
            else:
                score_refs[nxt - len(items)][...] = qk(
                    jnp.minimum(qi + 1, n_blk - 1), items[nxt - len(items)])
            blk, r0 = source(qi, wp)
            state[c] = _online_softmax(s, state[c][0], state[c][1],
                                       vt_ref[0, blk, :, r0:r0 + hk])
        outs = []
        for _, acc in state:
            heads = [acc[hh * HEAD_DIM:(hh + 1) * HEAD_DIM, hh * hk:(hh + 1) * hk]
                     / acc[2 * HEAD_DIM:2 * HEAD_DIM + 1, hh * hk:(hh + 1) * hk]
                     for hh in range(2)]
            outs.append(jnp.concatenate(heads, axis=0))
        for h in range(B_QUERY_SUBS):
            ot = jnp.concatenate(outs[h * per_sub:(h + 1) * per_sub], axis=1)
            o_ref[0, pl.ds(pl.multiple_of((qi * B_QUERY_SUBS + h) * ts, ts), ts), :] = (
                ot.T.astype(BF16))
        return carry

    for i, r in enumerate(score_refs):
        r[...] = qk(0, items[i])
    lax.fori_loop(0, n_blk, query_block, 0)


def _attn_b(qt, k, vt, bias):
    b, n_sub, w, ts = qt.shape
    s = n_sub * ts
    return pl.pallas_call(
        _attn_b_kernel,
        grid=(w // LANES, b),
        in_specs=[pl.BlockSpec((1, n_sub, LANES, ts), lambda pi, bi: (bi, 0, pi, 0)),
                  pl.BlockSpec((1, s, LANES), lambda pi, bi: (bi, 0, pi)),
                  pl.BlockSpec((1, n_sub, B_V_ROWS, ts), lambda pi, bi: (bi, 0, pi, 0)),
                  pl.BlockSpec((1,) + bias.shape[1:], lambda pi, bi: (pi, 0, 0))],
        out_specs=pl.BlockSpec((1, s, LANES), lambda pi, bi: (bi, 0, pi)),
        out_shape=jax.ShapeDtypeStruct((b, s, w), BF16),
        scratch_shapes=([pltpu.VMEM((n_sub // B_QUERY_SUBS, LANES, 2 * B_QUERY_SUBS * ts), BF16)]
                        + [pltpu.VMEM((KEY_PART, 2 * KEY_PART), F32)] * LOOKAHEAD),
        compiler_params=_params(2),
        name="attn_b",
    )(qt, k, vt, bias)


def _attn_c_kernel(qt_ref, k_ref, vt_ref, o_ref):
    n_blk, _, tq = qt_ref.shape[1:]
    items = [(qi, hd) for qi in range(n_blk) for hd in range(C_HEADS)]

    def qk(item):
        qi, hd = item
        rows = slice(hd * C_DIM, (hd + 1) * C_DIM)
        return _dot(k_ref[0, :, rows], qt_ref[0, qi, rows, :])

    scores = [qk(it) for it in items[:LOOKAHEAD]]
    outs = []
    for i, (qi, hd) in enumerate(items):
        s = scores.pop(0)
        if i + LOOKAHEAD < len(items):
            scores.append(qk(items[i + LOOKAHEAD]))
        _, acc = _online_softmax(s, None, None, vt_ref[0, hd * A_V_ROWS:(hd + 1) * A_V_ROWS, :])
        outs.append(_normalized(acc, C_DIM))
        if hd == C_HEADS - 1:
            o_ref[0, qi * tq:(qi + 1) * tq, :] = jnp.concatenate(outs, axis=0).T.astype(BF16)
            outs = []


def _attn_c(qt, mk, mvt):
    b, n_blk, w, tq = qt.shape
    n_mem = mk.shape[1]
    return pl.pallas_call(
        _attn_c_kernel,
        grid=(b,),
        in_specs=[pl.BlockSpec((1, n_blk, w, tq), lambda bi: (bi, 0, 0, 0)),
                  pl.BlockSpec((1, n_mem, w), lambda bi: (bi, 0, 0)),
                  pl.BlockSpec((1, mvt.shape[1], n_mem), lambda bi: (bi, 0, 0))],
        out_specs=pl.BlockSpec((1, n_blk * tq, w), lambda bi: (bi, 0, 0)),
        out_shape=jax.ShapeDtypeStruct((b, n_blk * tq, w), BF16),
        compiler_params=_params(1),
        name="attn_c",
    )(qt, mk, mvt)


def _mix_kernel(x_ref, g_ref, wgate_ref, bgate_ref, oa_ref, ob_ref, oc_ref, wbr_ref, wout_ref,
                o_ref):
    x = x_ref[...]
    d = x.shape[1]
    h = _rms(x, g_ref[...]).astype(BF16)
    y = None
    for n, br_ref in enumerate((oa_ref, ob_ref, oc_ref)):
        z = _dot(h, wgate_ref[:, n * d:(n + 1) * d]) + bgate_ref[:, n * d:(n + 1) * d]
        term = (1.0 / (1.0 + jnp.exp(-z))) * _dot(br_ref[...], wbr_ref[n])
        y = term if y is None else y + term
    o_ref[...] = x + _dot(y.astype(BF16), wout_ref[...])


def _mix_out(x, g, w_gate, b_gate, o_a, o_b, o_c, w_branch, w_out):
    t, d = x.shape
    tm = FFN_TILE
    tile = lambda i: (i, 0)
    br_spec = pl.BlockSpec((tm, BRANCH_W), tile)
    return pl.pallas_call(
        _mix_kernel,
        grid=(t // tm,),
        in_specs=[pl.BlockSpec((tm, d), tile), _const_spec((1, d)), _const_spec(w_gate.shape),
                  _const_spec(b_gate.shape), br_spec, br_spec, br_spec,
                  _const_spec(w_branch.shape), _const_spec(w_out.shape)],
        out_specs=pl.BlockSpec((tm, d), tile),
        out_shape=jax.ShapeDtypeStruct((t, d), F32),
        compiler_params=_params(1),
        name="mix_out",
    )(x, g, w_gate, b_gate, o_a, o_b, o_c, w_branch, w_out)


def _rope_tables(seq):
    half = HEAD_DIM // 2
    inv = ROPE_THETA ** (-jnp.arange(half, dtype=F32) / half)
    ang = jnp.arange(seq, dtype=F32)[:, None] * inv[None, :]
    return jnp.cos(ang).T, jnp.sin(ang).T


def _group_mean_matrix(group):
    idx = jnp.arange(MXU_TILE) // group
    return jnp.where(idx[:, None] == idx[None, :], 1.0 / group, 0.0).astype(BF16)


def _bias_rows(table):
    h = table.shape[0]
    far = jnp.broadcast_to(table[:, -1:], (h, TOKEN_TILE - REL_CLIP))
    near = table[:, ::-1]
    tail = jnp.broadcast_to(table[:, -1:], (h, 2 * TOKEN_TILE - far.shape[1] - near.shape[1]))
    return jnp.concatenate([far, near, tail], axis=1)[:, None, :]


def kernel(x, mem, ffn1_norm, ffn1_wg, ffn1_wu, ffn1_wd, mix_norm, w_in, a_q_norm, a_k_norm,
           a_lambda, a_sub_norm, b_q_norm, b_k_norm, b_rel_bias, mem_norm, w_mem_kv, c_q_norm,
           c_k_norm, w_gate, b_gate, w_branch, w_out, ffn2_norm, ffn2_wg, ffn2_wu, ffn2_wd,
           final_norm):
    b, s, d = x.shape
    n_mem = mem.shape[1]
    depth = ffn1_norm.shape[0]
    tm = TOKEN_TILE
    w = BRANCH_W
    assert s % tm == 0 and d % MXU_TILE == 0 and w_in.shape[2] == 7 * w

    tables = _rope_tables(s)
    g128 = _group_mean_matrix(C_DIM)
    row = lambda v: v.reshape(1, -1)
    heads = lambda v: jnp.tile(v, w // v.shape[0])

    xt = x.reshape(b * s, d)
    memt = mem.reshape(b * n_mem, d)
    for l in range(depth):
        lambda_init = 0.8 - 0.6 * math.exp(-0.3 * l)
        mk, mvt, (wg1, wu1, wd1) = _mem_proj(
            memt, row(mem_norm[l]), w_mem_kv[l], row(heads(c_k_norm[l])), g128, n_mem,
            to_bf16=(ffn1_wg[l], ffn1_wu[l], ffn1_wd[l]))
        xt, (wg2, wu2, wd2, w_in_b, w_gate_b, w_branch_b, w_out_b) = _ffn(
            xt, row(ffn1_norm[l]), wg1, wu1, wd1,
            to_bf16=(ffn2_wg[l], ffn2_wu[l], ffn2_wd[l], w_in[l], w_gate[l],
                     w_branch[l].reshape(-1, d), w_out[l]))
        w_branch_b = w_branch_b.reshape(w_branch[l].shape)

        col = lambda v: heads(v).reshape(w, 1)
        aqt, avt, bqt, bvt, cqt, ak, bk = _in_proj(
            xt, row(mix_norm[l]), w_in_b.T, tables,
            col(a_q_norm[l]), col(a_k_norm[l]), col(b_q_norm[l]), col(b_k_norm[l]),
            col(c_q_norm[l]), s)
        bias = _rel_bias(_bias_rows(b_rel_bias[l]))

        blocks = lambda a: a.reshape(b, s // tm, a.shape[1], tm)
        seq3 = lambda a: a.reshape(b, s, w)
        o_a = _attn_a(blocks(aqt), seq3(ak), blocks(avt), a_lambda[l], row(a_sub_norm[l]),
                      lambda_init)
        o_b = _attn_b(blocks(bqt), seq3(bk), blocks(bvt), bias)
        o_c = _attn_c(blocks(cqt), mk.reshape(b, n_mem, w), mvt)

        flat = lambda a: a.reshape(b * s, w)
        xt = _mix_out(xt, row(mix_norm[l]), w_gate_b, row(b_gate[l]),
                      flat(o_a), flat(o_b), flat(o_c), w_branch_b, w_out_b)

        xt, _ = _ffn(xt, row(ffn2_norm[l]), wg2, wu2, wd2, row(final_norm[l]))
    return xt.reshape(b, s, d)
```

```python
import functools
import math

import jax
import jax.numpy as jnp
from jax import lax
from jax.experimental import pallas as pl
from jax.experimental.pallas import tpu as pltpu

F32 = jnp.float32
BF16 = jnp.bfloat16

CHUNK = 64
HEAD_DIM = 64
A_HEADS = 4
B_HEADS = 8
B_LEFT_CHUNKS = 8
REL_CLIP = 128
C_HEADS = 4
C_DIM = 128
BRANCH_W = 512
ROPE_THETA = 10000.0
EPS = 1e-6
NEG = -1e30
LOG2E = math.log2(math.e)

LANES = 128
MXU_TILE = 256
BF16_ROWS = 16
TOKEN_TILE = 512
FFN_TILE = 1024
IN_TILES = 2
KEY_PART = MXU_TILE
LOOKAHEAD = 2
A_KEY_PART = MXU_TILE
A_LOOKAHEAD = 2
A_QUERY_SUBS = 4
B_QUERY_SUBS = 4
B_BAND_PARTS = 3
ONES_ROWS = BF16_ROWS
VMEM_LIMIT = 56 * 1024 * 1024

A_V_ROWS = 2 * HEAD_DIM + ONES_ROWS
B_V_ROWS = 2 * HEAD_DIM + ONES_ROWS

assert TOKEN_TILE == B_LEFT_CHUNKS * CHUNK


def _dot(a, b):
    return jnp.dot(a, b, preferred_element_type=F32)


def _dot_nt(a, b):
    return lax.dot_general(a, b, (((1,), (1,)), ((), ())), preferred_element_type=F32)


def _rms(x, g):
    return x * lax.rsqrt(jnp.mean(x * x, axis=-1, keepdims=True) + EPS) * g


def _group_rms(z, gmat_ref, gain):
    z2 = (z * z).astype(BF16)
    ms = jnp.concatenate(
        [_dot(z2[:, :MXU_TILE], gmat_ref[...]), _dot(z2[:, MXU_TILE:], gmat_ref[...])], axis=1)
    return z * lax.rsqrt(ms + EPS) * gain


def _heads_t(z, gain, group, scale, cos=None, sin=None):
    outs = []
    for g0 in range(0, z.shape[0], group):
        zz = z[g0:g0 + group]
        r = lax.rsqrt(jnp.mean(zz * zz, axis=0, keepdims=True) + EPS) * scale
        y = zz * r * gain[g0:g0 + group]
        if cos is not None:
            y1, y2 = y[:group // 2], y[group // 2:]
            y = jnp.concatenate([y1 * cos - y2 * sin, y2 * cos + y1 * sin], axis=0)
        outs.append(y)
    return jnp.concatenate(outs, axis=0)


def _with_ones(vt, head_rows):
    ones = jnp.ones((ONES_ROWS, vt.shape[1]), vt.dtype)
    parts = []
    for r0 in range(0, vt.shape[0], head_rows):
        parts += [vt[r0:r0 + head_rows], ones]
    return jnp.concatenate(parts, axis=0)


def _const_spec(shape):
    nd = len(shape)
    return pl.BlockSpec(shape, lambda *_: (0,) * nd, pipeline_mode=pl.Buffered(1))


def _cast_specs(arrays, steps):
    specs = []
    for a in arrays:
        n_blocks = math.gcd(steps, a.shape[0] // BF16_ROWS)
        rows, repeat = a.shape[0] // n_blocks, steps // n_blocks
        specs.append(pl.BlockSpec((rows, a.shape[1]), lambda i, r=repeat: (i // r, 0)))
    return specs


def _cast_side_job(srcs, dsts):
    for src, dst in zip(srcs, dsts):
        dst[...] = src[...].astype(BF16)


def _params(n_axes):
    return pltpu.CompilerParams(
        dimension_semantics=("arbitrary",) * n_axes, vmem_limit_bytes=VMEM_LIMIT)


def _ffn_chunks(d_ff):
    step = 3 * MXU_TILE
    return [(c, min(c + step, d_ff)) for c in range(0, d_ff, step)]


def _ffn_kernel(x_ref, g_ref, wg_ref, wu_ref, wd_ref, *rest, final_norm, n_cast):
    rest = list(rest)
    fg_ref = rest.pop(0) if final_norm else None
    cast_in, o_ref, cast_out, act_ref = (rest[:n_cast], rest[n_cast],
                                         rest[n_cast + 1:2 * n_cast + 1], rest[-1])
    _cast_side_job(cast_in, cast_out)
    x = x_ref[...]
    h = _rms(x, g_ref[...]).astype(BF16)
    for c0, c1 in _ffn_chunks(wg_ref.shape[1]):
        g = _dot(h, wg_ref[:, c0:c1])
        u = _dot(h, wu_ref[:, c0:c1])
        act_ref[:, c0:c1] = (g * (1.0 / (1.0 + jnp.exp(-g))) * u).astype(BF16)
    out = x + 0.5 * _dot(act_ref[...], wd_ref[...])
    if final_norm:
        out = _rms(out, fg_ref[...])
    o_ref[...] = out


def _ffn(x, g, wg, wu, wd, final_g=None, to_bf16=()):
    t, d = x.shape
    f = wg.shape[1]
    tm = FFN_TILE
    steps = t // tm
    tile = lambda i: (i, 0)
    in_specs = [pl.BlockSpec((tm, d), tile), _const_spec((1, d)),
                _const_spec((d, f)), _const_spec((d, f)), _const_spec((f, d))]
    args = [x, g, wg, wu, wd]
    if final_g is not None:
        in_specs.append(_const_spec((1, d)))
        args.append(final_g)
    cast_specs = _cast_specs(to_bf16, steps)
    outs = pl.pallas_call(
        functools.partial(_ffn_kernel, final_norm=final_g is not None, n_cast=len(to_bf16)),
        grid=(steps,),
        in_specs=in_specs + cast_specs,
        out_specs=[pl.BlockSpec((tm, d), tile)] + cast_specs,
        out_shape=([jax.ShapeDtypeStruct((t, d), F32)]
                   + [jax.ShapeDtypeStruct(a.shape, BF16) for a in to_bf16]),
        scratch_shapes=[pltpu.VMEM((tm, f), BF16)],
        compiler_params=_params(1),
        name="ffn_final" if final_g is not None else "ffn",
    )(*args, *to_bf16)
    return outs[0], outs[1:]


def _in_kernel(x_ref, g_ref, wt_ref, cost_ref, sint_ref,
               gaq_ref, gak_ref, gbq_ref, gbk_ref, gcq_ref,
               aqt_ref, avt_ref, bqt_ref, bvt_ref, cqt_ref, ak_ref, bk_ref):
    w = BRANCH_W
    tm = TOKEN_TILE
    qk_scale = HEAD_DIM ** -0.5 * LOG2E
    for sub in range(IN_TILES):
        rows = slice(sub * tm, (sub + 1) * tm)
        h = _rms(x_ref[rows, :], g_ref[...]).astype(BF16)
        cos, sin = cost_ref[:, rows], sint_ref[:, rows]

        def seg_t(i):
            return _dot_nt(wt_ref[i * w:(i + 1) * w, :], h)

        aqt_ref[sub] = _heads_t(seg_t(0), gaq_ref[...], HEAD_DIM, qk_scale, cos, sin
                                ).astype(BF16)
        ak_ref[rows, :] = _heads_t(seg_t(1), gak_ref[...], HEAD_DIM, 1.0, cos, sin
                                   ).T.astype(BF16)
        avt_ref[sub] = _with_ones(seg_t(2).astype(BF16), 2 * HEAD_DIM)
        bqt_ref[sub] = _heads_t(seg_t(3), gbq_ref[...], HEAD_DIM, qk_scale).astype(BF16)
        bk_ref[rows, :] = _heads_t(seg_t(4), gbk_ref[...], HEAD_DIM, 1.0).T.astype(BF16)
        bvt_ref[sub] = _with_ones(seg_t(5).astype(BF16), 2 * HEAD_DIM)
        cqt_ref[sub] = _heads_t(seg_t(6), gcq_ref[...], C_DIM, C_DIM ** -0.5 * LOG2E
                                ).astype(BF16)


def _in_proj(x, g, w_t, tables, gaq, gak, gbq, gbk, gcq, seq):
    t, d = x.shape
    tm = TOKEN_TILE
    ts = IN_TILES * tm
    n_tiles = t // tm
    n_seq_steps = seq // ts
    w = BRANCH_W
    cos_t, sin_t = tables
    tile = lambda i: (i, 0)
    half = HEAD_DIM // 2
    in_specs = [pl.BlockSpec((ts, d), tile), _const_spec((1, d)), _const_spec(w_t.shape),
                pl.BlockSpec((half, ts), lambda i: (0, i % n_seq_steps)),
                pl.BlockSpec((half, ts), lambda i: (0, i % n_seq_steps))]
    in_specs += [_const_spec((w, 1))] * 5
    t_rows = [w, A_HEADS * A_V_ROWS, w, B_HEADS // 2 * B_V_ROWS, w]
    return pl.pallas_call(
        _in_kernel,
        grid=(t // ts,),
        in_specs=in_specs,
        out_specs=([pl.BlockSpec((IN_TILES, r, tm), lambda i: (i, 0, 0)) for r in t_rows]
                   + [pl.BlockSpec((ts, w), tile)] * 2),
        out_shape=([jax.ShapeDtypeStruct((n_tiles, r, tm), BF16) for r in t_rows]
                   + [jax.ShapeDtypeStruct((t, w), BF16)] * 2),
        compiler_params=_params(1),
        name="in_proj",
    )(x, g, w_t, cos_t, sin_t, gaq, gak, gbq, gbk, gcq)


def _mem_kernel(m_ref, g_ref, wkv_ref, gk_ref, g128_ref, *rest, n_cast):
    cast_in, (mk_ref, mvt_ref), cast_out = rest[:n_cast], rest[n_cast:n_cast + 2], rest[n_cast + 2:]
    _cast_side_job(cast_in, cast_out)
    w = BRANCH_W
    h = _rms(m_ref[...], g_ref[...]).astype(BF16)
    wkv = wkv_ref[...].astype(BF16)
    mk_ref[...] = _group_rms(_dot(h, wkv[:, :w]), g128_ref, gk_ref[...]).astype(BF16)
    mvt_ref[0] = _with_ones(_dot(h, wkv[:, w:]).T.astype(BF16), C_DIM)


def _mem_proj(mem, g, w_kv, gk, g128, n_mem, to_bf16=()):
    t, d = mem.shape
    w = BRANCH_W
    v_rows = C_HEADS * A_V_ROWS
    steps = t // n_mem
    tile = lambda i: (i, 0)
    cast_specs = _cast_specs(to_bf16, steps)
    outs = pl.pallas_call(
        functools.partial(_mem_kernel, n_cast=len(to_bf16)),
        grid=(steps,),
        in_specs=[pl.BlockSpec((n_mem, d), tile), _const_spec((1, d)), _const_spec(w_kv.shape),
                  _const_spec((1, w)), _const_spec((MXU_TILE, MXU_TILE))] + cast_specs,
        out_specs=[pl.BlockSpec((n_mem, w), tile),
                   pl.BlockSpec((1, v_rows, n_mem), lambda i: (i, 0, 0))] + cast_specs,
        out_shape=([jax.ShapeDtypeStruct((t, w), BF16),
                    jax.ShapeDtypeStruct((steps, v_rows, n_mem), BF16)]
                   + [jax.ShapeDtypeStruct(a.shape, BF16) for a in to_bf16]),
        compiler_params=_params(1),
        name="mem_proj",
    )(mem, g, w_kv, gk, g128, *to_bf16)
    return outs[0], outs[1], outs[2:]


def _bias_kernel(r_ref, o_ref):
    nq = TOKEN_TILE
    nk = 2 * nq
    hk = KEY_PART
    qc = lax.broadcasted_iota(jnp.int32, (nq, nk), 0) // CHUNK + B_LEFT_CHUNKS
    kc = lax.broadcasted_iota(jnp.int32, (nq, nk), 1) // CHUNK
    valid = (qc >= kc) & (qc - kc <= B_LEFT_CHUNKS)
    for hh in range(2):
        t = pltpu.roll(jnp.broadcast_to(r_ref[hh], (nq, nk)), 0, 1, stride=1, stride_axis=0)
        t = jnp.where(valid, t * LOG2E, NEG).T
        o_ref[0, :, hh * hk:(hh + 1) * hk] = t[:B_BAND_PARTS * hk, :hk]


def _rel_bias(rows):
    h, _, nk = rows.shape
    tile = (B_BAND_PARTS * KEY_PART, 2 * KEY_PART)
    return pl.pallas_call(
        _bias_kernel,
        grid=(h // 2,),
        in_specs=[pl.BlockSpec((2, 1, nk), lambda i: (i, 0, 0))],
        out_specs=pl.BlockSpec((1,) + tile, lambda i: (i, 0, 0)),
        out_shape=jax.ShapeDtypeStruct((h // 2,) + tile, F32),
        compiler_params=_params(1),
        name="rel_bias",
    )(rows)


def _split_queries(qt):
    row = lax.broadcasted_iota(jnp.int32, qt.shape, 0)
    zero = jnp.zeros_like(qt)
    return jnp.concatenate(
        [jnp.where(row < HEAD_DIM, qt, zero), jnp.where(row >= HEAD_DIM, qt, zero)], axis=1)


def _rows(start, size):
    if isinstance(start, int):
        return pl.ds(start, size)
    return pl.ds(pl.multiple_of(start, size), size)


def _column_max(s):
    return jnp.max(s, axis=0, keepdims=True)


def _online_softmax(s, m_prev, acc_prev, vt, s_max=None):
    if s_max is None:
        s_max = _column_max(s)
    m_new = s_max if m_prev is None else jnp.maximum(m_prev, s_max)
    pv = _dot(vt, jnp.exp2(s - m_new).astype(BF16))
    if acc_prev is None:
        return m_new, pv
    return m_new, jnp.exp2(m_prev - m_new) * acc_prev + pv


def _normalized(acc, rows):
    return acc[:rows] / acc[rows:rows + 1]


def _attn_a_kernel(qt_ref, k_ref, vt_ref, lam_ref, gain_ref, o_ref, q12_ref, *scratch,
                   lambda_init):
    n_sub, _, ts = qt_ref.shape[1:]
    subs = A_QUERY_SUBS
    n_blk = n_sub // subs
    n_chains = 2 * subs
    look = A_LOOKAHEAD
    score_refs = scratch[:look]
    smax_refs = scratch[look:2 * look]
    state_refs = [scratch[2 * look + 2 * c:2 * look + 2 * c + 2] for c in range(n_chains)]
    hk = A_KEY_PART
    parts = ts // hk
    assert parts == 2
    tail_base = n_chains * ts
    for blk in range(n_blk):
        q12 = [_split_queries(qt_ref[0, blk * subs + h]) for h in range(subs)]
        tails = [jnp.concatenate([q[:, ts - hk:ts], q[:, 2 * ts - hk:]], axis=1) for q in q12]
        q12_ref[blk] = jnp.concatenate(q12 + tails, axis=1)
    lp = lam_ref[...]
    lam = (jnp.exp(jnp.sum(lp[0:1] * lp[1:2], axis=1, keepdims=True))
           - jnp.exp(jnp.sum(lp[2:3] * lp[3:4], axis=1, keepdims=True)) + lambda_init)


    def qk(item):
        qi, jb, kp, ch, _, tail = item
        c0 = (tail_base if tail else 0) + ch * ts
        kb = k_ref[0, _rows(jb * ts + kp * hk, hk), :]
        s = _dot(kb, q12_ref[qi, :, c0:c0 + ts])
        return s, _column_max(s)

    def reset(h):
        for m_ref, acc_ref in state_refs[2 * h:2 * h + 2]:
            m_ref[...] = jnp.full(m_ref.shape, NEG, F32)
            acc_ref[...] = jnp.zeros(acc_ref.shape, F32)

    def finish(qi, h):
        o1, o2 = (_normalized(acc_ref[...], 2 * HEAD_DIM)
                  for _, acc_ref in state_refs[2 * h:2 * h + 2])
        ot = o1 - lam * o2
        ot = ot * lax.rsqrt(jnp.mean(ot * ot, axis=0, keepdims=True) + EPS)
        o_ref[0, _rows((qi * subs + h) * ts, ts), :] = (
            ot.T * gain_ref[...] * (1.0 - lambda_init)).astype(BF16)
        reset(h)

    def run(seq, following, after=()):
        scores = [(score_refs[i][...], smax_refs[i][...]) for i in range(look)]
        for i, (qi, jb, kp, ch, masked, tail) in enumerate(seq):
            s, s_max = scores.pop(0)
            if masked:
                key_chunk = (lax.broadcasted_iota(jnp.int32, (hk, ts), 0) + kp * hk) // CHUNK
                query = lax.broadcasted_iota(jnp.int32, (hk, ts), 1)
                query_chunk = (query % hk + ts - hk if tail else query) // CHUNK
                s = jnp.where(key_chunk <= query_chunk, s, NEG)
                s_max = None
            nxt = i + look
            if nxt < len(seq):
                scores.append(qk(seq[nxt]))
            else:
                n = nxt - len(seq)
                score_refs[n][...], smax_refs[n][...] = qk(following[n])
            vt = vt_ref[0, jb, :, kp * hk:(kp + 1) * hk]
            if tail:
                refs = state_refs[2 * ch:2 * ch + 2]
                m_new, acc_new = _online_softmax(
                    s, jnp.concatenate([m_ref[:, ts - hk:] for m_ref, _ in refs], axis=1),
                    jnp.concatenate([acc_ref[:, ts - hk:] for _, acc_ref in refs], axis=1),
                    vt, s_max)
                for c, (m_ref, acc_ref) in enumerate(refs):
                    m_ref[:, ts - hk:] = m_new[:, c * hk:(c + 1) * hk]
                    acc_ref[:, ts - hk:] = acc_new[:, c * hk:(c + 1) * hk]
            else:
                m_ref, acc_ref = state_refs[ch]
                m_ref[...], acc_ref[...] = _online_softmax(s, m_ref[...], acc_ref[...], vt, s_max)
            if i in after:
                finish(qi, after[i])

    def key_block(qi, jb):
        return [(qi, jb, kp, ch, False, False) for kp in range(parts) for ch in range(n_chains)]

    def own_blocks(qi):
        seq, after = [], {}
        for kb in range(subs):
            for kp in range(parts):
                for h in range(kb, subs):
                    if h == kb and kp == parts - 1:
                        seq.append((qi, subs * qi + kb, kp, h, True, True))
                        after[len(seq) - 1] = kb
                    else:
                        seq += [(qi, subs * qi + kb, kp, 2 * h + c, h == kb, False)
                                for c in range(2)]
        return seq, after

    def query_block(qi, carry):
        def body(jb, c):
            run(key_block(qi, jb), key_block(qi, jb + 1))
            return c

        lax.fori_loop(0, subs * qi, body, 0)
        seq, after = own_blocks(qi)
        nxt_qi = jnp.minimum(qi + 1, n_blk - 1)
        run(seq, key_block(nxt_qi, 0), after)
        return carry

    for h in range(subs):
        reset(h)
    for i, item in enumerate(key_block(0, 0)[:look]):
        score_refs[i][...], smax_refs[i][...] = qk(item)
    if n_blk == 1:
        query_block(0, 0)
    else:
        lax.fori_loop(0, n_blk, query_block, 0)


def _attn_a(qt, k, vt, lam_p, gain, lambda_init):
    b, n_sub, w, ts = qt.shape
    s = n_sub * ts
    n_chains = 2 * A_QUERY_SUBS
    return pl.pallas_call(
        functools.partial(_attn_a_kernel, lambda_init=lambda_init),
        grid=(b, w // LANES),
        in_specs=[pl.BlockSpec((1, n_sub, LANES, ts), lambda bi, hi: (bi, 0, hi, 0)),
                  pl.BlockSpec((1, s, LANES), lambda bi, hi: (bi, 0, hi)),
                  pl.BlockSpec((1, n_sub, A_V_ROWS, ts), lambda bi, hi: (bi, 0, hi, 0)),
                  _const_spec(lam_p.shape), _const_spec((1, LANES))],
        out_specs=pl.BlockSpec((1, s, LANES), lambda bi, hi: (bi, 0, hi)),
        out_shape=jax.ShapeDtypeStruct((b, s, w), BF16),
        scratch_shapes=(
            [pltpu.VMEM((n_sub // A_QUERY_SUBS, LANES, (n_chains + A_QUERY_SUBS) * ts), BF16)]
            + [pltpu.VMEM((A_KEY_PART, ts), F32)] * A_LOOKAHEAD
            + [pltpu.VMEM((1, ts), F32)] * A_LOOKAHEAD
            + [pltpu.VMEM((1, ts), F32), pltpu.VMEM((A_V_ROWS, ts), F32)] * n_chains),
        compiler_params=_params(2),
        name="attn_a",
    )(qt, k, vt, lam_p, gain)


def _attn_b_kernel(qt_ref, k_ref, vt_ref, bias_ref, o_ref, q12_ref, *score_refs):
    n_sub, _, ts = qt_ref.shape[1:]
    hk = KEY_PART
    per_sub = ts // hk
    n_chains = B_QUERY_SUBS * per_sub
    n_blk = n_sub // B_QUERY_SUBS
    lead = B_BAND_PARTS - 1
    assert lead * hk == B_LEFT_CHUNKS * CHUNK and LOOKAHEAD <= n_chains
    for blk in range(n_blk):
        cols = []
        for c in range(n_chains):
            q12 = _split_queries(qt_ref[0, blk * B_QUERY_SUBS + c // per_sub])
            c0 = (c % per_sub) * hk
            cols += [q12[:, c0:c0 + hk], q12[:, ts + c0:ts + c0 + hk]]
        q12_ref[blk] = jnp.concatenate(cols, axis=1)

    items = [(c + d, c) for d in range(lead, -1, -1) for c in range(n_chains)]

    def source(qi, wp):
        part = qi * n_chains + wp - lead
        return jnp.maximum(part, 0) // per_sub, (wp - lead) % per_sub * hk

    def qk(qi, item):
        wp, c = item
        blk, r0 = source(qi, wp)
        cols = slice(c * 2 * hk, (c + 1) * 2 * hk)
        d = wp - c
        kb = k_ref[0, pl.ds(pl.multiple_of(blk * ts + r0, hk), hk), :]
        s = _dot(kb, q12_ref[qi, :, cols]) + bias_ref[0, d * hk:(d + 1) * hk, :]
        if wp < lead:
            s = jnp.where(qi > 0, s, NEG)
        return s

    def query_block(qi, carry):
        scores = [r[...] for r in score_refs]
        state = [(None, None)] * n_chains
        for i, (wp, c) in enumerate(items):
            s = scores.pop(0)
            nxt = i + LOOKAHEAD
            if nxt < len(items):
                scores.append(qk(qi, items[nxt]))
            else:
                score_refs[nxt - len(items)][...] = qk(
                    jnp.minimum(qi + 1, n_blk - 1), items[nxt - len(items)])
            blk, r0 = source(qi, wp)
            state[c] = _online_softmax(s, state[c][0], state[c][1],
                                       vt_ref[0, blk, :, r0:r0 + hk])
        outs = []
        for _, acc in state:
            heads = [acc[hh * HEAD_DIM:(hh + 1) * HEAD_DIM, hh * hk:(hh + 1) * hk]
                     / acc[2 * HEAD_DIM:2 * HEAD_DIM + 1, hh * hk:(hh + 1) * hk]
                     for hh in range(2)]
            outs.append(jnp.concatenate(heads, axis=0))
        for h in range(B_QUERY_SUBS):
            ot = jnp.concatenate(outs[h * per_sub:(h + 1) * per_sub], axis=1)
            o_ref[0, pl.ds(pl.multiple_of((qi * B_QUERY_SUBS + h) * ts, ts), ts), :] = (
                ot.T.astype(BF16))
        return carry

    for i, r in enumerate(score_refs):
        r[...] = qk(0, items[i])
    lax.fori_loop(0, n_blk, query_block, 0)


def _attn_b(qt, k, vt, bias):
    b, n_sub, w, ts = qt.shape
    s = n_sub * ts
    return pl.pallas_call(
        _attn_b_kernel,
        grid=(w // LANES, b),
        in_specs=[pl.BlockSpec((1, n_sub, LANES, ts), lambda pi, bi: (bi, 0, pi, 0)),
                  pl.BlockSpec((1, s, LANES), lambda pi, bi: (bi, 0, pi)),
                  pl.BlockSpec((1, n_sub, B_V_ROWS, ts), lambda pi, bi: (bi, 0, pi, 0)),
                  pl.BlockSpec((1,) + bias.shape[1:], lambda pi, bi: (pi, 0, 0))],
        out_specs=pl.BlockSpec((1, s, LANES), lambda pi, bi: (bi, 0, pi)),
        out_shape=jax.ShapeDtypeStruct((b, s, w), BF16),
        scratch_shapes=([pltpu.VMEM((n_sub // B_QUERY_SUBS, LANES, 2 * B_QUERY_SUBS * ts), BF16)]
                        + [pltpu.VMEM((KEY_PART, 2 * KEY_PART), F32)] * LOOKAHEAD),
        compiler_params=_params(2),
        name="attn_b",
    )(qt, k, vt, bias)


def _attn_c_kernel(qt_ref, k_ref, vt_ref, o_ref):
    n_blk, _, tq = qt_ref.shape[1:]
    items = [(qi, hd) for qi in range(n_blk) for hd in range(C_HEADS)]

    def qk(item):
        qi, hd = item
        rows = slice(hd * C_DIM, (hd + 1) * C_DIM)
        return _dot(k_ref[0, :, rows], qt_ref[0, qi, rows, :])

    scores = [qk(it) for it in items[:LOOKAHEAD]]
    outs = []
    for i, (qi, hd) in enumerate(items):
        s = scores.pop(0)
        if i + LOOKAHEAD < len(items):
            scores.append(qk(items[i + LOOKAHEAD]))
        _, acc = _online_softmax(s, None, None, vt_ref[0, hd * A_V_ROWS:(hd + 1) * A_V_ROWS, :])
        outs.append(_normalized(acc, C_DIM))
        if hd == C_HEADS - 1:
            o_ref[0, qi * tq:(qi + 1) * tq, :] = jnp.concatenate(outs, axis=0).T.astype(BF16)
            outs = []


def _attn_c(qt, mk, mvt):
    b, n_blk, w, tq = qt.shape
    n_mem = mk.shape[1]
    return pl.pallas_call(
        _attn_c_kernel,
        grid=(b,),
        in_specs=[pl.BlockSpec((1, n_blk, w, tq), lambda bi: (bi, 0, 0, 0)),
                  pl.BlockSpec((1, n_mem, w), lambda bi: (bi, 0, 0)),
                  pl.BlockSpec((1, mvt.shape[1], n_mem), lambda bi: (bi, 0, 0))],
        out_specs=pl.BlockSpec((1, n_blk * tq, w), lambda bi: (bi, 0, 0)),
        out_shape=jax.ShapeDtypeStruct((b, n_blk * tq, w), BF16),
        compiler_params=_params(1),
        name="attn_c",
    )(qt, mk, mvt)


def _mix_kernel(x_ref, g_ref, wgate_ref, bgate_ref, oa_ref, ob_ref, oc_ref, wbr_ref, wout_ref,
                o_ref):
    x = x_ref[...]
    d = x.shape[1]
    h = _rms(x, g_ref[...]).astype(BF16)
    y = None
    for n, br_ref in enumerate((oa_ref, ob_ref, oc_ref)):
        z = _dot(h, wgate_ref[:, n * d:(n + 1) * d]) + bgate_ref[:, n * d:(n + 1) * d]
        term = (1.0 / (1.0 + jnp.exp(-z))) * _dot(br_ref[...], wbr_ref[n])
        y = term if y is None else y + term
    o_ref[...] = x + _dot(y.astype(BF16), wout_ref[...])


def _mix_out(x, g, w_gate, b_gate, o_a, o_b, o_c, w_branch, w_out):
    t, d = x.shape
    tm = FFN_TILE
    tile = lambda i: (i, 0)
    br_spec = pl.BlockSpec((tm, BRANCH_W), tile)
    return pl.pallas_call(
        _mix_kernel,
        grid=(t // tm,),
        in_specs=[pl.BlockSpec((tm, d), tile), _const_spec((1, d)), _const_spec(w_gate.shape),
                  _const_spec(b_gate.shape), br_spec, br_spec, br_spec,
                  _const_spec(w_branch.shape), _const_spec(w_out.shape)],
        out_specs=pl.BlockSpec((tm, d), tile),
        out_shape=jax.ShapeDtypeStruct((t, d), F32),
        compiler_params=_params(1),
        name="mix_out",
    )(x, g, w_gate, b_gate, o_a, o_b, o_c, w_branch, w_out)


def _rope_tables(seq):
    half = HEAD_DIM // 2
    inv = ROPE_THETA ** (-jnp.arange(half, dtype=F32) / half)
    ang = jnp.arange(seq, dtype=F32)[:, None] * inv[None, :]
    return jnp.cos(ang).T, jnp.sin(ang).T


def _group_mean_matrix(group):
    idx = jnp.arange(MXU_TILE) // group
    return jnp.where(idx[:, None] == idx[None, :], 1.0 / group, 0.0).astype(BF16)


def _bias_rows(table):
    h = table.shape[0]
    far = jnp.broadcast_to(table[:, -1:], (h, TOKEN_TILE - REL_CLIP))
    near = table[:, ::-1]
    tail = jnp.broadcast_to(table[:, -1:], (h, 2 * TOKEN_TILE - far.shape[1] - near.shape[1]))
    return jnp.concatenate([far, near, tail], axis=1)[:, None, :]


def kernel(x, mem, ffn1_norm, ffn1_wg, ffn1_wu, ffn1_wd, mix_norm, w_in, a_q_norm, a_k_norm,
           a_lambda, a_sub_norm, b_q_norm, b_k_norm, b_rel_bias, mem_norm, w_mem_kv, c_q_norm,
           c_k_norm, w_gate, b_gate, w_branch, w_out, ffn2_norm, ffn2_wg, ffn2_wu, ffn2_wd,
           final_norm):
    b, s, d = x.shape
    n_mem = mem.shape[1]
    depth = ffn1_norm.shape[0]
    tm = TOKEN_TILE
    w = BRANCH_W
    assert s % tm == 0 and d % MXU_TILE == 0 and w_in.shape[2] == 7 * w

    tables = _rope_tables(s)
    g128 = _group_mean_matrix(C_DIM)
    row = lambda v: v.reshape(1, -1)
    heads = lambda v: jnp.tile(v, w // v.shape[0])

    xt = x.reshape(b * s, d)
    memt = mem.reshape(b * n_mem, d)
    for l in range(depth):
        lambda_init = 0.8 - 0.6 * math.exp(-0.3 * l)
        mk, mvt, (wg1, wu1, wd1) = _mem_proj(
            memt, row(mem_norm[l]), w_mem_kv[l], row(heads(c_k_norm[l])), g128, n_mem,
            to_bf16=(ffn1_wg[l], ffn1_wu[l], ffn1_wd[l]))
        xt, (wg2, wu2, wd2, w_in_b, w_gate_b, w_branch_b, w_out_b) = _ffn(
            xt, row(ffn1_norm[l]), wg1, wu1, wd1,
            to_bf16=(ffn2_wg[l], ffn2_wu[l], ffn2_wd[l], w_in[l], w_gate[l],
                     w_branch[l].reshape(-1, d), w_out[l]))
        w_branch_b = w_branch_b.reshape(w_branch[l].shape)

        col = lambda v: heads(v).reshape(w, 1)
        aqt, avt, bqt, bvt, cqt, ak, bk = _in_proj(
            xt, row(mix_norm[l]), w_in_b.T, tables,
            col(a_q_norm[l]), col(a_k_norm[l]), col(b_q_norm[l]), col(b_k_norm[l]),
            col(c_q_norm[l]), s)
        bias = _rel_bias(_bias_rows(b_rel_bias[l]))

        blocks = lambda a: a.reshape(b, s // tm, a.shape[1], tm)
        seq3 = lambda a: a.reshape(b, s, w)
        o_a = _attn_a(blocks(aqt), seq3(ak), blocks(avt), a_lambda[l], row(a_sub_norm[l]),
                      lambda_init)
        o_b = _attn_b(blocks(bqt), seq3(bk), blocks(bvt), bias)
        o_c = _attn_c(blocks(cqt), mk.reshape(b, n_mem, w), mvt)

        flat = lambda a: a.reshape(b * s, w)
        xt = _mix_out(xt, row(mix_norm[l]), w_gate_b, row(b_gate[l]),
                      flat(o_a), flat(o_b), flat(o_c), w_branch_b, w_out_b)

        xt, _ = _ffn(xt, row(ffn2_norm[l]), wg2, wu2, wd2, row(final_norm[l]))
    return xt.reshape(b, s, d)
```

```python
import functools
import math

import jax
import jax.numpy as jnp
from jax import lax
from jax.experimental import pallas as pl
from jax.experimental.pallas import tpu as pltpu

F32 = jnp.float32
BF16 = jnp.bfloat16

CHUNK = 64
HEAD_DIM = 64
A_HEADS = 4
B_HEADS = 8
B_LEFT_CHUNKS = 8
REL_CLIP = 128
C_HEADS = 4
C_DIM = 128
BRANCH_W = 512
ROPE_THETA = 10000.0
EPS = 1e-6
NEG = -1e30
LOG2E = math.log2(math.e)

LANES = 128
MXU_TILE = 256
BF16_ROWS = 16
TOKEN_TILE = 512
FFN_TILE = 1024
IN_TILES = 2
KEY_PART = MXU_TILE
LOOKAHEAD = 2
A_KEY_PART = MXU_TILE
A_LOOKAHEAD = 2
A_QUERY_SUBS = 4
B_QUERY_SUBS = 8
B_BAND_PARTS = 3
ONES_ROWS = BF16_ROWS
VMEM_LIMIT = 56 * 1024 * 1024

A_V_ROWS = 2 * HEAD_DIM + ONES_ROWS
B_V_ROWS = 2 * HEAD_DIM + ONES_ROWS

assert TOKEN_TILE == B_LEFT_CHUNKS * CHUNK


def _dot(a, b):
    return jnp.dot(a, b, preferred_element_type=F32)


def _dot_nt(a, b):
    return lax.dot_general(a, b, (((1,), (1,)), ((), ())), preferred_element_type=F32)


def _rms(x, g):
    return x * lax.rsqrt(jnp.mean(x * x, axis=-1, keepdims=True) + EPS) * g


def _group_rms(z, gmat_ref, gain):
    z2 = (z * z).astype(BF16)
    ms = jnp.concatenate(
        [_dot(z2[:, :MXU_TILE], gmat_ref[...]), _dot(z2[:, MXU_TILE:], gmat_ref[...])], axis=1)
    return z * lax.rsqrt(ms + EPS) * gain


def _heads_t(z, gain, group, scale, cos=None, sin=None):
    outs = []
    for g0 in range(0, z.shape[0], group):
        zz = z[g0:g0 + group]
        r = lax.rsqrt(jnp.mean(zz * zz, axis=0, keepdims=True) + EPS) * scale
        y = zz * r * gain[g0:g0 + group]
        if cos is not None:
            y1, y2 = y[:group // 2], y[group // 2:]
            y = jnp.concatenate([y1 * cos - y2 * sin, y2 * cos + y1 * sin], axis=0)
        outs.append(y)
    return jnp.concatenate(outs, axis=0)


def _with_ones(vt, head_rows):
    ones = jnp.ones((ONES_ROWS, vt.shape[1]), vt.dtype)
    parts = []
    for r0 in range(0, vt.shape[0], head_rows):
        parts += [vt[r0:r0 + head_rows], ones]
    return jnp.concatenate(parts, axis=0)


def _const_spec(shape):
    nd = len(shape)
    return pl.BlockSpec(shape, lambda *_: (0,) * nd, pipeline_mode=pl.Buffered(1))


def _cast_specs(arrays, steps):
    specs = []
    for a in arrays:
        n_blocks = math.gcd(steps, a.shape[0] // BF16_ROWS)
        rows, repeat = a.shape[0] // n_blocks, steps // n_blocks
        specs.append(pl.BlockSpec((rows, a.shape[1]), lambda i, r=repeat: (i // r, 0)))
    return specs


def _cast_side_job(srcs, dsts):
    for src, dst in zip(srcs, dsts):
        dst[...] = src[...].astype(BF16)


def _params(n_axes):
    return pltpu.CompilerParams(
        dimension_semantics=("arbitrary",) * n_axes, vmem_limit_bytes=VMEM_LIMIT)


def _ffn_chunks(d_ff):
    step = 3 * MXU_TILE
    return [(c, min(c + step, d_ff)) for c in range(0, d_ff, step)]


def _ffn_kernel(x_ref, g_ref, wg_ref, wu_ref, wd_ref, *rest, final_norm, n_cast):
    rest = list(rest)
    fg_ref = rest.pop(0) if final_norm else None
    cast_in, o_ref, cast_out, act_ref = (rest[:n_cast], rest[n_cast],
                                         rest[n_cast + 1:2 * n_cast + 1], rest[-1])
    _cast_side_job(cast_in, cast_out)
    x = x_ref[...]
    h = _rms(x, g_ref[...]).astype(BF16)
    for c0, c1 in _ffn_chunks(wg_ref.shape[1]):
        g = _dot(h, wg_ref[:, c0:c1])
        u = _dot(h, wu_ref[:, c0:c1])
        act_ref[:, c0:c1] = (g * (1.0 / (1.0 + jnp.exp(-g))) * u).astype(BF16)
    out = x + 0.5 * _dot(act_ref[...], wd_ref[...])
    if final_norm:
        out = _rms(out, fg_ref[...])
    o_ref[...] = out


def _ffn(x, g, wg, wu, wd, final_g=None, to_bf16=()):
    t, d = x.shape
    f = wg.shape[1]
    tm = FFN_TILE
    steps = t // tm
    tile = lambda i: (i, 0)
    in_specs = [pl.BlockSpec((tm, d), tile), _const_spec((1, d)),
                _const_spec((d, f)), _const_spec((d, f)), _const_spec((f, d))]
    args = [x, g, wg, wu, wd]
    if final_g is not None:
        in_specs.append(_const_spec((1, d)))
        args.append(final_g)
    cast_specs = _cast_specs(to_bf16, steps)
    outs = pl.pallas_call(
        functools.partial(_ffn_kernel, final_norm=final_g is not None, n_cast=len(to_bf16)),
        grid=(steps,),
        in_specs=in_specs + cast_specs,
        out_specs=[pl.BlockSpec((tm, d), tile)] + cast_specs,
        out_shape=([jax.ShapeDtypeStruct((t, d), F32)]
                   + [jax.ShapeDtypeStruct(a.shape, BF16) for a in to_bf16]),
        scratch_shapes=[pltpu.VMEM((tm, f), BF16)],
        compiler_params=_params(1),
        name="ffn_final" if final_g is not None else "ffn",
    )(*args, *to_bf16)
    return outs[0], outs[1:]


def _in_kernel(x_ref, g_ref, wt_ref, cost_ref, sint_ref,
               gaq_ref, gak_ref, gbq_ref, gbk_ref, gcq_ref,
               aqt_ref, avt_ref, bqt_ref, bvt_ref, cqt_ref, ak_ref, bk_ref):
    w = BRANCH_W
    tm = TOKEN_TILE
    qk_scale = HEAD_DIM ** -0.5 * LOG2E
    for sub in range(IN_TILES):
        rows = slice(sub * tm, (sub + 1) * tm)
        h = _rms(x_ref[rows, :], g_ref[...]).astype(BF16)
        cos, sin = cost_ref[:, rows], sint_ref[:, rows]

        def seg_t(i):
            return _dot_nt(wt_ref[i * w:(i + 1) * w, :], h)

        aqt_ref[sub] = _heads_t(seg_t(0), gaq_ref[...], HEAD_DIM, qk_scale, cos, sin
                                ).astype(BF16)
        ak_ref[rows, :] = _heads_t(seg_t(1), gak_ref[...], HEAD_DIM, 1.0, cos, sin
                                   ).T.astype(BF16)
        avt_ref[sub] = _with_ones(seg_t(2).astype(BF16), 2 * HEAD_DIM)
        bqt_ref[sub] = _heads_t(seg_t(3), gbq_ref[...], HEAD_DIM, qk_scale).astype(BF16)
        bk_ref[rows, :] = _heads_t(seg_t(4), gbk_ref[...], HEAD_DIM, 1.0).T.astype(BF16)
        bvt_ref[sub] = _with_ones(seg_t(5).astype(BF16), 2 * HEAD_DIM)
        cqt_ref[sub] = _heads_t(seg_t(6), gcq_ref[...], C_DIM, C_DIM ** -0.5 * LOG2E
                                ).astype(BF16)


def _in_proj(x, g, w_t, tables, gaq, gak, gbq, gbk, gcq, seq):
    t, d = x.shape
    tm = TOKEN_TILE
    ts = IN_TILES * tm
    n_tiles = t // tm
    n_seq_steps = seq // ts
    w = BRANCH_W
    cos_t, sin_t = tables
    tile = lambda i: (i, 0)
    half = HEAD_DIM // 2
    in_specs = [pl.BlockSpec((ts, d), tile), _const_spec((1, d)), _const_spec(w_t.shape),
                pl.BlockSpec((half, ts), lambda i: (0, i % n_seq_steps)),
                pl.BlockSpec((half, ts), lambda i: (0, i % n_seq_steps))]
    in_specs += [_const_spec((w, 1))] * 5
    t_rows = [w, A_HEADS * A_V_ROWS, w, B_HEADS // 2 * B_V_ROWS, w]
    return pl.pallas_call(
        _in_kernel,
        grid=(t // ts,),
        in_specs=in_specs,
        out_specs=([pl.BlockSpec((IN_TILES, r, tm), lambda i: (i, 0, 0)) for r in t_rows]
                   + [pl.BlockSpec((ts, w), tile)] * 2),
        out_shape=([jax.ShapeDtypeStruct((n_tiles, r, tm), BF16) for r in t_rows]
                   + [jax.ShapeDtypeStruct((t, w), BF16)] * 2),
        compiler_params=_params(1),
        name="in_proj",
    )(x, g, w_t, cos_t, sin_t, gaq, gak, gbq, gbk, gcq)


def _mem_kernel(m_ref, g_ref, wkv_ref, gk_ref, g128_ref, *rest, n_cast):
    cast_in, (mk_ref, mvt_ref), cast_out = rest[:n_cast], rest[n_cast:n_cast + 2], rest[n_cast + 2:]
    _cast_side_job(cast_in, cast_out)
    w = BRANCH_W
    h = _rms(m_ref[...], g_ref[...]).astype(BF16)
    wkv = wkv_ref[...].astype(BF16)
    mk_ref[...] = _group_rms(_dot(h, wkv[:, :w]), g128_ref, gk_ref[...]).astype(BF16)
    mvt_ref[0] = _with_ones(_dot(h, wkv[:, w:]).T.astype(BF16), C_DIM)


def _mem_proj(mem, g, w_kv, gk, g128, n_mem, to_bf16=()):
    t, d = mem.shape
    w = BRANCH_W
    v_rows = C_HEADS * A_V_ROWS
    steps = t // n_mem
    tile = lambda i: (i, 0)
    cast_specs = _cast_specs(to_bf16, steps)
    outs = pl.pallas_call(
        functools.partial(_mem_kernel, n_cast=len(to_bf16)),
        grid=(steps,),
        in_specs=[pl.BlockSpec((n_mem, d), tile), _const_spec((1, d)), _const_spec(w_kv.shape),
                  _const_spec((1, w)), _const_spec((MXU_TILE, MXU_TILE))] + cast_specs,
        out_specs=[pl.BlockSpec((n_mem, w), tile),
                   pl.BlockSpec((1, v_rows, n_mem), lambda i: (i, 0, 0))] + cast_specs,
        out_shape=([jax.ShapeDtypeStruct((t, w), BF16),
                    jax.ShapeDtypeStruct((steps, v_rows, n_mem), BF16)]
                   + [jax.ShapeDtypeStruct(a.shape, BF16) for a in to_bf16]),
        compiler_params=_params(1),
        name="mem_proj",
    )(mem, g, w_kv, gk, g128, *to_bf16)
    return outs[0], outs[1], outs[2:]


def _bias_kernel(r_ref, o_ref):
    nq = TOKEN_TILE
    nk = 2 * nq
    hk = KEY_PART
    qc = lax.broadcasted_iota(jnp.int32, (nq, nk), 0) // CHUNK + B_LEFT_CHUNKS
    kc = lax.broadcasted_iota(jnp.int32, (nq, nk), 1) // CHUNK
    valid = (qc >= kc) & (qc - kc <= B_LEFT_CHUNKS)
    for hh in range(2):
        t = pltpu.roll(jnp.broadcast_to(r_ref[hh], (nq, nk)), 0, 1, stride=1, stride_axis=0)
        t = jnp.where(valid, t * LOG2E, NEG).T
        o_ref[0, :, hh * hk:(hh + 1) * hk] = t[:B_BAND_PARTS * hk, :hk]


def _rel_bias(rows):
    h, _, nk = rows.shape
    tile = (B_BAND_PARTS * KEY_PART, 2 * KEY_PART)
    return pl.pallas_call(
        _bias_kernel,
        grid=(h // 2,),
        in_specs=[pl.BlockSpec((2, 1, nk), lambda i: (i, 0, 0))],
        out_specs=pl.BlockSpec((1,) + tile, lambda i: (i, 0, 0)),
        out_shape=jax.ShapeDtypeStruct((h // 2,) + tile, F32),
        compiler_params=_params(1),
        name="rel_bias",
    )(rows)


def _split_queries(qt):
    row = lax.broadcasted_iota(jnp.int32, qt.shape, 0)
    zero = jnp.zeros_like(qt)
    return jnp.concatenate(
        [jnp.where(row < HEAD_DIM, qt, zero), jnp.where(row >= HEAD_DIM, qt, zero)], axis=1)


def _rows(start, size):
    if isinstance(start, int):
        return pl.ds(start, size)
    return pl.ds(pl.multiple_of(start, size), size)


def _column_max(s):
    return jnp.max(s, axis=0, keepdims=True)


def _online_softmax(s, m_prev, acc_prev, vt, s_max=None):
    if s_max is None:
        s_max = _column_max(s)
    m_new = s_max if m_prev is None else jnp.maximum(m_prev, s_max)
    pv = _dot(vt, jnp.exp2(s - m_new).astype(BF16))
    if acc_prev is None:
        return m_new, pv
    return m_new, jnp.exp2(m_prev - m_new) * acc_prev + pv


def _normalized(acc, rows):
    return acc[:rows] / acc[rows:rows + 1]


def _attn_a_kernel(qt_ref, k_ref, vt_ref, lam_ref, gain_ref, o_ref, q12_ref, *scratch,
                   lambda_init):
    n_sub, _, ts = qt_ref.shape[1:]
    subs = A_QUERY_SUBS
    n_blk = n_sub // subs
    n_chains = 2 * subs
    look = A_LOOKAHEAD
    score_refs = scratch[:look]
    smax_refs = scratch[look:2 * look]
    state_refs = [scratch[2 * look + 2 * c:2 * look + 2 * c + 2] for c in range(n_chains)]
    hk = A_KEY_PART
    parts = ts // hk
    assert parts == 2
    tail_base = n_chains * ts
    for blk in range(n_blk):
        q12 = [_split_queries(qt_ref[0, blk * subs + h]) for h in range(subs)]
        tails = [jnp.concatenate([q[:, ts - hk:ts], q[:, 2 * ts - hk:]], axis=1) for q in q12]
        q12_ref[blk] = jnp.concatenate(q12 + tails, axis=1)
    lp = lam_ref[...]
    lam = (jnp.exp(jnp.sum(lp[0:1] * lp[1:2], axis=1, keepdims=True))
           - jnp.exp(jnp.sum(lp[2:3] * lp[3:4], axis=1, keepdims=True)) + lambda_init)


    def qk(item):
        qi, jb, kp, ch, _, tail = item
        c0 = (tail_base if tail else 0) + ch * ts
        kb = k_ref[0, _rows(jb * ts + kp * hk, hk), :]
        s = _dot(kb, q12_ref[qi, :, c0:c0 + ts])
        return s, _column_max(s)

    def reset(h):
        for m_ref, acc_ref in state_refs[2 * h:2 * h + 2]:
            m_ref[...] = jnp.full(m_ref.shape, NEG, F32)
            acc_ref[...] = jnp.zeros(acc_ref.shape, F32)

    def finish(qi, h):
        o1, o2 = (_normalized(acc_ref[...], 2 * HEAD_DIM)
                  for _, acc_ref in state_refs[2 * h:2 * h + 2])
        ot = o1 - lam * o2
        ot = ot * lax.rsqrt(jnp.mean(ot * ot, axis=0, keepdims=True) + EPS)
        o_ref[0, _rows((qi * subs + h) * ts, ts), :] = (
            ot.T * gain_ref[...] * (1.0 - lambda_init)).astype(BF16)
        reset(h)

    def run(seq, following, after=()):
        scores = [(score_refs[i][...], smax_refs[i][...]) for i in range(look)]
        for i, (qi, jb, kp, ch, masked, tail) in enumerate(seq):
            s, s_max = scores.pop(0)
            if masked:
                key_chunk = (lax.broadcasted_iota(jnp.int32, (hk, ts), 0) + kp * hk) // CHUNK
                query = lax.broadcasted_iota(jnp.int32, (hk, ts), 1)
                query_chunk = (query % hk + ts - hk if tail else query) // CHUNK
                s = jnp.where(key_chunk <= query_chunk, s, NEG)
                s_max = None
            nxt = i + look
            if nxt < len(seq):
                scores.append(qk(seq[nxt]))
            else:
                n = nxt - len(seq)
                score_refs[n][...], smax_refs[n][...] = qk(following[n])
            vt = vt_ref[0, jb, :, kp * hk:(kp + 1) * hk]
            if tail:
                refs = state_refs[2 * ch:2 * ch + 2]
                m_new, acc_new = _online_softmax(
                    s, jnp.concatenate([m_ref[:, ts - hk:] for m_ref, _ in refs], axis=1),
                    jnp.concatenate([acc_ref[:, ts - hk:] for _, acc_ref in refs], axis=1),
                    vt, s_max)
                for c, (m_ref, acc_ref) in enumerate(refs):
                    m_ref[:, ts - hk:] = m_new[:, c * hk:(c + 1) * hk]
                    acc_ref[:, ts - hk:] = acc_new[:, c * hk:(c + 1) * hk]
            else:
                m_ref, acc_ref = state_refs[ch]
                m_ref[...], acc_ref[...] = _online_softmax(s, m_ref[...], acc_ref[...], vt, s_max)
            if i in after:
                finish(qi, after[i])

    def key_block(qi, jb):
        return [(qi, jb, kp, ch, False, False) for kp in range(parts) for ch in range(n_chains)]

    def own_blocks(qi):
        seq, after = [], {}
        for kb in range(subs):
            for kp in range(parts):
                for h in range(kb, subs):
                    if h == kb and kp == parts - 1:
                        seq.append((qi, subs * qi + kb, kp, h, True, True))
                        after[len(seq) - 1] = kb
                    else:
                        seq += [(qi, subs * qi + kb, kp, 2 * h + c, h == kb, False)
                                for c in range(2)]
        return seq, after

    def query_block(qi, carry):
        def body(jb, c):
            run(key_block(qi, jb), key_block(qi, jb + 1))
            return c

        lax.fori_loop(0, subs * qi, body, 0)
        seq, after = own_blocks(qi)
        nxt_qi = jnp.minimum(qi + 1, n_blk - 1)
        run(seq, key_block(nxt_qi, 0), after)
        return carry

    for h in range(subs):
        reset(h)
    for i, item in enumerate(key_block(0, 0)[:look]):
        score_refs[i][...], smax_refs[i][...] = qk(item)
    if n_blk == 1:
        query_block(0, 0)
    else:
        lax.fori_loop(0, n_blk, query_block, 0)


def _attn_a(qt, k, vt, lam_p, gain, lambda_init):
    b, n_sub, w, ts = qt.shape
    s = n_sub * ts
    n_chains = 2 * A_QUERY_SUBS
    return pl.pallas_call(
        functools.partial(_attn_a_kernel, lambda_init=lambda_init),
        grid=(b, w // LANES),
        in_specs=[pl.BlockSpec((1, n_sub, LANES, ts), lambda bi, hi: (bi, 0, hi, 0)),
                  pl.BlockSpec((1, s, LANES), lambda bi, hi: (bi, 0, hi)),
                  pl.BlockSpec((1, n_sub, A_V_ROWS, ts), lambda bi, hi: (bi, 0, hi, 0)),
                  _const_spec(lam_p.shape), _const_spec((1, LANES))],
        out_specs=pl.BlockSpec((1, s, LANES), lambda bi, hi: (bi, 0, hi)),
        out_shape=jax.ShapeDtypeStruct((b, s, w), BF16),
        scratch_shapes=(
            [pltpu.VMEM((n_sub // A_QUERY_SUBS, LANES, (n_chains + A_QUERY_SUBS) * ts), BF16)]
            + [pltpu.VMEM((A_KEY_PART, ts), F32)] * A_LOOKAHEAD
            + [pltpu.VMEM((1, ts), F32)] * A_LOOKAHEAD
            + [pltpu.VMEM((1, ts), F32), pltpu.VMEM((A_V_ROWS, ts), F32)] * n_chains),
        compiler_params=_params(2),
        name="attn_a",
    )(qt, k, vt, lam_p, gain)


def _attn_b_kernel(qt_ref, k_ref, vt_ref, bias_ref, o_ref, q12_ref, *score_refs):
    n_sub, _, ts = qt_ref.shape[1:]
    hk = KEY_PART
    per_sub = ts // hk
    n_chains = B_QUERY_SUBS * per_sub
    n_blk = n_sub // B_QUERY_SUBS
    lead = B_BAND_PARTS - 1
    assert lead * hk == B_LEFT_CHUNKS * CHUNK and LOOKAHEAD <= n_chains
    for blk in range(n_blk):
        cols = []
        for c in range(n_chains):
            q12 = _split_queries(qt_ref[0, blk * B_QUERY_SUBS + c // per_sub])
            c0 = (c % per_sub) * hk
            cols += [q12[:, c0:c0 + hk], q12[:, ts + c0:ts + c0 + hk]]
        q12_ref[blk] = jnp.concatenate(cols, axis=1)

    items = [(c + d, c) for d in range(lead, -1, -1) for c in range(n_chains)]

    def source(qi, wp):
        part = qi * n_chains + wp - lead
        return jnp.maximum(part, 0) // per_sub, (wp - lead) % per_sub * hk

    def qk(qi, item):
        wp, c = item
        blk, r0 = source(qi, wp)
        cols = slice(c * 2 * hk, (c + 1) * 2 * hk)
        d = wp - c
        kb = k_ref[0, pl.ds(pl.multiple_of(blk * ts + r0, hk), hk), :]
        s = _dot(kb, q12_ref[qi, :, cols]) + bias_ref[0, d * hk:(d + 1) * hk, :]
        if wp < lead:
            s = jnp.where(qi > 0, s, NEG)
        return s

    def query_block(qi, carry):
        scores = [r[...] for r in score_refs]
        state = [(None, None)] * n_chains
        for i, (wp, c) in enumerate(items):
            s = scores.pop(0)
            nxt = i + LOOKAHEAD
            if nxt < len(items):
                scores.append(qk(qi, items[nxt]))
            else:
                score_refs[nxt - len(items)][...] = qk(
                    jnp.minimum(qi + 1, n_blk - 1), items[nxt - len(items)])
            blk, r0 = source(qi, wp)
            state[c] = _online_softmax(s, state[c][0], state[c][1],
                                       vt_ref[0, blk, :, r0:r0 + hk])
        outs = []
        for _, acc in state:
            heads = [acc[hh * HEAD_DIM:(hh + 1) * HEAD_DIM, hh * hk:(hh + 1) * hk]
                     / acc[2 * HEAD_DIM:2 * HEAD_DIM + 1, hh * hk:(hh + 1) * hk]
                     for hh in range(2)]
            outs.append(jnp.concatenate(heads, axis=0))
        for h in range(B_QUERY_SUBS):
            ot = jnp.concatenate(outs[h * per_sub:(h + 1) * per_sub], axis=1)
            o_ref[0, pl.ds(pl.multiple_of((qi * B_QUERY_SUBS + h) * ts, ts), ts), :] = (
                ot.T.astype(BF16))
        return carry

    for i, r in enumerate(score_refs):
        r[...] = qk(0, items[i])
    lax.fori_loop(0, n_blk, query_block, 0)


def _attn_b(qt, k, vt, bias):
    b, n_sub, w, ts = qt.shape
    s = n_sub * ts
    return pl.pallas_call(
        _attn_b_kernel,
        grid=(w // LANES, b),
        in_specs=[pl.BlockSpec((1, n_sub, LANES, ts), lambda pi, bi: (bi, 0, pi, 0)),
                  pl.BlockSpec((1, s, LANES), lambda pi, bi: (bi, 0, pi)),
                  pl.BlockSpec((1, n_sub, B_V_ROWS, ts), lambda pi, bi: (bi, 0, pi, 0)),
                  pl.BlockSpec((1,) + bias.shape[1:], lambda pi, bi: (pi, 0, 0))],
        out_specs=pl.BlockSpec((1, s, LANES), lambda pi, bi: (bi, 0, pi)),
        out_shape=jax.ShapeDtypeStruct((b, s, w), BF16),
        scratch_shapes=([pltpu.VMEM((n_sub // B_QUERY_SUBS, LANES, 2 * B_QUERY_SUBS * ts), BF16)]
                        + [pltpu.VMEM((KEY_PART, 2 * KEY_PART), F32)] * LOOKAHEAD),
        compiler_params=_params(2),
        name="attn_b",
    )(qt, k, vt, bias)


def _attn_c_kernel(qt_ref, k_ref, vt_ref, o_ref):
    n_blk, _, tq = qt_ref.shape[1:]
    items = [(qi, hd) for qi in range(n_blk) for hd in range(C_HEADS)]

    def qk(item):
        qi, hd = item
        rows = slice(hd * C_DIM, (hd + 1) * C_DIM)
        return _dot(k_ref[0, :, rows], qt_ref[0, qi, rows, :])

    scores = [qk(it) for it in items[:LOOKAHEAD]]
    outs = []
    for i, (qi, hd) in enumerate(items):
        s = scores.pop(0)
        if i + LOOKAHEAD < len(items):
            scores.append(qk(items[i + LOOKAHEAD]))
        _, acc = _online_softmax(s, None, None, vt_ref[0, hd * A_V_ROWS:(hd + 1) * A_V_ROWS, :])
        outs.append(_normalized(acc, C_DIM))
        if hd == C_HEADS - 1:
            o_ref[0, qi * tq:(qi + 1) * tq, :] = jnp.concatenate(outs, axis=0).T.astype(BF16)
            outs = []


def _attn_c(qt, mk, mvt):
    b, n_blk, w, tq = qt.shape
    n_mem = mk.shape[1]
    return pl.pallas_call(
        _attn_c_kernel,
        grid=(b,),
        in_specs=[pl.BlockSpec((1, n_blk, w, tq), lambda bi: (bi, 0, 0, 0)),
                  pl.BlockSpec((1, n_mem, w), lambda bi: (bi, 0, 0)),
                  pl.BlockSpec((1, mvt.shape[1], n_mem), lambda bi: (bi, 0, 0))],
        out_specs=pl.BlockSpec((1, n_blk * tq, w), lambda bi: (bi, 0, 0)),
        out_shape=jax.ShapeDtypeStruct((b, n_blk * tq, w), BF16),
        compiler_params=_params(1),
        name="attn_c",
    )(qt, mk, mvt)


def _mix_kernel(x_ref, g_ref, wgate_ref, bgate_ref, oa_ref, ob_ref, oc_ref, wbr_ref, wout_ref,
                o_ref):
    x = x_ref[...]
    d = x.shape[1]
    h = _rms(x, g_ref[...]).astype(BF16)
    y = None
    for n, br_ref in enumerate((oa_ref, ob_ref, oc_ref)):
        z = _dot(h, wgate_ref[:, n * d:(n + 1) * d]) + bgate_ref[:, n * d:(n + 1) * d]
        term = (1.0 / (1.0 + jnp.exp(-z))) * _dot(br_ref[...], wbr_ref[n])
        y = term if y is None else y + term
    o_ref[...] = x + _dot(y.astype(BF16), wout_ref[...])


def _mix_out(x, g, w_gate, b_gate, o_a, o_b, o_c, w_branch, w_out):
    t, d = x.shape
    tm = FFN_TILE
    tile = lambda i: (i, 0)
    br_spec = pl.BlockSpec((tm, BRANCH_W), tile)
    return pl.pallas_call(
        _mix_kernel,
        grid=(t // tm,),
        in_specs=[pl.BlockSpec((tm, d), tile), _const_spec((1, d)), _const_spec(w_gate.shape),
                  _const_spec(b_gate.shape), br_spec, br_spec, br_spec,
                  _const_spec(w_branch.shape), _const_spec(w_out.shape)],
        out_specs=pl.BlockSpec((tm, d), tile),
        out_shape=jax.ShapeDtypeStruct((t, d), F32),
        compiler_params=_params(1),
        name="mix_out",
    )(x, g, w_gate, b_gate, o_a, o_b, o_c, w_branch, w_out)


def _rope_tables(seq):
    half = HEAD_DIM // 2
    inv = ROPE_THETA ** (-jnp.arange(half, dtype=F32) / half)
    ang = jnp.arange(seq, dtype=F32)[:, None] * inv[None, :]
    return jnp.cos(ang).T, jnp.sin(ang).T


def _group_mean_matrix(group):
    idx = jnp.arange(MXU_TILE) // group
    return jnp.where(idx[:, None] == idx[None, :], 1.0 / group, 0.0).astype(BF16)


def _bias_rows(table):
    h = table.shape[0]
    far = jnp.broadcast_to(table[:, -1:], (h, TOKEN_TILE - REL_CLIP))
    near = table[:, ::-1]
    tail = jnp.broadcast_to(table[:, -1:], (h, 2 * TOKEN_TILE - far.shape[1] - near.shape[1]))
    return jnp.concatenate([far, near, tail], axis=1)[:, None, :]


def kernel(x, mem, ffn1_norm, ffn1_wg, ffn1_wu, ffn1_wd, mix_norm, w_in, a_q_norm, a_k_norm,
           a_lambda, a_sub_norm, b_q_norm, b_k_norm, b_rel_bias, mem_norm, w_mem_kv, c_q_norm,
           c_k_norm, w_gate, b_gate, w_branch, w_out, ffn2_norm, ffn2_wg, ffn2_wu, ffn2_wd,
           final_norm):
    b, s, d = x.shape
    n_mem = mem.shape[1]
    depth = ffn1_norm.shape[0]
    tm = TOKEN_TILE
    w = BRANCH_W
    assert s % tm == 0 and d % MXU_TILE == 0 and w_in.shape[2] == 7 * w

    tables = _rope_tables(s)
    g128 = _group_mean_matrix(C_DIM)
    row = lambda v: v.reshape(1, -1)
    heads = lambda v: jnp.tile(v, w // v.shape[0])

    xt = x.reshape(b * s, d)
    memt = mem.reshape(b * n_mem, d)
    for l in range(depth):
        lambda_init = 0.8 - 0.6 * math.exp(-0.3 * l)
        mk, mvt, (wg1, wu1, wd1) = _mem_proj(
            memt, row(mem_norm[l]), w_mem_kv[l], row(heads(c_k_norm[l])), g128, n_mem,
            to_bf16=(ffn1_wg[l], ffn1_wu[l], ffn1_wd[l]))
        xt, (wg2, wu2, wd2, w_in_b, w_gate_b, w_branch_b, w_out_b) = _ffn(
            xt, row(ffn1_norm[l]), wg1, wu1, wd1,
            to_bf16=(ffn2_wg[l], ffn2_wu[l], ffn2_wd[l], w_in[l], w_gate[l],
                     w_branch[l].reshape(-1, d), w_out[l]))
        w_branch_b = w_branch_b.reshape(w_branch[l].shape)

        col = lambda v: heads(v).reshape(w, 1)
        aqt, avt, bqt, bvt, cqt, ak, bk = _in_proj(
            xt, row(mix_norm[l]), w_in_b.T, tables,
            col(a_q_norm[l]), col(a_k_norm[l]), col(b_q_norm[l]), col(b_k_norm[l]),
            col(c_q_norm[l]), s)
        bias = _rel_bias(_bias_rows(b_rel_bias[l]))

        blocks = lambda a: a.reshape(b, s // tm, a.shape[1], tm)
        seq3 = lambda a: a.reshape(b, s, w)
        o_a = _attn_a(blocks(aqt), seq3(ak), blocks(avt), a_lambda[l], row(a_sub_norm[l]),
                      lambda_init)
        o_b = _attn_b(blocks(bqt), seq3(bk), blocks(bvt), bias)
        o_c = _attn_c(blocks(cqt), mk.reshape(b, n_mem, w), mvt)

        flat = lambda a: a.reshape(b * s, w)
        xt = _mix_out(xt, row(mix_norm[l]), w_gate_b, row(b_gate[l]),
                      flat(o_a), flat(o_b), flat(o_c), w_branch_b, w_out_b)

        xt, _ = _ffn(xt, row(ffn2_norm[l]), wg2, wu2, wd2, row(final_norm[l]))
    return xt.reshape(b, s, d)
```
